```python
import math
import jax
import jax.numpy as jnp
from jax import lax
import numpy as np

D_MODEL = 1024
BATCH = 8
SEQ = 2048
DEPTH = 2

CHUNK = 64

D_A = D_MODEL // 2
S5_H = 16
S5_G = D_A // S5_H
S5_P = 64
DT_MIN = 1e-3
DT_MAX = 1e-1
D_B = D_MODEL - D_A
POOL_WINDOWS = (2, 4, 8, 16)
POOL_GROUPS = len(POOL_WINDOWS)
POOL_C = D_B // POOL_GROUPS
SB_HEADS = 16
SB_HEAD_DIM = D_MODEL // SB_HEADS
SB_BLOCK = 128
N_EXPERTS = 16
N_EXPERT_GROUPS = 4
EXPERTS_PER_GROUP = N_EXPERTS // N_EXPERT_GROUPS
TOP_K = 2
D_EXPERT = 512
PLE_DIM = 256
DEEPNORM_ALPHA = (2 * DEPTH) ** 0.25
DEEPNORM_BETA = (8 * DEPTH) ** -0.25
LN_EPS = 1e-5
N_EVEN = (DEPTH + 1) // 2
N_ODD = DEPTH // 2

kernel_name = 'hybrid_s5_pool_stickbreak_moe_trunk'

F32 = jnp.float32


def layer_norm(x, g, b):
    x32 = x.astype(F32)
    mu = jnp.mean(x32, axis=-1, keepdims=True)
    var = jnp.mean(jnp.square(x32 - mu), axis=-1, keepdims=True)
    return ((x32 - mu) * lax.rsqrt(var + LN_EPS) * g.astype(F32) + b.astype(F32)).astype(x.dtype)


def _complex_linear_combine(e1, e2):
    a1r, a1i, b1r, b1i = e1
    a2r, a2i, b2r, b2i = e2
    ar = a2r * a1r - a2i * a1i
    ai = a2r * a1i + a2i * a1r
    br = a2r * b1r - a2i * b1i + b2r
    bi = a2r * b1i + a2i * b1r + b2i
    return (ar, ai, br, bi)


def s5_mixer(u, lam_re, lam_im, log_dt, b_re, b_im, c_re, c_im, d_skip, w_glu):
    bsz, s, _ = u.shape
    u32 = u.astype(F32).reshape(bsz, s, S5_G, S5_H)
    lam_re = lam_re.astype(F32)
    lam_im = lam_im.astype(F32)
    dt = jnp.exp(log_dt.astype(F32))[:, None]
    mag = jnp.exp(lam_re * dt)
    ang = lam_im * dt
    lb_re = mag * jnp.cos(ang)
    lb_im = mag * jnp.sin(ang)
    den = lam_re * lam_re + lam_im * lam_im
    num_re = lb_re - 1.0
    f_re = (num_re * lam_re + lb_im * lam_im) / den
    f_im = (lb_im * lam_re - num_re * lam_im) / den
    b_re = b_re.astype(F32)
    b_im = b_im.astype(F32)
    bb_re = f_re[..., None] * b_re - f_im[..., None] * b_im
    bb_im = f_re[..., None] * b_im + f_im[..., None] * b_re
    bu_re = jnp.einsum('bsgh,gph->sbgp', u32, bb_re)
    bu_im = jnp.einsum('bsgh,gph->sbgp', u32, bb_im)
    a_re = jnp.broadcast_to(lb_re[None, None], (s, 1, S5_G, S5_P))
    a_im = jnp.broadcast_to(lb_im[None, None], (s, 1, S5_G, S5_P))
    _, _, st_re, st_im = lax.associative_scan(
        _complex_linear_combine, (a_re, a_im, bu_re, bu_im), axis=0)
    y = (jnp.einsum('sbgp,ghp->bsgh', st_re, c_re.astype(F32))
         - jnp.einsum('sbgp,ghp->bsgh', st_im, c_im.astype(F32))
         + d_skip.astype(F32) * u32)
    y = jax.nn.gelu(y.reshape(bsz, s, D_A))
    out = y * jax.nn.sigmoid(y @ w_glu.astype(F32))
    return out.astype(u.dtype)


def pool_mixer(v, pool_w, pool_scale):
    bsz, s, _ = v.shape
    v32 = v.astype(F32).reshape(bsz, s, POOL_GROUPS, POOL_C)
    csum = jnp.cumsum(v32, axis=1)
    t = jnp.arange(s)
    means = []
    for gi, w in enumerate(POOL_WINDOWS):
        c = csum[:, :, gi]
        lagged = jnp.pad(c, ((0, 0), (w, 0), (0, 0)))[:, :s]
        count = jnp.minimum(t + 1, w).astype(F32)[None, :, None]
        means.append((c - lagged) / count)
    pooled = jnp.stack(means, axis=2) - v32
    mixed = jnp.einsum('bsgc,gcd->bsgd', pooled, pool_w.astype(F32))
    return (mixed.reshape(bsz, s, D_B) * pool_scale.astype(F32)).astype(v.dtype)


def stick_breaking_attention(qkv):
    bsz, s, _ = qkv.shape
    q, k, v = jnp.split(qkv.astype(F32), 3, axis=-1)
    to_heads = lambda z: z.reshape(bsz, s, SB_HEADS, SB_HEAD_DIM).transpose(0, 2, 1, 3)
    q, k, v = to_heads(q), to_heads(k), to_heads(v)
    scale = SB_HEAD_DIM ** -0.5
    outs = []
    for blk in range(s // SB_BLOCK):
        q0 = blk * SB_BLOCK
        end = q0 + SB_BLOCK
        z = jnp.einsum('bhqd,bhkd->bhqk', q[:, :, q0:end], k[:, :, :end]) * scale
        t_pos = q0 + jnp.arange(SB_BLOCK)
        s_pos = jnp.arange(end)
        causal = s_pos[None, :] < t_pos[:, None]
        log_keep = jnp.where(causal, jax.nn.log_sigmoid(-z), 0.0)
        log_after = lax.cumsum(log_keep, axis=3, reverse=True) - log_keep
        wts = jnp.where(causal, jnp.exp(jax.nn.log_sigmoid(z) + log_after), 0.0)
        outs.append(jnp.einsum('bhqk,bhkd->bhqd', wts, v[:, :, :end]))
    o = jnp.concatenate(outs, axis=2)
    return o.transpose(0, 2, 1, 3).reshape(bsz, s, D_MODEL).astype(qkv.dtype)


def grouped_moe(x, router_w, router_bias, w1, w3, w2):
    bsz, s, d = x.shape
    xt = x.reshape(-1, d)
    scores = jax.nn.sigmoid((xt @ router_w).astype(F32))
    sel = scores + router_bias.astype(F32)
    grp = sel.reshape(-1, N_EXPERT_GROUPS, EXPERTS_PER_GROUP)
    grp_score = jnp.sum(lax.top_k(grp, TOP_K)[0], axis=-1)
    top_group = jnp.argmax(grp_score, axis=-1)
    in_group = (jnp.arange(N_EXPERTS) // EXPERTS_PER_GROUP)[None, :] == top_group[:, None]
    _, idx = lax.top_k(jnp.where(in_group, sel, -jnp.inf), TOP_K)
    w = jnp.take_along_axis(scores, idx, axis=-1)
    w = w / jnp.sum(w, axis=-1, keepdims=True)
    gates = jnp.sum(jax.nn.one_hot(idx, N_EXPERTS, dtype=F32) * w[..., None], axis=1)
    h = jax.nn.silu(jnp.einsum('td,edf->etf', xt, w1)) * jnp.einsum('td,edf->etf', xt, w3)
    h = h * gates.T[:, :, None].astype(h.dtype)
    y = jnp.einsum('etf,efd->td', h, w2)
    return y.reshape(bsz, s, d).astype(x.dtype)


def setup_inputs(seed: int = 0) -> dict:
    key = jax.random.key(seed)
    ks = jax.random.split(key, 32)
    nrm = lambda k, shape, sc: jax.random.normal(k, shape, F32) * sc
    n_idx = jnp.arange(S5_P, dtype=F32)
    return {
        'x': nrm(ks[0], (BATCH, SEQ, D_MODEL), 1.0),
        'p': nrm(ks[1], (DEPTH, BATCH, SEQ, PLE_DIM), 1.0),
        'ab_w_in': nrm(ks[2], (N_EVEN, D_MODEL, D_A + D_B), D_MODEL ** -0.5),
        's5_lambda_re': -0.5 + nrm(ks[3], (N_EVEN, S5_G, S5_P), 0.01),
        's5_lambda_im': math.pi * n_idx + nrm(ks[4], (N_EVEN, S5_G, S5_P), 0.01),
        's5_log_dt': jax.random.uniform(ks[5], (N_EVEN, S5_G), F32, math.log(DT_MIN), math.log(DT_MAX)),
        's5_b_re': nrm(ks[6], (N_EVEN, S5_G, S5_P, S5_H), (2 * S5_H) ** -0.5),
        's5_b_im': nrm(ks[7], (N_EVEN, S5_G, S5_P, S5_H), (2 * S5_H) ** -0.5),
        's5_c_re': nrm(ks[8], (N_EVEN, S5_G, S5_H, S5_P), S5_P ** -0.5),
        's5_c_im': nrm(ks[9], (N_EVEN, S5_G, S5_H, S5_P), S5_P ** -0.5),
        's5_d': nrm(ks[10], (N_EVEN, S5_G, S5_H), 1.0),
        's5_w_glu': nrm(ks[11], (N_EVEN, D_A, D_A), D_A ** -0.5),
        'pool_w': nrm(ks[12], (N_EVEN, POOL_GROUPS, POOL_C, POOL_C), POOL_C ** -0.5),
        'pool_scale': 1.0 + nrm(ks[13], (N_EVEN, D_B), 0.1),
        'ab_w_out': nrm(ks[14], (N_EVEN, D_A + D_B, D_MODEL), (D_A + D_B) ** -0.5 * DEEPNORM_BETA),
        'sb_w_qkv': nrm(ks[15], (N_ODD, D_MODEL, 3 * D_MODEL), D_MODEL ** -0.5),
        'sb_w_out': nrm(ks[16], (N_ODD, D_MODEL, D_MODEL), D_MODEL ** -0.5 * DEEPNORM_BETA),
        'ln_mix_g': 1.0 + nrm(ks[17], (DEPTH, D_MODEL), 0.01),
        'ln_mix_b': nrm(ks[18], (DEPTH, D_MODEL), 0.01),
        'ln_ffn_g': 1.0 + nrm(ks[19], (DEPTH, D_MODEL), 0.01),
        'ln_ffn_b': nrm(ks[20], (DEPTH, D_MODEL), 0.01),
        'router_w': nrm(ks[21], (D_MODEL, N_EXPERTS), D_MODEL ** -0.5),
        'router_bias': nrm(ks[22], (N_EXPERTS,), 0.01),
        'moe_w1': nrm(ks[23], (DEPTH, N_EXPERTS, D_MODEL, D_EXPERT), D_MODEL ** -0.5),
        'moe_w3': nrm(ks[24], (DEPTH, N_EXPERTS, D_MODEL, D_EXPERT), D_MODEL ** -0.5),
        'moe_w2': nrm(ks[25], (DEPTH, N_EXPERTS, D_EXPERT, D_MODEL), D_EXPERT ** -0.5 * DEEPNORM_BETA),
        'ple_w_proj': nrm(ks[26], (DEPTH, PLE_DIM, D_MODEL), PLE_DIM ** -0.5 * DEEPNORM_BETA),
        'ple_w_gate': nrm(ks[27], (DEPTH, D_MODEL, D_MODEL), D_MODEL ** -0.5),
    }


def reference(x, p, ab_w_in, s5_lambda_re, s5_lambda_im, s5_log_dt, s5_b_re, s5_b_im,
              s5_c_re, s5_c_im, s5_d, s5_w_glu, pool_w, pool_scale, ab_w_out,
              sb_w_qkv, sb_w_out, ln_mix_g, ln_mix_b, ln_ffn_g, ln_ffn_b,
              router_w, router_bias, moe_w1, moe_w3, moe_w2, ple_w_proj, ple_w_gate):
    for i in range(DEPTH):
        j = i // 2
        if i % 2 == 0:
            h = x @ ab_w_in[j]
            out_a = s5_mixer(h[..., :D_A], s5_lambda_re[j], s5_lambda_im[j], s5_log_dt[j],
                             s5_b_re[j], s5_b_im[j], s5_c_re[j], s5_c_im[j], s5_d[j], s5_w_glu[j])
            out_b = pool_mixer(h[..., D_A:], pool_w[j], pool_scale[j])
            mix = jnp.concatenate([out_a, out_b], axis=-1) @ ab_w_out[j]
        else:
            mix = stick_breaking_attention(x @ sb_w_qkv[j]) @ sb_w_out[j]
        x = layer_norm(DEEPNORM_ALPHA * x + mix, ln_mix_g[i], ln_mix_b[i])
        ple = (p[i] @ ple_w_proj[i]) * jax.nn.sigmoid(x @ ple_w_gate[i])
        ffn = grouped_moe(x, router_w, router_bias, moe_w1[i], moe_w3[i], moe_w2[i])
        x = layer_norm(DEEPNORM_ALPHA * x + ffn + ple, ln_ffn_g[i], ln_ffn_b[i])
    return x
```

```python
import functools
import math

import jax
import jax.numpy as jnp
from jax import lax
from jax.experimental import pallas as pl
from jax.experimental.pallas import tpu as pltpu

F32 = jnp.float32
BF16 = jnp.bfloat16
I32 = jnp.int32

D_MODEL = 1024
BATCH = 8
SEQ = 2048
DEPTH = 2
TOKENS = BATCH * SEQ

D_A = 512
S5_H = 16
S5_G = 32
S5_P = 64
S5_STATE = S5_G * S5_P
S5_CHUNKS = 4
POOL_WINDOWS = (2, 4, 8, 16)
POOL_C = 128
POOL_HALO = 16 * BATCH
SB_BLOCK = 256
N_EXPERTS = 16
EXPERTS_PER_GROUP = 4
D_EXPERT = 512
PLE_DIM = 256
ALPHA = (2 * DEPTH) ** 0.25
LN_EPS = 1e-5

TS0 = 64
R0 = TS0 * BATCH
RP = 512
RR = 512
RD = 256
TM = 256
N_SLOTS = 2 * TOKENS + N_EXPERTS * TM
N_MTILES = N_SLOTS // TM
TE_LANES = 256

VMEM_LIMIT = 52 * 1024 * 1024
NT_DIMS = (((1,), (1,)), ((), ()))


def _sigmoid(x):
    return 1.0 / (1.0 + jnp.exp(-x))


def _layer_norm(x, g, b):
    mu = jnp.mean(x, axis=-1, keepdims=True)
    xc = x - mu
    var = jnp.mean(xc * xc, axis=-1, keepdims=True)
    return xc * lax.rsqrt(var + LN_EPS) * g + b


def _s5_prep_kernel(lre_ref, lim_ref, ldt_ref, bre_ref, bim_ref,
                    are_ref, aim_ref, bcat_ref):
    lam_re = lre_ref[...]
    lam_im = lim_ref[...]
    dt = jnp.exp(ldt_ref[...])
    mag = jnp.exp(lam_re * dt)
    ang = lam_im * dt
    lb_re = mag * jnp.cos(ang)
    lb_im = mag * jnp.sin(ang)
    are_ref[...] = lb_re
    aim_ref[...] = lb_im
    den = lam_re * lam_re + lam_im * lam_im
    num_re = lb_re - 1.0
    f_re = (num_re * lam_re + lb_im * lam_im) / den
    f_im = (lb_im * lam_re - num_re * lam_im) / den
    for c in range(S5_CHUNKS):
        fr = f_re[:, c * 512:(c + 1) * 512]
        fi = f_im[:, c * 512:(c + 1) * 512]
        br = bre_ref[c]
        bi = bim_ref[c]
        bcat_ref[c, :, 0:512] = (fr * br - fi * bi).astype(BF16)
        bcat_ref[c, :, 512:1024] = (fr * bi + fi * br).astype(BF16)


def _s5_prep(lam_re, lam_im, log_dt, b_re, b_im):
    def blockdiag(b):
        b4 = b.reshape(S5_CHUNKS, 8, S5_P, S5_H).transpose(0, 1, 3, 2)
        eye = jnp.eye(8, dtype=F32)
        full = b4[:, :, :, None, :] * eye[None, :, None, :, None]
        return full.reshape(S5_CHUNKS, 8 * S5_H, 8 * S5_P)

    return pl.pallas_call(
        _s5_prep_kernel,
        out_shape=(jax.ShapeDtypeStruct((1, S5_STATE), F32),
                   jax.ShapeDtypeStruct((1, S5_STATE), F32),
                   jax.ShapeDtypeStruct((S5_CHUNKS, 128, 1024), BF16)),
        name="s5_prep",
    )(lam_re.reshape(1, S5_STATE), lam_im.reshape(1, S5_STATE),
      jnp.repeat(log_dt, S5_P).reshape(1, S5_STATE), blockdiag(b_re), blockdiag(b_im))


def _mixer0_kernel(x_ref, win_ref, bcat_ref, are_ref, aim_ref, ccat_ref, d_ref,
                   wglu_ref, poolw_ref, pscale_ref, o_ref, bus, hst, pe):
    i = pl.program_id(0)

    @pl.when(i == 0)
    def _():
        hst[...] = jnp.zeros_like(hst)
        pe[0:POOL_HALO, :] = jnp.zeros((POOL_HALO, D_A), F32)

    h = jnp.dot(x_ref[...].astype(BF16), win_ref[...], preferred_element_type=F32)
    u = h[:, :D_A]
    v = h[:, D_A:]
    ub = u.astype(BF16)

    for c in range(S5_CHUNKS):
        bus[:, c * 1024:(c + 1) * 1024] = jnp.dot(
            ub[:, c * 128:(c + 1) * 128], bcat_ref[c], preferred_element_type=F32)

    for c in range(S5_CHUNKS):
        re_cols = slice(c * 1024, c * 1024 + 512)
        im_cols = slice(c * 1024 + 512, (c + 1) * 1024)
        ar = jnp.broadcast_to(are_ref[:, c * 512:(c + 1) * 512], (BATCH, 512))
        ai = jnp.broadcast_to(aim_ref[:, c * 512:(c + 1) * 512], (BATCH, 512))

        def step(t, carry, re_cols=re_cols, im_cols=im_cols, ar=ar, ai=ai):
            hr, hi = carry
            rows = pl.ds(pl.multiple_of(t * BATCH, BATCH), BATCH)
            nr = ar * hr - ai * hi + bus[rows, re_cols]
            ni = ar * hi + ai * hr + bus[rows, im_cols]
            bus[rows, re_cols] = nr
            bus[rows, im_cols] = ni
            return nr, ni

        hr, hi = lax.fori_loop(0, TS0, step, (hst[:, re_cols], hst[:, im_cols]), unroll=8)
        hst[:, re_cols] = hr
        hst[:, im_cols] = hi

    ys = [jnp.dot(bus[:, c * 1024:(c + 1) * 1024].astype(BF16), ccat_ref[c],
                  preferred_element_type=F32) for c in range(S5_CHUNKS)]
    y = jnp.concatenate(ys, axis=1) + d_ref[...] * u
    y = 0.5 * y * (1.0 + jnp.tanh(math.sqrt(2.0 / math.pi) * (y + 0.044715 * (y * y * y))))
    ga = y * _sigmoid(jnp.dot(y.astype(BF16), wglu_ref[...], preferred_element_type=F32))
    o_ref[:, 0:D_A] = ga.astype(BF16)

    pe[POOL_HALO:, :] = v
    t_glob = lax.shift_right_logical(
        lax.broadcasted_iota(I32, (R0, 1), 0), int(math.log2(BATCH))) + i * TS0
    for gi, w in enumerate(POOL_WINDOWS):
        cols = slice(gi * POOL_C, (gi + 1) * POOL_C)
        s = pe[:, cols]
        off = BATCH
        while off < BATCH * w:
            s = s[off:] + s[:-off]
            off *= 2
        s = s[POOL_HALO - BATCH * (w - 1):]
        cnt = jnp.minimum(t_glob + 1, w).astype(F32)
        pooled = s / cnt - v[:, cols]
        mixed = jnp.dot(pooled.astype(BF16), poolw_ref[gi], preferred_element_type=F32)
        o_ref[:, D_A + gi * POOL_C:D_A + (gi + 1) * POOL_C] = (
            mixed * pscale_ref[:, cols]).astype(BF16)
    pe[0:POOL_HALO, :] = pe[R0:R0 + POOL_HALO, :]


def _mixer0(x_tm, win, bcat, a_re, a_im, ccat, dskip, wglu, poolw, pscale):
    full = lambda shape: pl.BlockSpec(shape, lambda i: (0,) * len(shape))
    return pl.pallas_call(
        _mixer0_kernel,
        out_shape=jax.ShapeDtypeStruct((TOKENS, D_MODEL), BF16),
        grid=(SEQ // TS0,),
        in_specs=[
            pl.BlockSpec((R0, D_MODEL), lambda i: (i, 0)),
            full((D_MODEL, D_MODEL)),
            full((S5_CHUNKS, 128, 1024)),
            full((1, S5_STATE)),
            full((1, S5_STATE)),
            full((S5_CHUNKS, 1024, 128)),
            full((1, D_A)),
            full((D_A, D_A)),
            full((4, POOL_C, POOL_C)),
            full((1, D_A)),
        ],
        out_specs=pl.BlockSpec((R0, D_MODEL), lambda i: (i, 0)),
        scratch_shapes=[
            pltpu.VMEM((R0, 2 * S5_STATE), F32),
            pltpu.VMEM((BATCH, 2 * S5_STATE), F32),
            pltpu.VMEM((POOL_HALO + R0, D_A), F32),
        ],
        compiler_params=pltpu.CompilerParams(
            dimension_semantics=("arbitrary",), vmem_limit_bytes=VMEM_LIMIT),
        name="mixer0",
    )(x_tm, win, bcat, a_re, a_im, ccat, dskip, wglu, poolw, pscale)


def _proj_ln_route_kernel(a_ref, w_ref, x_ref, g_ref, b_ref, rwh_ref, rwl_ref, rb_ref,
                          x1_ref, ri_ref, wcol_ref):
    mix = jnp.dot(a_ref[...], w_ref[...], preferred_element_type=F32)
    x1 = _layer_norm(ALPHA * x_ref[...] + mix, g_ref[...], b_ref[...])
    x1_ref[...] = x1

    xh = x1.astype(BF16)
    xl = (x1 - xh.astype(F32)).astype(BF16)
    rwh = rwh_ref[...]
    logits = (lax.dot_general(rwh, xh, NT_DIMS, preferred_element_type=F32)
              + lax.dot_general(rwh, xl, NT_DIMS, preferred_element_type=F32)
              + lax.dot_general(rwl_ref[...], xh, NT_DIMS, preferred_element_type=F32))
    scores = _sigmoid(logits)
    sel = scores + rb_ref[...]
    row = lambda a, e: a[e:e + 1, :]

    best = None
    grp = None
    for g in range(N_EXPERTS // EXPERTS_PER_GROUP):
        m = [row(sel, EXPERTS_PER_GROUP * g + k) for k in range(EXPERTS_PER_GROUP)]
        gs = None
        for a in range(EXPERTS_PER_GROUP):
            for b in range(a + 1, EXPERTS_PER_GROUP):
                pair = m[a] + m[b]
                gs = pair if gs is None else jnp.maximum(gs, pair)
        if best is None:
            best, grp = gs, jnp.zeros(gs.shape, I32)
        else:
            better = gs > best
            grp = jnp.where(better, g, grp)
            best = jnp.where(better, gs, best)

    def pick(a, k):
        out = row(a, k)
        for g in range(1, N_EXPERTS // EXPERTS_PER_GROUP):
            out = jnp.where(grp == g, row(a, EXPERTS_PER_GROUP * g + k), out)
        return out

    cs = [pick(sel, k) for k in range(EXPERTS_PER_GROUP)]
    ss = [pick(scores, k) for k in range(EXPERTS_PER_GROUP)]
    m1, i1, s1 = cs[0], jnp.zeros(cs[0].shape, I32), ss[0]
    for k in range(1, EXPERTS_PER_GROUP):
        better = cs[k] > m1
        i1 = jnp.where(better, k, i1)
        s1 = jnp.where(better, ss[k], s1)
        m1 = jnp.where(better, cs[k], m1)
    m2 = jnp.full(m1.shape, -jnp.inf, F32)
    i2 = jnp.zeros(m1.shape, I32)
    s2 = jnp.zeros(m1.shape, F32)
    for k in range(EXPERTS_PER_GROUP):
        better = jnp.logical_and(i1 != k, cs[k] > m2)
        i2 = jnp.where(better, k, i2)
        s2 = jnp.where(better, ss[k], s2)
        m2 = jnp.where(better, cs[k], m2)
    e0 = grp * EXPERTS_PER_GROUP + i1
    e1 = grp * EXPERTS_PER_GROUP + i2
    tot = s1 + s2
    w0 = s1 / tot
    w1 = s2 / tot
    ri_ref[...] = jnp.concatenate([e0, e1, jnp.zeros((6, RP), I32)], axis=0)
    wslab = jnp.concatenate([w0, w1, jnp.zeros((126, RP), F32)], axis=0)
    wcol_ref[...] = wslab.T


def _proj_ln_route(a, w, xres, g, b, rwh, rwl, rb):
    full = lambda shape: pl.BlockSpec(shape, lambda i: (0,) * len(shape))
    return pl.pallas_call(
        _proj_ln_route_kernel,
        out_shape=(jax.ShapeDtypeStruct((TOKENS, D_MODEL), F32),
                   jax.ShapeDtypeStruct((8, TOKENS), I32),
                   jax.ShapeDtypeStruct((TOKENS, 128), F32)),
        grid=(TOKENS // RP,),
        in_specs=[
            pl.BlockSpec((RP, D_MODEL), lambda i: (i, 0)),
            full((D_MODEL, D_MODEL)),
            pl.BlockSpec((RP, D_MODEL), lambda i: (i, 0)),
            full((1, D_MODEL)),
            full((1, D_MODEL)),
            full((N_EXPERTS, D_MODEL)),
            full((N_EXPERTS, D_MODEL)),
            full((N_EXPERTS, 1)),
        ],
        out_specs=(pl.BlockSpec((RP, D_MODEL), lambda i: (i, 0)),
                   pl.BlockSpec((8, RP), lambda i: (0, i)),
                   pl.BlockSpec((RP, 128), lambda i: (i, 0))),
        compiler_params=pltpu.CompilerParams(
            dimension_semantics=("arbitrary",), vmem_limit_bytes=VMEM_LIMIT),
        name="proj_ln_route",
    )(a, w, xres, g, b, rwh, rwl, rb)


def _rank_kernel(ri_ref, pos_ref, te_ref, cnt, offs, tot):
    ph = pl.program_id(0)
    i = pl.program_id(1)
    n_tiles = pl.num_programs(1)
    e_iota = lax.broadcasted_iota(I32, (N_EXPERTS, RR), 0)
    m0 = e_iota == ri_ref[0:1, :]
    m1 = e_iota == ri_ref[1:2, :]
    mask = jnp.where(m0, 1.0, 0.0) + jnp.where(m1, 1.0, 0.0)
    tile_cnt = jnp.sum(mask, axis=1, keepdims=True)

    @pl.when(jnp.logical_and(ph == 0, i == 0))
    def _():
        cnt[...] = jnp.zeros_like(cnt)

    @pl.when(ph == 0)
    def _():
        cnt[...] = cnt[...] + tile_cnt
        pos_ref[...] = jnp.zeros_like(pos_ref)

    @pl.when(jnp.logical_and(ph == 0, i == n_tiles - 1))
    def _():
        c = cnt[...]
        padded = jnp.ceil(c * (1.0 / TM)) * TM
        sub = lax.broadcasted_iota(I32, (N_EXPERTS, 128), 0)
        acc = jnp.zeros((N_EXPERTS, 128), F32)
        for e in range(N_EXPERTS):
            acc = acc + jnp.where(sub > e, padded[e:e + 1, :], 0.0)
        offs[...] = acc
        tot[...] = acc + padded
        cnt[...] = jnp.zeros_like(cnt)

    @pl.when(ph == 1)
    def _():
        r = lax.broadcasted_iota(I32, (RR, RR), 0)
        cidx = lax.broadcasted_iota(I32, (RR, RR), 1)
        tri = jnp.where(r < cidx, 1.0, 0.0).astype(BF16)
        prefix = jnp.dot(mask.astype(BF16), tri, preferred_element_type=F32)
        slot = prefix + cnt[:, 0:1] + offs[:, 0:1]
        p0 = jnp.sum(jnp.where(m0, slot, 0.0), axis=0, keepdims=True)
        p1 = jnp.sum(jnp.where(m1, slot, 0.0), axis=0, keepdims=True)
        pos_ref[...] = jnp.concatenate(
            [p0.astype(I32), p1.astype(I32), jnp.zeros((6, RR), I32)], axis=0)
        cnt[...] = cnt[...] + tile_cnt

    @pl.when(jnp.logical_and(ph == 1, i == n_tiles - 1))
    def _():
        ends = tot[:, 0:1]
        start = (lax.broadcasted_iota(I32, (N_EXPERTS, TE_LANES), 1) * TM).astype(F32)
        te = jnp.sum(jnp.where(start >= ends, 1.0, 0.0), axis=0, keepdims=True)
        te = jnp.minimum(te, N_EXPERTS - 1.0).astype(I32)
        n_used = (tot[N_EXPERTS - 1:N_EXPERTS, :] * (1.0 / TM)).astype(I32)
        n_used = jnp.concatenate([n_used, n_used], axis=1)
        te_ref[...] = jnp.concatenate(
            [te, n_used, jnp.zeros((6, TE_LANES), I32)], axis=0)


def _rank(ri):
    return pl.pallas_call(
        _rank_kernel,
        out_shape=(jax.ShapeDtypeStruct((8, TOKENS), I32),
                   jax.ShapeDtypeStruct((8, TE_LANES), I32)),
        grid=(2, TOKENS // RR),
        in_specs=[pl.BlockSpec((8, RR), lambda p, i: (0, i))],
        out_specs=(pl.BlockSpec((8, RR), lambda p, i: (0, i * p)),
                   pl.BlockSpec((8, TE_LANES), lambda p, i: (0, 0))),
        scratch_shapes=[pltpu.VMEM((N_EXPERTS, 128), F32),
                        pltpu.VMEM((N_EXPERTS, 128), F32),
                        pltpu.VMEM((N_EXPERTS, 128), F32)],
        compiler_params=pltpu.CompilerParams(
            dimension_semantics=("arbitrary", "arbitrary")),
        name="moe_rank",
    )(ri)


def _dispatch_kernel(pos_ref, x_ref, xs_in_ref, xs_ref, sem):
    del xs_in_ref
    base = pl.program_id(0) * RD

    def row_copy(r, slot):
        return pltpu.make_async_copy(x_ref.at[pl.ds(r, 1)], xs_ref.at[pl.ds(slot, 1)], sem)

    def issue(r, carry):
        row_copy(r, pos_ref[base + r]).start()
        row_copy(r, pos_ref[TOKENS + base + r]).start()
        return carry

    lax.fori_loop(0, RD, issue, 0, unroll=8)

    def drain(r, carry):
        row_copy(0, 0).wait()
        row_copy(0, 0).wait()
        return carry

    lax.fori_loop(0, RD, drain, 0, unroll=8)


def _dispatch(pos_flat, x1, xs_init):
    return pl.pallas_call(
        _dispatch_kernel,
        out_shape=jax.ShapeDtypeStruct((N_SLOTS, D_MODEL), F32),
        grid_spec=pltpu.PrefetchScalarGridSpec(
            num_scalar_prefetch=1,
            grid=(TOKENS // RD,),
            in_specs=[pl.BlockSpec((RD, D_MODEL), lambda i, pos: (i, 0)),
                      pl.BlockSpec(memory_space=pl.ANY)],
            out_specs=pl.BlockSpec(memory_space=pl.ANY),
            scratch_shapes=[pltpu.SemaphoreType.DMA],
        ),
        input_output_aliases={2: 0},
        compiler_params=pltpu.CompilerParams(
            dimension_semantics=("arbitrary",), has_side_effects=True),
        name="moe_dispatch",
    )(pos_flat, x1, xs_init)


def _expert_kernel(te_ref, nu_ref, xs_ref, w1_ref, w3_ref, w2_ref, o_ref, w1b, w3b, w2b):
    i = pl.program_id(0)
    e = te_ref[i]
    prev = te_ref[jnp.maximum(i - 1, 0)]

    @pl.when(jnp.logical_or(i == 0, e != prev))
    def _():
        w1b[...] = w1_ref[...].astype(BF16)
        w3b[...] = w3_ref[...].astype(BF16)
        w2b[...] = w2_ref[...].astype(BF16)

    @pl.when(i < nu_ref[0])
    def _():
        xb = xs_ref[...].astype(BF16)
        h1 = jnp.dot(xb, w1b[...], preferred_element_type=F32)
        h3 = jnp.dot(xb, w3b[...], preferred_element_type=F32)
        h = (h1 * _sigmoid(h1)) * h3
        o_ref[...] = jnp.dot(h.astype(BF16), w2b[...], preferred_element_type=F32)

    @pl.when(i >= nu_ref[0])
    def _():
        o_ref[...] = jnp.zeros_like(o_ref)


def _experts(te, nu, xs, w1, w3, w2, layer):
    wspec = lambda r, c: pl.BlockSpec((None, None, r, c),
                                      lambda i, te, nu: (layer, te[i], 0, 0))
    return pl.pallas_call(
        _expert_kernel,
        out_shape=jax.ShapeDtypeStruct((N_SLOTS, D_MODEL), F32),
        grid_spec=pltpu.PrefetchScalarGridSpec(
            num_scalar_prefetch=2,
            grid=(N_MTILES,),
            in_specs=[pl.BlockSpec((TM, D_MODEL), lambda i, te, nu: (i, 0)),
                      wspec(D_MODEL, D_EXPERT), wspec(D_MODEL, D_EXPERT),
                      wspec(D_EXPERT, D_MODEL)],
            out_specs=pl.BlockSpec((TM, D_MODEL), lambda i, te, nu: (i, 0)),
            scratch_shapes=[pltpu.VMEM((D_MODEL, D_EXPERT), BF16),
                            pltpu.VMEM((D_MODEL, D_EXPERT), BF16),
                            pltpu.VMEM((D_EXPERT, D_MODEL), BF16)],
        ),
        compiler_params=pltpu.CompilerParams(
            dimension_semantics=("arbitrary",), vmem_limit_bytes=VMEM_LIMIT),
        name="moe_experts",
    )(te, nu, xs, w1, w3, w2)


def _combine_kernel(pos_ref, x1_ref, p_ref, wcol_ref, ys_ref, wp_ref, wg_ref, g_ref, b_ref,
                    o_ref, ybuf, sem):
    base = pl.program_id(0) * RD

    def row_copy(k, r, slot):
        return pltpu.make_async_copy(ys_ref.at[pl.ds(slot, 1)], ybuf.at[k, pl.ds(r, 1)], sem)

    def issue(r, carry):
        row_copy(0, r, pos_ref[base + r]).start()
        row_copy(1, r, pos_ref[TOKENS + base + r]).start()
        return carry

    lax.fori_loop(0, RD, issue, 0, unroll=8)

    x1 = x1_ref[...]
    gate = _sigmoid(jnp.dot(x1.astype(BF16), wg_ref[...], preferred_element_type=F32))
    ple = jnp.dot(p_ref[...].astype(BF16), wp_ref[...], preferred_element_type=F32) * gate

    def drain(r, carry):
        row_copy(0, 0, 0).wait()
        row_copy(1, 0, 0).wait()
        return carry

    lax.fori_loop(0, RD, drain, 0, unroll=8)

    wc = wcol_ref[...]
    ffn = wc[:, 0:1] * ybuf[0] + wc[:, 1:2] * ybuf[1]
    o_ref[...] = _layer_norm(ALPHA * x1 + ffn + ple, g_ref[...], b_ref[...])


def _combine(pos_flat, x1, p, wcol, ys, wp, wg, g, b):
    full = lambda shape: pl.BlockSpec(shape, lambda i, pos: (0,) * len(shape))
    return pl.pallas_call(
        _combine_kernel,
        out_shape=jax.ShapeDtypeStruct((TOKENS, D_MODEL), F32),
        grid_spec=pltpu.PrefetchScalarGridSpec(
            num_scalar_prefetch=1,
            grid=(TOKENS // RD,),
            in_specs=[pl.BlockSpec((RD, D_MODEL), lambda i, pos: (i, 0)),
                      pl.BlockSpec((RD, PLE_DIM), lambda i, pos: (i, 0)),
                      pl.BlockSpec((RD, 128), lambda i, pos: (i, 0)),
                      pl.BlockSpec(memory_space=pl.ANY),
                      full((PLE_DIM, D_MODEL)),
                      full((D_MODEL, D_MODEL)),
                      full((1, D_MODEL)),
                      full((1, D_MODEL))],
            out_specs=pl.BlockSpec((RD, D_MODEL), lambda i, pos: (i, 0)),
            scratch_shapes=[pltpu.VMEM((2, RD, D_MODEL), F32),
                            pltpu.SemaphoreType.DMA],
        ),
        compiler_params=pltpu.CompilerParams(
            dimension_semantics=("arbitrary",), vmem_limit_bytes=VMEM_LIMIT),
        name="moe_combine",
    )(pos_flat, x1, p, wcol, ys, wp, wg, g, b)


def _qkv_kernel(x_ref, w_ref, o_ref):
    o_ref[...] = jnp.dot(x_ref[...].astype(BF16), w_ref[...],
                         preferred_element_type=F32).astype(BF16)


def _qkv_proj(x, w):
    return pl.pallas_call(
        _qkv_kernel,
        out_shape=jax.ShapeDtypeStruct((TOKENS, 3 * D_MODEL), BF16),
        grid=(TOKENS // RP, 3),
        in_specs=[pl.BlockSpec((RP, D_MODEL), lambda i, j: (i, 0)),
                  pl.BlockSpec((D_MODEL, D_MODEL), lambda i, j: (0, j))],
        out_specs=pl.BlockSpec((RP, D_MODEL), lambda i, j: (i, j)),
        compiler_params=pltpu.CompilerParams(
            dimension_semantics=("arbitrary", "arbitrary"), vmem_limit_bytes=VMEM_LIMIT),
        name="qkv_proj",
    )(x, w)


def _attn_kernel(q_ref, k_ref, v_ref, o_ref, acc, cbuf):
    qb = pl.program_id(2)
    lane = lax.broadcasted_iota(I32, (1, 128), 1)
    q = q_ref[...] * jnp.asarray(0.125, BF16)
    zero = jnp.zeros_like(q)
    qh = (jnp.where(lane < 64, q, zero), jnp.where(lane >= 64, q, zero))
    r = lax.broadcasted_iota(I32, (SB_BLOCK, SB_BLOCK), 0)
    cidx = lax.broadcasted_iota(I32, (SB_BLOCK, SB_BLOCK), 1)
    suffix = jnp.where(r >= cidx, 1.0, 0.0).astype(BF16)
    causal = cidx < r

    acc[...] = jnp.zeros_like(acc)
    cbuf[...] = jnp.zeros_like(cbuf)

    def block(kb, masked):
        rows = pl.ds(pl.multiple_of(kb * SB_BLOCK, SB_BLOCK), SB_BLOCK)
        k = k_ref[rows, :]
        v = v_ref[rows, :]
        for hh in range(2):
            z = lax.dot_general(qh[hh], k, NT_DIMS, preferred_element_type=F32)
            lk = -(jnp.maximum(z, 0.0) + jnp.log(1.0 + jnp.exp(-jnp.abs(z))))
            if masked:
                lk = jnp.where(causal, lk, 0.0)
            c = cbuf[hh]
            rs = (jnp.dot(lk.astype(BF16), suffix, preferred_element_type=F32)
                  + jnp.concatenate([c, c], axis=1))
            w = jnp.exp(z + rs)
            if masked:
                w = jnp.where(causal, w, 0.0)
            acc[hh] = acc[hh] + jnp.dot(w.astype(BF16), v, preferred_element_type=F32)
            cbuf[hh] = jnp.broadcast_to(rs[:, 0:1], (SB_BLOCK, 128))

    block(qb, True)

    def body(j, carry):
        block(qb - 1 - j, False)
        return carry

    lax.fori_loop(0, qb, body, 0)
    o_ref[...] = jnp.where(lane < 64, acc[0], acc[1]).astype(BF16)


def _attention(qkv):
    n_hp = D_MODEL // 128
    return pl.pallas_call(
        _attn_kernel,
        out_shape=jax.ShapeDtypeStruct((BATCH, SEQ, D_MODEL), BF16),
        grid=(BATCH, n_hp, SEQ // SB_BLOCK),
        in_specs=[pl.BlockSpec((None, SB_BLOCK, 128), lambda b, h, i: (b, i, h)),
                  pl.BlockSpec((None, SEQ, 128), lambda b, h, i: (b, 0, n_hp + h)),
                  pl.BlockSpec((None, SEQ, 128), lambda b, h, i: (b, 0, 2 * n_hp + h))],
        out_specs=pl.BlockSpec((None, SB_BLOCK, 128), lambda b, h, i: (b, i, h)),
        scratch_shapes=[pltpu.VMEM((2, SB_BLOCK, 128), F32),
                        pltpu.VMEM((2, SB_BLOCK, 128), F32)],
        compiler_params=pltpu.CompilerParams(
            dimension_semantics=("arbitrary", "arbitrary", "arbitrary"),
            vmem_limit_bytes=VMEM_LIMIT),
        name="sb_attention",
    )(qkv, qkv, qkv)


def _moe_ffn(x1, ri, wcol, p_rows, layer, moe_w1, moe_w3, moe_w2, wp, wg, g, b):
    pos, te = _rank(ri)
    pos_flat = pos[0:2].reshape(2 * TOKENS)
    xs = _dispatch(pos_flat, x1, jnp.zeros((N_SLOTS, D_MODEL), F32))
    ys = _experts(te[0, :N_MTILES], te[1, :1], xs, moe_w1, moe_w3, moe_w2, layer)
    return _combine(pos_flat, x1, p_rows, wcol, ys, wp, wg, g, b)


def kernel(x, p, ab_w_in, s5_lambda_re, s5_lambda_im, s5_log_dt, s5_b_re, s5_b_im, s5_c_re, s5_c_im, s5_d, s5_w_glu, pool_w, pool_scale, ab_w_out, sb_w_qkv, sb_w_out, ln_mix_g, ln_mix_b, ln_ffn_g, ln_ffn_b, router_w, router_bias, moe_w1, moe_w3, moe_w2, ple_w_proj, ple_w_gate):
    row = lambda a: a.reshape(1, -1)
    rwt = router_w.T
    rwh = rwt.astype(BF16)
    rwl = (rwt - rwh.astype(F32)).astype(BF16)
    rb = router_bias.reshape(N_EXPERTS, 1)

    x_tm = x.transpose(1, 0, 2).reshape(TOKENS, D_MODEL)
    p_tm = p[0].transpose(1, 0, 2).reshape(TOKENS, PLE_DIM)
    a_re, a_im, bcat = _s5_prep(s5_lambda_re[0], s5_lambda_im[0], s5_log_dt[0],
                                s5_b_re[0], s5_b_im[0])

    def c_blockdiag(c):
        c4 = c.reshape(S5_CHUNKS, 8, S5_H, S5_P).transpose(0, 1, 3, 2)
        eye = jnp.eye(8, dtype=F32)
        full = c4[:, :, :, None, :] * eye[None, :, None, :, None]
        return full.reshape(S5_CHUNKS, 8 * S5_P, 8 * S5_H)

    ccat = jnp.concatenate([c_blockdiag(s5_c_re[0]), -c_blockdiag(s5_c_im[0])],
                           axis=1).astype(BF16)
    mix_in = _mixer0(x_tm, ab_w_in[0].astype(BF16), bcat, a_re, a_im, ccat,
                     row(s5_d[0]), s5_w_glu[0].astype(BF16), pool_w[0].astype(BF16),
                     row(pool_scale[0]))
    x1, ri, wcol = _proj_ln_route(mix_in, ab_w_out[0].astype(BF16), x_tm,
                                  row(ln_mix_g[0]), row(ln_mix_b[0]), rwh, rwl, rb)
    x2 = _moe_ffn(x1, ri, wcol, p_tm, 0, moe_w1, moe_w3, moe_w2,
                  ple_w_proj[0].astype(BF16), ple_w_gate[0].astype(BF16),
                  row(ln_ffn_g[0]), row(ln_ffn_b[0]))

    xb = x2.reshape(SEQ, BATCH, D_MODEL).transpose(1, 0, 2).reshape(TOKENS, D_MODEL)
    qkv = _qkv_proj(xb, sb_w_qkv[0].astype(BF16))
    att = _attention(qkv.reshape(BATCH, SEQ, 3 * D_MODEL)).reshape(TOKENS, D_MODEL)
    x3, ri, wcol = _proj_ln_route(att, sb_w_out[0].astype(BF16), xb,
                                  row(ln_mix_g[1]), row(ln_mix_b[1]), rwh, rwl, rb)
    x4 = _moe_ffn(x3, ri, wcol, p[1].reshape(TOKENS, PLE_DIM), 1, moe_w1, moe_w3, moe_w2,
                  ple_w_proj[1].astype(BF16), ple_w_gate[1].astype(BF16),
                  row(ln_ffn_g[1]), row(ln_ffn_b[1]))
    return x4.reshape(BATCH, SEQ, D_MODEL)
```

```python
import functools
import math

import jax
import jax.numpy as jnp
from jax import lax
from jax.experimental import pallas as pl
from jax.experimental.pallas import tpu as pltpu

F32 = jnp.float32
BF16 = jnp.bfloat16
I32 = jnp.int32

D_MODEL = 1024
BATCH = 8
SEQ = 2048
DEPTH = 2
TOKENS = BATCH * SEQ

D_A = 512
S5_H = 16
S5_G = 32
S5_P = 64
S5_STATE = S5_G * S5_P
S5_CHUNKS = 4
POOL_WINDOWS = (2, 4, 8, 16)
POOL_C = 128
POOL_HALO = 16 * BATCH
SB_BLOCK = 256
ATT_PAIRS = 4
ATT_LANES = 128 * ATT_PAIRS
N_EXPERTS = 16
EXPERTS_PER_GROUP = 4
D_EXPERT = 512
PLE_DIM = 256
ALPHA = (2 * DEPTH) ** 0.25
LN_EPS = 1e-5

TS0 = 64
R0 = TS0 * BATCH
RP = 512
RR = 512
RD = 256
TM = 256
N_SLOTS = 2 * TOKENS + N_EXPERTS * TM
N_MTILES = N_SLOTS // TM
TE_LANES = 256

VMEM_LIMIT = 52 * 1024 * 1024
NT_DIMS = (((1,), (1,)), ((), ()))
SIGN_BIT = 0x80000000


def _sigmoid(x):
    return 1.0 / (1.0 + jnp.exp(-x))


def _layer_norm(x, g, b):
    mu = jnp.mean(x, axis=-1, keepdims=True)
    xc = x - mu
    var = jnp.mean(xc * xc, axis=-1, keepdims=True)
    return xc * lax.rsqrt(var + LN_EPS) * g + b


def _s5_prep_kernel(lre_ref, lim_ref, ldt_ref, bre_ref, bim_ref,
                    are_ref, aim_ref, bcat_ref):
    lam_re = lre_ref[...]
    lam_im = lim_ref[...]
    dt = jnp.exp(ldt_ref[...])
    mag = jnp.exp(lam_re * dt)
    ang = lam_im * dt
    lb_re = mag * jnp.cos(ang)
    lb_im = mag * jnp.sin(ang)
    are_ref[...] = lb_re
    aim_ref[...] = lb_im
    den = lam_re * lam_re + lam_im * lam_im
    num_re = lb_re - 1.0
    f_re = (num_re * lam_re + lb_im * lam_im) / den
    f_im = (lb_im * lam_re - num_re * lam_im) / den
    for c in range(S5_CHUNKS):
        fr = f_re[:, c * 512:(c + 1) * 512]
        fi = f_im[:, c * 512:(c + 1) * 512]
        br = bre_ref[c]
        bi = bim_ref[c]
        bcat_ref[c, :, 0:512] = (fr * br - fi * bi).astype(BF16)
        bcat_ref[c, :, 512:1024] = (fr * bi + fi * br).astype(BF16)


def _s5_prep(lam_re, lam_im, log_dt, b_re, b_im):
    def blockdiag(b):
        b4 = b.reshape(S5_CHUNKS, 8, S5_P, S5_H).transpose(0, 1, 3, 2)
        eye = jnp.eye(8, dtype=F32)
        full = b4[:, :, :, None, :] * eye[None, :, None, :, None]
        return full.reshape(S5_CHUNKS, 8 * S5_H, 8 * S5_P)

    return pl.pallas_call(
        _s5_prep_kernel,
        out_shape=(jax.ShapeDtypeStruct((1, S5_STATE), F32),
                   jax.ShapeDtypeStruct((1, S5_STATE), F32),
                   jax.ShapeDtypeStruct((S5_CHUNKS, 128, 1024), BF16)),
        name="s5_prep",
    )(lam_re.reshape(1, S5_STATE), lam_im.reshape(1, S5_STATE),
      jnp.repeat(log_dt, S5_P).reshape(1, S5_STATE), blockdiag(b_re), blockdiag(b_im))


def _mixer0_kernel(x_ref, win_ref, bcat_ref, are_ref, aim_ref, ccat_ref, d_ref,
                   wglu_ref, poolw_ref, pscale_ref, o_ref, bus, hst, pe):
    i = pl.program_id(0)

    @pl.when(i == 0)
    def _():
        hst[...] = jnp.zeros_like(hst)
        pe[0:POOL_HALO, :] = jnp.zeros((POOL_HALO, D_A), F32)

    h = jnp.dot(x_ref[...].astype(BF16), win_ref[...], preferred_element_type=F32)
    u = h[:, :D_A]
    v = h[:, D_A:]
    ub = u.astype(BF16)

    for c in range(S5_CHUNKS):
        bus[:, c * 1024:(c + 1) * 1024] = jnp.dot(
            ub[:, c * 128:(c + 1) * 128], bcat_ref[c], preferred_element_type=F32)

    for c in range(S5_CHUNKS):
        re_cols = slice(c * 1024, c * 1024 + 512)
        im_cols = slice(c * 1024 + 512, (c + 1) * 1024)
        ar = jnp.broadcast_to(are_ref[:, c * 512:(c + 1) * 512], (BATCH, 512))
        ai = jnp.broadcast_to(aim_ref[:, c * 512:(c + 1) * 512], (BATCH, 512))

        def step(t, carry, re_cols=re_cols, im_cols=im_cols, ar=ar, ai=ai):
            hr, hi = carry
            rows = pl.ds(pl.multiple_of(t * BATCH, BATCH), BATCH)
            nr = ar * hr - ai * hi + bus[rows, re_cols]
            ni = ar * hi + ai * hr + bus[rows, im_cols]
            bus[rows, re_cols] = nr
            bus[rows, im_cols] = ni
            return nr, ni

        hr, hi = lax.fori_loop(0, TS0, step, (hst[:, re_cols], hst[:, im_cols]), unroll=8)
        hst[:, re_cols] = hr
        hst[:, im_cols] = hi

    ys = [jnp.dot(bus[:, c * 1024:(c + 1) * 1024].astype(BF16), ccat_ref[c],
                  preferred_element_type=F32) for c in range(S5_CHUNKS)]
    y = jnp.concatenate(ys, axis=1) + d_ref[...] * u
    y = 0.5 * y * (1.0 + jnp.tanh(math.sqrt(2.0 / math.pi) * (y + 0.044715 * (y * y * y))))
    ga = y * _sigmoid(jnp.dot(y.astype(BF16), wglu_ref[...], preferred_element_type=F32))
    o_ref[:, 0:D_A] = ga.astype(BF16)

    pe[POOL_HALO:, :] = v
    t_glob = lax.shift_right_logical(
        lax.broadcasted_iota(I32, (R0, 1), 0), int(math.log2(BATCH))) + i * TS0
    for gi, w in enumerate(POOL_WINDOWS):
        cols = slice(gi * POOL_C, (gi + 1) * POOL_C)
        s = pe[:, cols]
        off = BATCH
        while off < BATCH * w:
            s = s[off:] + s[:-off]
            off *= 2
        s = s[POOL_HALO - BATCH * (w - 1):]
        cnt = jnp.minimum(t_glob + 1, w).astype(F32)
        pooled = s / cnt - v[:, cols]
        mixed = jnp.dot(pooled.astype(BF16), poolw_ref[gi], preferred_element_type=F32)
        o_ref[:, D_A + gi * POOL_C:D_A + (gi + 1) * POOL_C] = (
            mixed * pscale_ref[:, cols]).astype(BF16)
    pe[0:POOL_HALO, :] = pe[R0:R0 + POOL_HALO, :]


def _mixer0(x_tm, win, bcat, a_re, a_im, ccat, dskip, wglu, poolw, pscale):
    full = lambda shape: pl.BlockSpec(shape, lambda i: (0,) * len(shape))
    return pl.pallas_call(
        _mixer0_kernel,
        out_shape=jax.ShapeDtypeStruct((TOKENS, D_MODEL), BF16),
        grid=(SEQ // TS0,),
        in_specs=[
            pl.BlockSpec((R0, D_MODEL), lambda i: (i, 0)),
            full((D_MODEL, D_MODEL)),
            full((S5_CHUNKS, 128, 1024)),
            full((1, S5_STATE)),
            full((1, S5_STATE)),
            full((S5_CHUNKS, 1024, 128)),
            full((1, D_A)),
            full((D_A, D_A)),
            full((4, POOL_C, POOL_C)),
            full((1, D_A)),
        ],
        out_specs=pl.BlockSpec((R0, D_MODEL), lambda i: (i, 0)),
        scratch_shapes=[
            pltpu.VMEM((R0, 2 * S5_STATE), F32),
            pltpu.VMEM((BATCH, 2 * S5_STATE), F32),
            pltpu.VMEM((POOL_HALO + R0, D_A), F32),
        ],
        compiler_params=pltpu.CompilerParams(
            dimension_semantics=("arbitrary",), vmem_limit_bytes=VMEM_LIMIT),
        name="mixer0",
    )(x_tm, win, bcat, a_re, a_im, ccat, dskip, wglu, poolw, pscale)


def _proj_ln_route_kernel(a_ref, w_ref, x_ref, g_ref, b_ref, rwh_ref, rwl_ref, rb_ref,
                          x1_ref, ri_ref, wcol_ref):
    mix = jnp.dot(a_ref[...], w_ref[...], preferred_element_type=F32)
    x1 = _layer_norm(ALPHA * x_ref[...] + mix, g_ref[...], b_ref[...])
    x1_ref[...] = x1

    xh = x1.astype(BF16)
    xl = (x1 - xh.astype(F32)).astype(BF16)
    rwh = rwh_ref[...]
    logits = (lax.dot_general(rwh, xh, NT_DIMS, preferred_element_type=F32)
              + lax.dot_general(rwh, xl, NT_DIMS, preferred_element_type=F32)
              + lax.dot_general(rwl_ref[...], xh, NT_DIMS, preferred_element_type=F32))
    scores = _sigmoid(logits)
    sel = scores + rb_ref[...]
    row = lambda a, e: a[e:e + 1, :]

    best = None
    grp = None
    for g in range(N_EXPERTS // EXPERTS_PER_GROUP):
        m = [row(sel, EXPERTS_PER_GROUP * g + k) for k in range(EXPERTS_PER_GROUP)]
        gs = None
        for a in range(EXPERTS_PER_GROUP):
            for b in range(a + 1, EXPERTS_PER_GROUP):
                pair = m[a] + m[b]
                gs = pair if gs is None else jnp.maximum(gs, pair)
        if best is None:
            best, grp = gs, jnp.zeros(gs.shape, I32)
        else:
            better = gs > best
            grp = jnp.where(better, g, grp)
            best = jnp.where(better, gs, best)

    def pick(a, k):
        out = row(a, k)
        for g in range(1, N_EXPERTS // EXPERTS_PER_GROUP):
            out = jnp.where(grp == g, row(a, EXPERTS_PER_GROUP * g + k), out)
        return out

    cs = [pick(sel, k) for k in range(EXPERTS_PER_GROUP)]
    ss = [pick(scores, k) for k in range(EXPERTS_PER_GROUP)]
    m1, i1, s1 = cs[0], jnp.zeros(cs[0].shape, I32), ss[0]
    for k in range(1, EXPERTS_PER_GROUP):
        better = cs[k] > m1
        i1 = jnp.where(better, k, i1)
        s1 = jnp.where(better, ss[k], s1)
        m1 = jnp.where(better, cs[k], m1)
    m2 = jnp.full(m1.shape, -jnp.inf, F32)
    i2 = jnp.zeros(m1.shape, I32)
    s2 = jnp.zeros(m1.shape, F32)
    for k in range(EXPERTS_PER_GROUP):
        better = jnp.logical_and(i1 != k, cs[k] > m2)
        i2 = jnp.where(better, k, i2)
        s2 = jnp.where(better, ss[k], s2)
        m2 = jnp.where(better, cs[k], m2)
    e0 = grp * EXPERTS_PER_GROUP + i1
    e1 = grp * EXPERTS_PER_GROUP + i2
    tot = s1 + s2
    w0 = s1 / tot
    w1 = s2 / tot
    ri_ref[...] = jnp.concatenate([e0, e1, jnp.zeros((6, RP), I32)], axis=0)
    wslab = jnp.concatenate([w0, w1, jnp.zeros((126, RP), F32)], axis=0)
    wcol_ref[...] = wslab.T


def _proj_ln_route(a, w, xres, g, b, rwh, rwl, rb):
    full = lambda shape: pl.BlockSpec(shape, lambda i: (0,) * len(shape))
    return pl.pallas_call(
        _proj_ln_route_kernel,
        out_shape=(jax.ShapeDtypeStruct((TOKENS, D_MODEL), F32),
                   jax.ShapeDtypeStruct((8, TOKENS), I32),
                   jax.ShapeDtypeStruct((TOKENS, 128), F32)),
        grid=(TOKENS // RP,),
        in_specs=[
            pl.BlockSpec((RP, D_MODEL), lambda i: (i, 0)),
            full((D_MODEL, D_MODEL)),
            pl.BlockSpec((RP, D_MODEL), lambda i: (i, 0)),
            full((1, D_MODEL)),
            full((1, D_MODEL)),
            full((N_EXPERTS, D_MODEL)),
            full((N_EXPERTS, D_MODEL)),
            full((N_EXPERTS, 1)),
        ],
        out_specs=(pl.BlockSpec((RP, D_MODEL), lambda i: (i, 0)),
                   pl.BlockSpec((8, RP), lambda i: (0, i)),
                   pl.BlockSpec((RP, 128), lambda i: (i, 0))),
        compiler_params=pltpu.CompilerParams(
            dimension_semantics=("arbitrary",), vmem_limit_bytes=VMEM_LIMIT),
        name="proj_ln_route",
    )(a, w, xres, g, b, rwh, rwl, rb)


def _rank_kernel(ri_ref, pos_ref, te_ref, cnt, offs, tot):
    ph = pl.program_id(0)
    i = pl.program_id(1)
    n_tiles = pl.num_programs(1)
    e_iota = lax.broadcasted_iota(I32, (N_EXPERTS, RR), 0)
    m0 = e_iota == ri_ref[0:1, :]
    m1 = e_iota == ri_ref[1:2, :]
    mask = jnp.where(m0, 1.0, 0.0) + jnp.where(m1, 1.0, 0.0)
    tile_cnt = jnp.sum(mask, axis=1, keepdims=True)

    @pl.when(jnp.logical_and(ph == 0, i == 0))
    def _():
        cnt[...] = jnp.zeros_like(cnt)

    @pl.when(ph == 0)
    def _():
        cnt[...] = cnt[...] + tile_cnt
        pos_ref[...] = jnp.zeros_like(pos_ref)

    @pl.when(jnp.logical_and(ph == 0, i == n_tiles - 1))
    def _():
        c = cnt[...]
        padded = jnp.ceil(c * (1.0 / TM)) * TM
        sub = lax.broadcasted_iota(I32, (N_EXPERTS, 128), 0)
        acc = jnp.zeros((N_EXPERTS, 128), F32)
        for e in range(N_EXPERTS):
            acc = acc + jnp.where(sub > e, padded[e:e + 1, :], 0.0)
        offs[...] = acc
        tot[...] = acc + padded
        cnt[...] = jnp.zeros_like(cnt)

    @pl.when(ph == 1)
    def _():
        r = lax.broadcasted_iota(I32, (RR, RR), 0)
        cidx = lax.broadcasted_iota(I32, (RR, RR), 1)
        tri = jnp.where(r < cidx, 1.0, 0.0).astype(BF16)
        prefix = jnp.dot(mask.astype(BF16), tri, preferred_element_type=F32)
        slot = prefix + cnt[:, 0:1] + offs[:, 0:1]
        p0 = jnp.sum(jnp.where(m0, slot, 0.0), axis=0, keepdims=True)
        p1 = jnp.sum(jnp.where(m1, slot, 0.0), axis=0, keepdims=True)
        pos_ref[...] = jnp.concatenate(
            [p0.astype(I32), p1.astype(I32), jnp.zeros((6, RR), I32)], axis=0)
        cnt[...] = cnt[...] + tile_cnt

    @pl.when(jnp.logical_and(ph == 1, i == n_tiles - 1))
    def _():
        ends = tot[:, 0:1]
        start = (lax.broadcasted_iota(I32, (N_EXPERTS, TE_LANES), 1) * TM).astype(F32)
        te = jnp.sum(jnp.where(start >= ends, 1.0, 0.0), axis=0, keepdims=True)
        te = jnp.minimum(te, N_EXPERTS - 1.0).astype(I32)
        n_used = (tot[N_EXPERTS - 1:N_EXPERTS, :] * (1.0 / TM)).astype(I32)
        n_used = jnp.concatenate([n_used, n_used], axis=1)
        te_ref[...] = jnp.concatenate(
            [te, n_used, jnp.zeros((6, TE_LANES), I32)], axis=0)


def _rank(ri):
    return pl.pallas_call(
        _rank_kernel,
        out_shape=(jax.ShapeDtypeStruct((8, TOKENS), I32),
                   jax.ShapeDtypeStruct((8, TE_LANES), I32)),
        grid=(2, TOKENS // RR),
        in_specs=[pl.BlockSpec((8, RR), lambda p, i: (0, i))],
        out_specs=(pl.BlockSpec((8, RR), lambda p, i: (0, i * p)),
                   pl.BlockSpec((8, TE_LANES), lambda p, i: (0, 0))),
        scratch_shapes=[pltpu.VMEM((N_EXPERTS, 128), F32),
                        pltpu.VMEM((N_EXPERTS, 128), F32),
                        pltpu.VMEM((N_EXPERTS, 128), F32)],
        compiler_params=pltpu.CompilerParams(
            dimension_semantics=("arbitrary", "arbitrary")),
        name="moe_rank",
    )(ri)


def _dispatch_kernel(pos_ref, x_ref, xs_in_ref, xs_ref, sem):
    del xs_in_ref
    base = pl.program_id(0) * RD

    def row_copy(r, slot):
        return pltpu.make_async_copy(x_ref.at[pl.ds(r, 1)], xs_ref.at[pl.ds(slot, 1)], sem)

    def issue(r, carry):
        row_copy(r, pos_ref[base + r]).start()
        row_copy(r, pos_ref[TOKENS + base + r]).start()
        return carry

    lax.fori_loop(0, RD, issue, 0, unroll=8)

    def drain(r, carry):
        row_copy(0, 0).wait()
        row_copy(0, 0).wait()
        return carry

    lax.fori_loop(0, RD, drain, 0, unroll=8)


def _dispatch(pos_flat, x1, xs_init):
    return pl.pallas_call(
        _dispatch_kernel,
        out_shape=jax.ShapeDtypeStruct((N_SLOTS, D_MODEL), F32),
        grid_spec=pltpu.PrefetchScalarGridSpec(
            num_scalar_prefetch=1,
            grid=(TOKENS // RD,),
            in_specs=[pl.BlockSpec((RD, D_MODEL), lambda i, pos: (i, 0)),
                      pl.BlockSpec(memory_space=pl.ANY)],
            out_specs=pl.BlockSpec(memory_space=pl.ANY),
            scratch_shapes=[pltpu.SemaphoreType.DMA],
        ),
        input_output_aliases={2: 0},
        compiler_params=pltpu.CompilerParams(
            dimension_semantics=("arbitrary",), has_side_effects=True),
        name="moe_dispatch",
    )(pos_flat, x1, xs_init)


def _expert_kernel(te_ref, nu_ref, xs_ref, w1_ref, w3_ref, w2_ref, o_ref, w1b, w3b, w2b):
    i = pl.program_id(0)
    e = te_ref[i]
    prev = te_ref[jnp.maximum(i - 1, 0)]

    @pl.when(jnp.logical_or(i == 0, e != prev))
    def _():
        w1b[...] = w1_ref[...].astype(BF16)
        w3b[...] = w3_ref[...].astype(BF16)
        w2b[...] = w2_ref[...].astype(BF16)

    @pl.when(i < nu_ref[0])
    def _():
        xb = xs_ref[...].astype(BF16)
        h1 = jnp.dot(xb, w1b[...], preferred_element_type=F32)
        h3 = jnp.dot(xb, w3b[...], preferred_element_type=F32)
        h = (h1 * _sigmoid(h1)) * h3
        o_ref[...] = jnp.dot(h.astype(BF16), w2b[...], preferred_element_type=F32)

    @pl.when(i >= nu_ref[0])
    def _():
        o_ref[...] = jnp.zeros_like(o_ref)


def _experts(te, nu, xs, w1, w3, w2, layer):
    wspec = lambda r, c: pl.BlockSpec((None, None, r, c),
                                      lambda i, te, nu: (layer, te[i], 0, 0))
    return pl.pallas_call(
        _expert_kernel,
        out_shape=jax.ShapeDtypeStruct((N_SLOTS, D_MODEL), F32),
        grid_spec=pltpu.PrefetchScalarGridSpec(
            num_scalar_prefetch=2,
            grid=(N_MTILES,),
            in_specs=[pl.BlockSpec((TM, D_MODEL), lambda i, te, nu: (i, 0)),
                      wspec(D_MODEL, D_EXPERT), wspec(D_MODEL, D_EXPERT),
                      wspec(D_EXPERT, D_MODEL)],
            out_specs=pl.BlockSpec((TM, D_MODEL), lambda i, te, nu: (i, 0)),
            scratch_shapes=[pltpu.VMEM((D_MODEL, D_EXPERT), BF16),
                            pltpu.VMEM((D_MODEL, D_EXPERT), BF16),
                            pltpu.VMEM((D_EXPERT, D_MODEL), BF16)],
        ),
        compiler_params=pltpu.CompilerParams(
            dimension_semantics=("arbitrary",), vmem_limit_bytes=VMEM_LIMIT),
        name="moe_experts",
    )(te, nu, xs, w1, w3, w2)


def _combine_kernel(pos_ref, x1_ref, p_ref, wcol_ref, ys_ref, wp_ref, wg_ref, g_ref, b_ref,
                    o_ref, ybuf, sem):
    base = pl.program_id(0) * RD

    def row_copy(k, r, slot):
        return pltpu.make_async_copy(ys_ref.at[pl.ds(slot, 1)], ybuf.at[k, pl.ds(r, 1)], sem)

    def issue(r, carry):
        row_copy(0, r, pos_ref[base + r]).start()
        row_copy(1, r, pos_ref[TOKENS + base + r]).start()
        return carry

    lax.fori_loop(0, RD, issue, 0, unroll=8)

    x1 = x1_ref[...]
    gate = _sigmoid(jnp.dot(x1.astype(BF16), wg_ref[...], preferred_element_type=F32))
    ple = jnp.dot(p_ref[...].astype(BF16), wp_ref[...], preferred_element_type=F32) * gate

    def drain(r, carry):
        row_copy(0, 0, 0).wait()
        row_copy(1, 0, 0).wait()
        return carry

    lax.fori_loop(0, RD, drain, 0, unroll=8)

    wc = wcol_ref[...]
    ffn = wc[:, 0:1] * ybuf[0] + wc[:, 1:2] * ybuf[1]
    o_ref[...] = _layer_norm(ALPHA * x1 + ffn + ple, g_ref[...], b_ref[...])


def _combine(pos_flat, x1, p, wcol, ys, wp, wg, g, b):
    full = lambda shape: pl.BlockSpec(shape, lambda i, pos: (0,) * len(shape))
    return pl.pallas_call(
        _combine_kernel,
        out_shape=jax.ShapeDtypeStruct((TOKENS, D_MODEL), F32),
        grid_spec=pltpu.PrefetchScalarGridSpec(
            num_scalar_prefetch=1,
            grid=(TOKENS // RD,),
            in_specs=[pl.BlockSpec((RD, D_MODEL), lambda i, pos: (i, 0)),
                      pl.BlockSpec((RD, PLE_DIM), lambda i, pos: (i, 0)),
                      pl.BlockSpec((RD, 128), lambda i, pos: (i, 0)),
                      pl.BlockSpec(memory_space=pl.ANY),
                      full((PLE_DIM, D_MODEL)),
                      full((D_MODEL, D_MODEL)),
                      full((1, D_MODEL)),
                      full((1, D_MODEL))],
            out_specs=pl.BlockSpec((RD, D_MODEL), lambda i, pos: (i, 0)),
            scratch_shapes=[pltpu.VMEM((2, RD, D_MODEL), F32),
                            pltpu.SemaphoreType.DMA],
        ),
        compiler_params=pltpu.CompilerParams(
            dimension_semantics=("arbitrary",), vmem_limit_bytes=VMEM_LIMIT),
        name="moe_combine",
    )(pos_flat, x1, p, wcol, ys, wp, wg, g, b)


def _qkv_kernel(x_ref, w_ref, o_ref):
    o_ref[...] = jnp.dot(x_ref[...].astype(BF16), w_ref[...],
                         preferred_element_type=F32).astype(BF16)


def _qkv_proj(x, w):
    return pl.pallas_call(
        _qkv_kernel,
        out_shape=jax.ShapeDtypeStruct((TOKENS, 3 * D_MODEL), BF16),
        grid=(TOKENS // RP, 3),
        in_specs=[pl.BlockSpec((RP, D_MODEL), lambda i, j: (i, 0)),
                  pl.BlockSpec((D_MODEL, D_MODEL), lambda i, j: (0, j))],
        out_specs=pl.BlockSpec((RP, D_MODEL), lambda i, j: (i, j)),
        compiler_params=pltpu.CompilerParams(
            dimension_semantics=("arbitrary", "arbitrary"), vmem_limit_bytes=VMEM_LIMIT),
        name="qkv_proj",
    )(x, w)


def _attn_kernel(q_ref, k_ref, v_ref, o_ref, acc, cbuf):
    qb = pl.program_id(2)
    lane = lax.broadcasted_iota(I32, (1, 128), 1)
    qh = []
    for pr in range(ATT_PAIRS):
        q = q_ref[:, pr * 128:(pr + 1) * 128] * jnp.asarray(0.125, BF16)
        zero = jnp.zeros_like(q)
        qh.append((jnp.where(lane < 64, q, zero), jnp.where(lane >= 64, q, zero)))
    r = lax.broadcasted_iota(I32, (SB_BLOCK, SB_BLOCK), 0)
    cidx = lax.broadcasted_iota(I32, (SB_BLOCK, SB_BLOCK), 1)
    neg_suffix = jnp.where(r >= cidx, -1.0, 0.0).astype(BF16)
    causal = cidx < r

    acc[...] = jnp.zeros_like(acc)
    cbuf[...] = jnp.zeros_like(cbuf)

    def block(kb, masked):
        rows = pl.ds(pl.multiple_of(kb * SB_BLOCK, SB_BLOCK), SB_BLOCK)
        heads = [(pr, hh) for pr in range(ATT_PAIRS) for hh in range(2)]
        ks = [k_ref[rows, pr * 128:(pr + 1) * 128] for pr in range(ATT_PAIRS)]
        vs = [v_ref[rows, pr * 128:(pr + 1) * 128] for pr in range(ATT_PAIRS)]
        zs = [lax.dot_general(qh[pr][hh], ks[pr], NT_DIMS, preferred_element_type=F32)
              for pr, hh in heads]
        sps = []
        for z in zs:
            nabs = lax.bitcast_convert_type(
                lax.bitcast_convert_type(z, jnp.uint32) | jnp.uint32(SIGN_BIT), F32)
            sp = jnp.maximum(z, 0.0) + jnp.log(1.0 + jnp.exp(nabs))
            if masked:
                sp = jnp.where(causal, sp, 0.0)
            sps.append(sp.astype(BF16))
        rss = []
        for idx, sp in enumerate(sps):
            c = cbuf[idx]
            rss.append(jnp.dot(sp, neg_suffix, preferred_element_type=F32)
                       + jnp.concatenate([c, c], axis=1))
        for idx, (pr, hh) in enumerate(heads):
            w = jnp.exp(zs[idx] + rss[idx])
            if masked:
                w = jnp.where(causal, w, 0.0)
            acc[idx] = acc[idx] + jnp.dot(w.astype(BF16), vs[pr], preferred_element_type=F32)
            cbuf[idx] = jnp.broadcast_to(rss[idx][:, 0:1], (SB_BLOCK, 128))

    block(qb, True)

    def body(j, carry):
        block(qb - 1 - j, False)
        return carry

    lax.fori_loop(0, qb, body, 0)
    for pr in range(ATT_PAIRS):
        o_ref[:, pr * 128:(pr + 1) * 128] = jnp.where(
            lane < 64, acc[2 * pr], acc[2 * pr + 1]).astype(BF16)


def _attention(qkv):
    n_hg = D_MODEL // ATT_LANES
    return pl.pallas_call(
        _attn_kernel,
        out_shape=jax.ShapeDtypeStruct((BATCH, SEQ, D_MODEL), BF16),
        grid=(BATCH, n_hg, SEQ // SB_BLOCK),
        in_specs=[pl.BlockSpec((None, SB_BLOCK, ATT_LANES), lambda b, h, i: (b, i, h)),
                  pl.BlockSpec((None, SEQ, ATT_LANES), lambda b, h, i: (b, 0, n_hg + h)),
                  pl.BlockSpec((None, SEQ, ATT_LANES), lambda b, h, i: (b, 0, 2 * n_hg + h))],
        out_specs=pl.BlockSpec((None, SB_BLOCK, ATT_LANES), lambda b, h, i: (b, i, h)),
        scratch_shapes=[pltpu.VMEM((2 * ATT_PAIRS, SB_BLOCK, 128), F32),
                        pltpu.VMEM((2 * ATT_PAIRS, SB_BLOCK, 128), F32)],
        compiler_params=pltpu.CompilerParams(
            dimension_semantics=("arbitrary", "arbitrary", "arbitrary"),
            vmem_limit_bytes=VMEM_LIMIT),
        name="sb_attention",
    )(qkv, qkv, qkv)


def _moe_ffn(x1, ri, wcol, p_rows, layer, moe_w1, moe_w3, moe_w2, wp, wg, g, b):
    pos, te = _rank(ri)
    pos_flat = pos[0:2].reshape(2 * TOKENS)
    xs = _dispatch(pos_flat, x1, jnp.zeros((N_SLOTS, D_MODEL), F32))
    ys = _experts(te[0, :N_MTILES], te[1, :1], xs, moe_w1, moe_w3, moe_w2, layer)
    return _combine(pos_flat, x1, p_rows, wcol, ys, wp, wg, g, b)


def kernel(x, p, ab_w_in, s5_lambda_re, s5_lambda_im, s5_log_dt, s5_b_re, s5_b_im, s5_c_re, s5_c_im, s5_d, s5_w_glu, pool_w, pool_scale, ab_w_out, sb_w_qkv, sb_w_out, ln_mix_g, ln_mix_b, ln_ffn_g, ln_ffn_b, router_w, router_bias, moe_w1, moe_w3, moe_w2, ple_w_proj, ple_w_gate):
    row = lambda a: a.reshape(1, -1)
    rwt = router_w.T
    rwh = rwt.astype(BF16)
    rwl = (rwt - rwh.astype(F32)).astype(BF16)
    rb = router_bias.reshape(N_EXPERTS, 1)

    x_tm = x.transpose(1, 0, 2).reshape(TOKENS, D_MODEL)
    p_tm = p[0].transpose(1, 0, 2).reshape(TOKENS, PLE_DIM)
    a_re, a_im, bcat = _s5_prep(s5_lambda_re[0], s5_lambda_im[0], s5_log_dt[0],
                                s5_b_re[0], s5_b_im[0])

    def c_blockdiag(c):
        c4 = c.reshape(S5_CHUNKS, 8, S5_H, S5_P).transpose(0, 1, 3, 2)
        eye = jnp.eye(8, dtype=F32)
        full = c4[:, :, :, None, :] * eye[None, :, None, :, None]
        return full.reshape(S5_CHUNKS, 8 * S5_P, 8 * S5_H)

    ccat = jnp.concatenate([c_blockdiag(s5_c_re[0]), -c_blockdiag(s5_c_im[0])],
                           axis=1).astype(BF16)
    mix_in = _mixer0(x_tm, ab_w_in[0].astype(BF16), bcat, a_re, a_im, ccat,
                     row(s5_d[0]), s5_w_glu[0].astype(BF16), pool_w[0].astype(BF16),
                     row(pool_scale[0]))
    x1, ri, wcol = _proj_ln_route(mix_in, ab_w_out[0].astype(BF16), x_tm,
                                  row(ln_mix_g[0]), row(ln_mix_b[0]), rwh, rwl, rb)
    x2 = _moe_ffn(x1, ri, wcol, p_tm, 0, moe_w1, moe_w3, moe_w2,
                  ple_w_proj[0].astype(BF16), ple_w_gate[0].astype(BF16),
                  row(ln_ffn_g[0]), row(ln_ffn_b[0]))

    xb = x2.reshape(SEQ, BATCH, D_MODEL).transpose(1, 0, 2).reshape(TOKENS, D_MODEL)
    qkv = _qkv_proj(xb, sb_w_qkv[0].astype(BF16))
    att = _attention(qkv.reshape(BATCH, SEQ, 3 * D_MODEL)).reshape(TOKENS, D_MODEL)
    x3, ri, wcol = _proj_ln_route(att, sb_w_out[0].astype(BF16), xb,
                                  row(ln_mix_g[1]), row(ln_mix_b[1]), rwh, rwl, rb)
    x4 = _moe_ffn(x3, ri, wcol, p[1].reshape(TOKENS, PLE_DIM), 1, moe_w1, moe_w3, moe_w2,
                  ple_w_proj[1].astype(BF16), ple_w_gate[1].astype(BF16),
                  row(ln_ffn_g[1]), row(ln_ffn_b[1]))
    return x4.reshape(BATCH, SEQ, D_MODEL)
```

```python
import functools
import math

import jax
import jax.numpy as jnp
from jax import lax
from jax.experimental import pallas as pl
from jax.experimental.pallas import tpu as pltpu

F32 = jnp.float32
BF16 = jnp.bfloat16
I32 = jnp.int32

D_MODEL = 1024
BATCH = 8
SEQ = 2048
DEPTH = 2
TOKENS = BATCH * SEQ

D_A = 512
S5_H = 16
S5_G = 32
S5_P = 64
S5_STATE = S5_G * S5_P
S5_CHUNKS = 4
POOL_WINDOWS = (2, 4, 8, 16)
POOL_C = 128
POOL_HALO = 16 * BATCH
SB_BLOCK = 256
ATT_PAIRS = 4
ATT_LANES = 128 * ATT_PAIRS
N_EXPERTS = 16
EXPERTS_PER_GROUP = 4
D_EXPERT = 512
PLE_DIM = 256
ALPHA = (2 * DEPTH) ** 0.25
LN_EPS = 1e-5

TS0 = 64
R0 = TS0 * BATCH
RP = 512
RR = 512
RD = 256
TM = 256
N_SLOTS = 2 * TOKENS + N_EXPERTS * TM
N_MTILES = N_SLOTS // TM
TE_LANES = 256

VMEM_LIMIT = 52 * 1024 * 1024
NT_DIMS = (((1,), (1,)), ((), ()))
LOG2E = 1.0 / math.log(2.0)
EXIT_LOG_WEIGHT = -110.0
BOUND_SLACK = 1.01


def _sigmoid(x):
    return 1.0 / (1.0 + jnp.exp(-x))


def _layer_norm(x, g, b):
    mu = jnp.mean(x, axis=-1, keepdims=True)
    xc = x - mu
    var = jnp.mean(xc * xc, axis=-1, keepdims=True)
    return xc * lax.rsqrt(var + LN_EPS) * g + b


def _s5_prep_kernel(lre_ref, lim_ref, ldt_ref, bre_ref, bim_ref,
                    are_ref, aim_ref, bcat_ref):
    lam_re = lre_ref[...]
    lam_im = lim_ref[...]
    dt = jnp.exp(ldt_ref[...])
    mag = jnp.exp(lam_re * dt)
    ang = lam_im * dt
    lb_re = mag * jnp.cos(ang)
    lb_im = mag * jnp.sin(ang)
    are_ref[...] = lb_re
    aim_ref[...] = lb_im
    den = lam_re * lam_re + lam_im * lam_im
    num_re = lb_re - 1.0
    f_re = (num_re * lam_re + lb_im * lam_im) / den
    f_im = (lb_im * lam_re - num_re * lam_im) / den
    for c in range(S5_CHUNKS):
        fr = f_re[:, c * 512:(c + 1) * 512]
        fi = f_im[:, c * 512:(c + 1) * 512]
        br = bre_ref[c]
        bi = bim_ref[c]
        bcat_ref[c, :, 0:512] = (fr * br - fi * bi).astype(BF16)
        bcat_ref[c, :, 512:1024] = (fr * bi + fi * br).astype(BF16)


def _s5_prep(lam_re, lam_im, log_dt, b_re, b_im):
    def blockdiag(b):
        b4 = b.reshape(S5_CHUNKS, 8, S5_P, S5_H).transpose(0, 1, 3, 2)
        eye = jnp.eye(8, dtype=F32)
        full = b4[:, :, :, None, :] * eye[None, :, None, :, None]
        return full.reshape(S5_CHUNKS, 8 * S5_H, 8 * S5_P)

    return pl.pallas_call(
        _s5_prep_kernel,
        out_shape=(jax.ShapeDtypeStruct((1, S5_STATE), F32),
                   jax.ShapeDtypeStruct((1, S5_STATE), F32),
                   jax.ShapeDtypeStruct((S5_CHUNKS, 128, 1024), BF16)),
        name="s5_prep",
    )(lam_re.reshape(1, S5_STATE), lam_im.reshape(1, S5_STATE),
      jnp.repeat(log_dt, S5_P).reshape(1, S5_STATE), blockdiag(b_re), blockdiag(b_im))


def _mixer0_kernel(x_ref, win_ref, bcat_ref, are_ref, aim_ref, ccat_ref, d_ref,
                   wglu_ref, poolw_ref, pscale_ref, o_ref, bus, hst, pe):
    i = pl.program_id(0)

    @pl.when(i == 0)
    def _():
        hst[...] = jnp.zeros_like(hst)
        pe[0:POOL_HALO, :] = jnp.zeros((POOL_HALO, D_A), F32)

    h = jnp.dot(x_ref[...].astype(BF16), win_ref[...], preferred_element_type=F32)
    u = h[:, :D_A]
    v = h[:, D_A:]
    ub = u.astype(BF16)

    for c in range(S5_CHUNKS):
        bus[:, c * 1024:(c + 1) * 1024] = jnp.dot(
            ub[:, c * 128:(c + 1) * 128], bcat_ref[c], preferred_element_type=F32)

    for c in range(S5_CHUNKS):
        re_cols = slice(c * 1024, c * 1024 + 512)
        im_cols = slice(c * 1024 + 512, (c + 1) * 1024)
        ar = jnp.broadcast_to(are_ref[:, c * 512:(c + 1) * 512], (BATCH, 512))
        ai = jnp.broadcast_to(aim_ref[:, c * 512:(c + 1) * 512], (BATCH, 512))

        def step(t, carry, re_cols=re_cols, im_cols=im_cols, ar=ar, ai=ai):
            hr, hi = carry
            rows = pl.ds(pl.multiple_of(t * BATCH, BATCH), BATCH)
            nr = ar * hr - ai * hi + bus[rows, re_cols]
            ni = ar * hi + ai * hr + bus[rows, im_cols]
            bus[rows, re_cols] = nr
            bus[rows, im_cols] = ni
            return nr, ni

        hr, hi = lax.fori_loop(0, TS0, step, (hst[:, re_cols], hst[:, im_cols]), unroll=8)
        hst[:, re_cols] = hr
        hst[:, im_cols] = hi

    ys = [jnp.dot(bus[:, c * 1024:(c + 1) * 1024].astype(BF16), ccat_ref[c],
                  preferred_element_type=F32) for c in range(S5_CHUNKS)]
    y = jnp.concatenate(ys, axis=1) + d_ref[...] * u
    y = 0.5 * y * (1.0 + jnp.tanh(math.sqrt(2.0 / math.pi) * (y + 0.044715 * (y * y * y))))
    ga = y * _sigmoid(jnp.dot(y.astype(BF16), wglu_ref[...], preferred_element_type=F32))
    o_ref[:, 0:D_A] = ga.astype(BF16)

    pe[POOL_HALO:, :] = v
    t_glob = lax.shift_right_logical(
        lax.broadcasted_iota(I32, (R0, 1), 0), int(math.log2(BATCH))) + i * TS0
    for gi, w in enumerate(POOL_WINDOWS):
        cols = slice(gi * POOL_C, (gi + 1) * POOL_C)
        s = pe[:, cols]
        off = BATCH
        while off < BATCH * w:
            s = s[off:] + s[:-off]
            off *= 2
        s = s[POOL_HALO - BATCH * (w - 1):]
        cnt = jnp.minimum(t_glob + 1, w).astype(F32)
        pooled = s / cnt - v[:, cols]
        mixed = jnp.dot(pooled.astype(BF16), poolw_ref[gi], preferred_element_type=F32)
        o_ref[:, D_A + gi * POOL_C:D_A + (gi + 1) * POOL_C] = (
            mixed * pscale_ref[:, cols]).astype(BF16)
    pe[0:POOL_HALO, :] = pe[R0:R0 + POOL_HALO, :]


def _mixer0(x_tm, win, bcat, a_re, a_im, ccat, dskip, wglu, poolw, pscale):
    full = lambda shape: pl.BlockSpec(shape, lambda i: (0,) * len(shape))
    return pl.pallas_call(
        _mixer0_kernel,
        out_shape=jax.ShapeDtypeStruct((TOKENS, D_MODEL), BF16),
        grid=(SEQ // TS0,),
        in_specs=[
            pl.BlockSpec((R0, D_MODEL), lambda i: (i, 0)),
            full((D_MODEL, D_MODEL)),
            full((S5_CHUNKS, 128, 1024)),
            full((1, S5_STATE)),
            full((1, S5_STATE)),
            full((S5_CHUNKS, 1024, 128)),
            full((1, D_A)),
            full((D_A, D_A)),
            full((4, POOL_C, POOL_C)),
            full((1, D_A)),
        ],
        out_specs=pl.BlockSpec((R0, D_MODEL), lambda i: (i, 0)),
        scratch_shapes=[
            pltpu.VMEM((R0, 2 * S5_STATE), F32),
            pltpu.VMEM((BATCH, 2 * S5_STATE), F32),
            pltpu.VMEM((POOL_HALO + R0, D_A), F32),
        ],
        compiler_params=pltpu.CompilerParams(
            dimension_semantics=("arbitrary",), vmem_limit_bytes=VMEM_LIMIT),
        name="mixer0",
    )(x_tm, win, bcat, a_re, a_im, ccat, dskip, wglu, poolw, pscale)


def _proj_ln_route_kernel(a_ref, w_ref, x_ref, g_ref, b_ref, rwh_ref, rwl_ref, rb_ref,
                          x1_ref, ri_ref, wcol_ref):
    mix = jnp.dot(a_ref[...], w_ref[...], preferred_element_type=F32)
    x1 = _layer_norm(ALPHA * x_ref[...] + mix, g_ref[...], b_ref[...])
    x1_ref[...] = x1

    xh = x1.astype(BF16)
    xl = (x1 - xh.astype(F32)).astype(BF16)
    rwh = rwh_ref[...]
    logits = (lax.dot_general(rwh, xh, NT_DIMS, preferred_element_type=F32)
              + lax.dot_general(rwh, xl, NT_DIMS, preferred_element_type=F32)
              + lax.dot_general(rwl_ref[...], xh, NT_DIMS, preferred_element_type=F32))
    scores = _sigmoid(logits)
    sel = scores + rb_ref[...]
    row = lambda a, e: a[e:e + 1, :]

    best = None
    grp = None
    for g in range(N_EXPERTS // EXPERTS_PER_GROUP):
        m = [row(sel, EXPERTS_PER_GROUP * g + k) for k in range(EXPERTS_PER_GROUP)]
        gs = None
        for a in range(EXPERTS_PER_GROUP):
            for b in range(a + 1, EXPERTS_PER_GROUP):
                pair = m[a] + m[b]
                gs = pair if gs is None else jnp.maximum(gs, pair)
        if best is None:
            best, grp = gs, jnp.zeros(gs.shape, I32)
        else:
            better = gs > best
            grp = jnp.where(better, g, grp)
            best = jnp.where(better, gs, best)

    def pick(a, k):
        out = row(a, k)
        for g in range(1, N_EXPERTS // EXPERTS_PER_GROUP):
            out = jnp.where(grp == g, row(a, EXPERTS_PER_GROUP * g + k), out)
        return out

    cs = [pick(sel, k) for k in range(EXPERTS_PER_GROUP)]
    ss = [pick(scores, k) for k in range(EXPERTS_PER_GROUP)]
    m1, i1, s1 = cs[0], jnp.zeros(cs[0].shape, I32), ss[0]
    for k in range(1, EXPERTS_PER_GROUP):
        better = cs[k] > m1
        i1 = jnp.where(better, k, i1)
        s1 = jnp.where(better, ss[k], s1)
        m1 = jnp.where(better, cs[k], m1)
    m2 = jnp.full(m1.shape, -jnp.inf, F32)
    i2 = jnp.zeros(m1.shape, I32)
    s2 = jnp.zeros(m1.shape, F32)
    for k in range(EXPERTS_PER_GROUP):
        better = jnp.logical_and(i1 != k, cs[k] > m2)
        i2 = jnp.where(better, k, i2)
        s2 = jnp.where(better, ss[k], s2)
        m2 = jnp.where(better, cs[k], m2)
    e0 = grp * EXPERTS_PER_GROUP + i1
    e1 = grp * EXPERTS_PER_GROUP + i2
    tot = s1 + s2
    w0 = s1 / tot
    w1 = s2 / tot
    ri_ref[...] = jnp.concatenate([e0, e1, jnp.zeros((6, RP), I32)], axis=0)
    wslab = jnp.concatenate([w0, w1, jnp.zeros((126, RP), F32)], axis=0)
    wcol_ref[...] = wslab.T


def _proj_ln_route(a, w, xres, g, b, rwh, rwl, rb):
    full = lambda shape: pl.BlockSpec(shape, lambda i: (0,) * len(shape))
    return pl.pallas_call(
        _proj_ln_route_kernel,
        out_shape=(jax.ShapeDtypeStruct((TOKENS, D_MODEL), F32),
                   jax.ShapeDtypeStruct((8, TOKENS), I32),
                   jax.ShapeDtypeStruct((TOKENS, 128), F32)),
        grid=(TOKENS // RP,),
        in_specs=[
            pl.BlockSpec((RP, D_MODEL), lambda i: (i, 0)),
            full((D_MODEL, D_MODEL)),
            pl.BlockSpec((RP, D_MODEL), lambda i: (i, 0)),
            full((1, D_MODEL)),
            full((1, D_MODEL)),
            full((N_EXPERTS, D_MODEL)),
            full((N_EXPERTS, D_MODEL)),
            full((N_EXPERTS, 1)),
        ],
        out_specs=(pl.BlockSpec((RP, D_MODEL), lambda i: (i, 0)),
                   pl.BlockSpec((8, RP), lambda i: (0, i)),
                   pl.BlockSpec((RP, 128), lambda i: (i, 0))),
        compiler_params=pltpu.CompilerParams(
            dimension_semantics=("arbitrary",), vmem_limit_bytes=VMEM_LIMIT),
        name="proj_ln_route",
    )(a, w, xres, g, b, rwh, rwl, rb)


def _rank_kernel(ri_ref, pos_ref, te_ref, cnt, offs, tot):
    ph = pl.program_id(0)
    i = pl.program_id(1)
    n_tiles = pl.num_programs(1)
    e_iota = lax.broadcasted_iota(I32, (N_EXPERTS, RR), 0)
    m0 = e_iota == ri_ref[0:1, :]
    m1 = e_iota == ri_ref[1:2, :]
    mask = jnp.where(m0, 1.0, 0.0) + jnp.where(m1, 1.0, 0.0)
    tile_cnt = jnp.sum(mask, axis=1, keepdims=True)

    @pl.when(jnp.logical_and(ph == 0, i == 0))
    def _():
        cnt[...] = jnp.zeros_like(cnt)

    @pl.when(ph == 0)
    def _():
        cnt[...] = cnt[...] + tile_cnt
        pos_ref[...] = jnp.zeros_like(pos_ref)

    @pl.when(jnp.logical_and(ph == 0, i == n_tiles - 1))
    def _():
        c = cnt[...]
        padded = jnp.ceil(c * (1.0 / TM)) * TM
        sub = lax.broadcasted_iota(I32, (N_EXPERTS, 128), 0)
        acc = jnp.zeros((N_EXPERTS, 128), F32)
        for e in range(N_EXPERTS):
            acc = acc + jnp.where(sub > e, padded[e:e + 1, :], 0.0)
        offs[...] = acc
        tot[...] = acc + padded
        cnt[...] = jnp.zeros_like(cnt)

    @pl.when(ph == 1)
    def _():
        r = lax.broadcasted_iota(I32, (RR, RR), 0)
        cidx = lax.broadcasted_iota(I32, (RR, RR), 1)
        tri = jnp.where(r < cidx, 1.0, 0.0).astype(BF16)
        prefix = jnp.dot(mask.astype(BF16), tri, preferred_element_type=F32)
        slot = prefix + cnt[:, 0:1] + offs[:, 0:1]
        p0 = jnp.sum(jnp.where(m0, slot, 0.0), axis=0, keepdims=True)
        p1 = jnp.sum(jnp.where(m1, slot, 0.0), axis=0, keepdims=True)
        pos_ref[...] = jnp.concatenate(
            [p0.astype(I32), p1.astype(I32), jnp.zeros((6, RR), I32)], axis=0)
        cnt[...] = cnt[...] + tile_cnt

    @pl.when(jnp.logical_and(ph == 1, i == n_tiles - 1))
    def _():
        ends = tot[:, 0:1]
        start = (lax.broadcasted_iota(I32, (N_EXPERTS, TE_LANES), 1) * TM).astype(F32)
        te = jnp.sum(jnp.where(start >= ends, 1.0, 0.0), axis=0, keepdims=True)
        te = jnp.minimum(te, N_EXPERTS - 1.0).astype(I32)
        n_used = (tot[N_EXPERTS - 1:N_EXPERTS, :] * (1.0 / TM)).astype(I32)
        n_used = jnp.concatenate([n_used, n_used], axis=1)
        te_ref[...] = jnp.concatenate(
            [te, n_used, jnp.zeros((6, TE_LANES), I32)], axis=0)


def _rank(ri):
    return pl.pallas_call(
        _rank_kernel,
        out_shape=(jax.ShapeDtypeStruct((8, TOKENS), I32),
                   jax.ShapeDtypeStruct((8, TE_LANES), I32)),
        grid=(2, TOKENS // RR),
        in_specs=[pl.BlockSpec((8, RR), lambda p, i: (0, i))],
        out_specs=(pl.BlockSpec((8, RR), lambda p, i: (0, i * p)),
                   pl.BlockSpec((8, TE_LANES), lambda p, i: (0, 0))),
        scratch_shapes=[pltpu.VMEM((N_EXPERTS, 128), F32),
                        pltpu.VMEM((N_EXPERTS, 128), F32),
                        pltpu.VMEM((N_EXPERTS, 128), F32)],
        compiler_params=pltpu.CompilerParams(
            dimension_semantics=("arbitrary", "arbitrary")),
        name="moe_rank",
    )(ri)


def _dispatch_kernel(pos_ref, x_ref, xs_in_ref, xs_ref, sem):
    del xs_in_ref
    base = pl.program_id(0) * RD

    def row_copy(r, slot):
        return pltpu.make_async_copy(x_ref.at[pl.ds(r, 1)], xs_ref.at[pl.ds(slot, 1)], sem)

    def issue(r, carry):
        row_copy(r, pos_ref[base + r]).start()
        row_copy(r, pos_ref[TOKENS + base + r]).start()
        return carry

    lax.fori_loop(0, RD, issue, 0, unroll=8)

    def drain(r, carry):
        row_copy(0, 0).wait()
        row_copy(0, 0).wait()
        return carry

    lax.fori_loop(0, RD, drain, 0, unroll=8)


def _dispatch(pos_flat, x1, xs_init):
    return pl.pallas_call(
        _dispatch_kernel,
        out_shape=jax.ShapeDtypeStruct((N_SLOTS, D_MODEL), F32),
        grid_spec=pltpu.PrefetchScalarGridSpec(
            num_scalar_prefetch=1,
            grid=(TOKENS // RD,),
            in_specs=[pl.BlockSpec((RD, D_MODEL), lambda i, pos: (i, 0)),
                      pl.BlockSpec(memory_space=pl.ANY)],
            out_specs=pl.BlockSpec(memory_space=pl.ANY),
            scratch_shapes=[pltpu.SemaphoreType.DMA],
        ),
        input_output_aliases={2: 0},
        compiler_params=pltpu.CompilerParams(
            dimension_semantics=("arbitrary",), has_side_effects=True),
        name="moe_dispatch",
    )(pos_flat, x1, xs_init)


def _expert_kernel(te_ref, nu_ref, xs_ref, w1_ref, w3_ref, w2_ref, o_ref, w1b, w3b, w2b):
    i = pl.program_id(0)
    e = te_ref[i]
    prev = te_ref[jnp.maximum(i - 1, 0)]

    @pl.when(jnp.logical_or(i == 0, e != prev))
    def _():
        w1b[...] = w1_ref[...].astype(BF16)
        w3b[...] = w3_ref[...].astype(BF16)
        w2b[...] = w2_ref[...].astype(BF16)

    @pl.when(i < nu_ref[0])
    def _():
        xb = xs_ref[...].astype(BF16)
        h1 = jnp.dot(xb, w1b[...], preferred_element_type=F32)
        h3 = jnp.dot(xb, w3b[...], preferred_element_type=F32)
        h = (h1 * _sigmoid(h1)) * h3
        o_ref[...] = jnp.dot(h.astype(BF16), w2b[...], preferred_element_type=F32)

    @pl.when(i >= nu_ref[0])
    def _():
        o_ref[...] = jnp.zeros_like(o_ref)


def _experts(te, nu, xs, w1, w3, w2, layer):
    wspec = lambda r, c: pl.BlockSpec((None, None, r, c),
                                      lambda i, te, nu: (layer, te[i], 0, 0))
    return pl.pallas_call(
        _expert_kernel,
        out_shape=jax.ShapeDtypeStruct((N_SLOTS, D_MODEL), F32),
        grid_spec=pltpu.PrefetchScalarGridSpec(
            num_scalar_prefetch=2,
            grid=(N_MTILES,),
            in_specs=[pl.BlockSpec((TM, D_MODEL), lambda i, te, nu: (i, 0)),
                      wspec(D_MODEL, D_EXPERT), wspec(D_MODEL, D_EXPERT),
                      wspec(D_EXPERT, D_MODEL)],
            out_specs=pl.BlockSpec((TM, D_MODEL), lambda i, te, nu: (i, 0)),
            scratch_shapes=[pltpu.VMEM((D_MODEL, D_EXPERT), BF16),
                            pltpu.VMEM((D_MODEL, D_EXPERT), BF16),
                            pltpu.VMEM((D_EXPERT, D_MODEL), BF16)],
        ),
        compiler_params=pltpu.CompilerParams(
            dimension_semantics=("arbitrary",), vmem_limit_bytes=VMEM_LIMIT),
        name="moe_experts",
    )(te, nu, xs, w1, w3, w2)


def _combine_kernel(pos_ref, x1_ref, p_ref, wcol_ref, ys_ref, wp_ref, wg_ref, g_ref, b_ref,
                    o_ref, ybuf, sem):
    base = pl.program_id(0) * RD

    def row_copy(k, r, slot):
        return pltpu.make_async_copy(ys_ref.at[pl.ds(slot, 1)], ybuf.at[k, pl.ds(r, 1)], sem)

    def issue(r, carry):
        row_copy(0, r, pos_ref[base + r]).start()
        row_copy(1, r, pos_ref[TOKENS + base + r]).start()
        return carry

    lax.fori_loop(0, RD, issue, 0, unroll=8)

    x1 = x1_ref[...]
    gate = _sigmoid(jnp.dot(x1.astype(BF16), wg_ref[...], preferred_element_type=F32))
    ple = jnp.dot(p_ref[...].astype(BF16), wp_ref[...], preferred_element_type=F32) * gate

    def drain(r, carry):
        row_copy(0, 0, 0).wait()
        row_copy(1, 0, 0).wait()
        return carry

    lax.fori_loop(0, RD, drain, 0, unroll=8)

    wc = wcol_ref[...]
    ffn = wc[:, 0:1] * ybuf[0] + wc[:, 1:2] * ybuf[1]
    o_ref[...] = _layer_norm(ALPHA * x1 + ffn + ple, g_ref[...], b_ref[...])


def _combine(pos_flat, x1, p, wcol, ys, wp, wg, g, b):
    full = lambda shape: pl.BlockSpec(shape, lambda i, pos: (0,) * len(shape))
    return pl.pallas_call(
        _combine_kernel,
        out_shape=jax.ShapeDtypeStruct((TOKENS, D_MODEL), F32),
        grid_spec=pltpu.PrefetchScalarGridSpec(
            num_scalar_prefetch=1,
            grid=(TOKENS // RD,),
            in_specs=[pl.BlockSpec((RD, D_MODEL), lambda i, pos: (i, 0)),
                      pl.BlockSpec((RD, PLE_DIM), lambda i, pos: (i, 0)),
                      pl.BlockSpec((RD, 128), lambda i, pos: (i, 0)),
                      pl.BlockSpec(memory_space=pl.ANY),
                      full((PLE_DIM, D_MODEL)),
                      full((D_MODEL, D_MODEL)),
                      full((1, D_MODEL)),
                      full((1, D_MODEL))],
            out_specs=pl.BlockSpec((RD, D_MODEL), lambda i, pos: (i, 0)),
            scratch_shapes=[pltpu.VMEM((2, RD, D_MODEL), F32),
                            pltpu.SemaphoreType.DMA],
        ),
        compiler_params=pltpu.CompilerParams(
            dimension_semantics=("arbitrary",), vmem_limit_bytes=VMEM_LIMIT),
        name="moe_combine",
    )(pos_flat, x1, p, wcol, ys, wp, wg, g, b)


def _qkv_kernel(x_ref, w_ref, o_ref):
    o_ref[...] = jnp.dot(x_ref[...].astype(BF16), w_ref[...],
                         preferred_element_type=F32).astype(BF16)


def _qkv_proj(x, w):
    return pl.pallas_call(
        _qkv_kernel,
        out_shape=jax.ShapeDtypeStruct((TOKENS, 3 * D_MODEL), BF16),
        grid=(TOKENS // RP, 3),
        in_specs=[pl.BlockSpec((RP, D_MODEL), lambda i, j: (i, 0)),
                  pl.BlockSpec((D_MODEL, D_MODEL), lambda i, j: (0, j))],
        out_specs=pl.BlockSpec((RP, D_MODEL), lambda i, j: (i, j)),
        compiler_params=pltpu.CompilerParams(
            dimension_semantics=("arbitrary", "arbitrary"), vmem_limit_bytes=VMEM_LIMIT),
        name="qkv_proj",
    )(x, w)


def _attn_kernel(q_ref, k_ref, v_ref, o_ref, acc, cbuf, bnd, kinf):
    qb = pl.program_id(2)
    lane = lax.broadcasted_iota(I32, (1, 128), 1)
    half = (lane < 64, lane >= 64)

    @pl.when(qb == 0)
    def _():
        for pr in range(ATT_PAIRS):
            ka = jnp.max(jnp.abs(k_ref[:, pr * 128:(pr + 1) * 128].astype(F32)),
                         axis=0, keepdims=True)
            for hh in range(2):
                m = jnp.max(jnp.where(half[hh], ka, 0.0), axis=1, keepdims=True)
                kinf[2 * pr + hh] = jnp.broadcast_to(m, (8, 128))

    qh = []
    for pr in range(ATT_PAIRS):
        q = q_ref[:, pr * 128:(pr + 1) * 128] * jnp.asarray(0.125, BF16)
        zero = jnp.zeros_like(q)
        qh.append((jnp.where(half[0], q, zero), jnp.where(half[1], q, zero)))
        qa = jnp.abs(q.astype(F32))
        for hh in range(2):
            qn = jnp.sum(jnp.where(half[hh], qa, 0.0), axis=1, keepdims=True)
            bnd[2 * pr + hh] = (jnp.broadcast_to(qn, (SB_BLOCK, 128))
                                * kinf[2 * pr + hh][0:1, :] * BOUND_SLACK)
    r = lax.broadcasted_iota(I32, (SB_BLOCK, SB_BLOCK), 0)
    cidx = lax.broadcasted_iota(I32, (SB_BLOCK, SB_BLOCK), 1)
    neg_suffix = jnp.where(r >= cidx, -1.0, 0.0).astype(BF16)
    causal = cidx < r

    acc[...] = jnp.zeros_like(acc)
    cbuf[...] = jnp.zeros_like(cbuf)

    def block(kb, masked):
        rows = pl.ds(pl.multiple_of(kb * SB_BLOCK, SB_BLOCK), SB_BLOCK)
        heads = [(pr, hh) for pr in range(ATT_PAIRS) for hh in range(2)]
        ks = [k_ref[rows, pr * 128:(pr + 1) * 128] for pr in range(ATT_PAIRS)]
        vs = [v_ref[rows, pr * 128:(pr + 1) * 128] for pr in range(ATT_PAIRS)]
        zs = [lax.dot_general(qh[pr][hh], ks[pr], NT_DIMS, preferred_element_type=F32)
              for pr, hh in heads]
        sps = []
        for z in zs:
            sp = jnp.maximum(z, 0.0) + jnp.log(1.0 + jnp.exp2(jnp.abs(z) * (-LOG2E)))
            if masked:
                sp = jnp.where(causal, sp, 0.0)
            sps.append(sp.astype(BF16))
        rss = []
        for idx, sp in enumerate(sps):
            c = cbuf[idx]
            rss.append(jnp.dot(sp, neg_suffix, preferred_element_type=F32)
                       + jnp.concatenate([c, c], axis=1))
        for idx, (pr, hh) in enumerate(heads):
            w = jnp.exp(zs[idx] + rss[idx])
            if masked:
                w = jnp.where(causal, w, 0.0)
            acc[idx] = acc[idx] + jnp.dot(w.astype(BF16), vs[pr], preferred_element_type=F32)
            cbuf[idx] = jnp.broadcast_to(rss[idx][:, 0:1], (SB_BLOCK, 128))

    def log_weight_bound():
        m = cbuf[0] + bnd[0]
        for idx in range(1, 2 * ATT_PAIRS):
            m = jnp.maximum(m, cbuf[idx] + bnd[idx])
        return jnp.max(m)

    block(qb, True)

    def cond(carry):
        j, m = carry
        return jnp.logical_and(j < qb, m > EXIT_LOG_WEIGHT)

    def body(carry):
        j, _ = carry
        block(qb - 1 - j, False)
        return j + 1, log_weight_bound()

    lax.while_loop(cond, body, (jnp.int32(0), log_weight_bound()))
    for pr in range(ATT_PAIRS):
        o_ref[:, pr * 128:(pr + 1) * 128] = jnp.where(
            lane < 64, acc[2 * pr], acc[2 * pr + 1]).astype(BF16)


def _attention(qkv):
    n_hg = D_MODEL // ATT_LANES
    return pl.pallas_call(
        _attn_kernel,
        out_shape=jax.ShapeDtypeStruct((BATCH, SEQ, D_MODEL), BF16),
        grid=(BATCH, n_hg, SEQ // SB_BLOCK),
        in_specs=[pl.BlockSpec((None, SB_BLOCK, ATT_LANES), lambda b, h, i: (b, i, h)),
                  pl.BlockSpec((None, SEQ, ATT_LANES), lambda b, h, i: (b, 0, n_hg + h)),
                  pl.BlockSpec((None, SEQ, ATT_LANES), lambda b, h, i: (b, 0, 2 * n_hg + h))],
        out_specs=pl.BlockSpec((None, SB_BLOCK, ATT_LANES), lambda b, h, i: (b, i, h)),
        scratch_shapes=[pltpu.VMEM((2 * ATT_PAIRS, SB_BLOCK, 128), F32),
                        pltpu.VMEM((2 * ATT_PAIRS, SB_BLOCK, 128), F32),
                        pltpu.VMEM((2 * ATT_PAIRS, SB_BLOCK, 128), F32),
                        pltpu.VMEM((2 * ATT_PAIRS, 8, 128), F32)],
        compiler_params=pltpu.CompilerParams(
            dimension_semantics=("arbitrary", "arbitrary", "arbitrary"),
            vmem_limit_bytes=VMEM_LIMIT),
        name="sb_attention",
    )(qkv, qkv, qkv)


def _moe_ffn(x1, ri, wcol, p_rows, layer, moe_w1, moe_w3, moe_w2, wp, wg, g, b):
    pos, te = _rank(ri)
    pos_flat = pos[0:2].reshape(2 * TOKENS)
    xs = _dispatch(pos_flat, x1, jnp.zeros((N_SLOTS, D_MODEL), F32))
    ys = _experts(te[0, :N_MTILES], te[1, :1], xs, moe_w1, moe_w3, moe_w2, layer)
    return _combine(pos_flat, x1, p_rows, wcol, ys, wp, wg, g, b)


def kernel(x, p, ab_w_in, s5_lambda_re, s5_lambda_im, s5_log_dt, s5_b_re, s5_b_im, s5_c_re, s5_c_im, s5_d, s5_w_glu, pool_w, pool_scale, ab_w_out, sb_w_qkv, sb_w_out, ln_mix_g, ln_mix_b, ln_ffn_g, ln_ffn_b, router_w, router_bias, moe_w1, moe_w3, moe_w2, ple_w_proj, ple_w_gate):
    row = lambda a: a.reshape(1, -1)
    rwt = router_w.T
    rwh = rwt.astype(BF16)
    rwl = (rwt - rwh.astype(F32)).astype(BF16)
    rb = router_bias.reshape(N_EXPERTS, 1)

    x_tm = x.transpose(1, 0, 2).reshape(TOKENS, D_MODEL)
    p_tm = p[0].transpose(1, 0, 2).reshape(TOKENS, PLE_DIM)
    a_re, a_im, bcat = _s5_prep(s5_lambda_re[0], s5_lambda_im[0], s5_log_dt[0],
                                s5_b_re[0], s5_b_im[0])

    def c_blockdiag(c):
        c4 = c.reshape(S5_CHUNKS, 8, S5_H, S5_P).transpose(0, 1, 3, 2)
        eye = jnp.eye(8, dtype=F32)
        full = c4[:, :, :, None, :] * eye[None, :, None, :, None]
        return full.reshape(S5_CHUNKS, 8 * S5_P, 8 * S5_H)

    ccat = jnp.concatenate([c_blockdiag(s5_c_re[0]), -c_blockdiag(s5_c_im[0])],
                           axis=1).astype(BF16)
    mix_in = _mixer0(x_tm, ab_w_in[0].astype(BF16), bcat, a_re, a_im, ccat,
                     row(s5_d[0]), s5_w_glu[0].astype(BF16), pool_w[0].astype(BF16),
                     row(pool_scale[0]))
    x1, ri, wcol = _proj_ln_route(mix_in, ab_w_out[0].astype(BF16), x_tm,
                                  row(ln_mix_g[0]), row(ln_mix_b[0]), rwh, rwl, rb)
    x2 = _moe_ffn(x1, ri, wcol, p_tm, 0, moe_w1, moe_w3, moe_w2,
                  ple_w_proj[0].astype(BF16), ple_w_gate[0].astype(BF16),
                  row(ln_ffn_g[0]), row(ln_ffn_b[0]))

    xb = x2.reshape(SEQ, BATCH, D_MODEL).transpose(1, 0, 2).reshape(TOKENS, D_MODEL)
    qkv = _qkv_proj(xb, sb_w_qkv[0].astype(BF16))
    att = _attention(qkv.reshape(BATCH, SEQ, 3 * D_MODEL)).reshape(TOKENS, D_MODEL)
    x3, ri, wcol = _proj_ln_route(att, sb_w_out[0].astype(BF16), xb,
                                  row(ln_mix_g[1]), row(ln_mix_b[1]), rwh, rwl, rb)
    x4 = _moe_ffn(x3, ri, wcol, p[1].reshape(TOKENS, PLE_DIM), 1, moe_w1, moe_w3, moe_w2,
                  ple_w_proj[1].astype(BF16), ple_w_gate[1].astype(BF16),
                  row(ln_ffn_g[1]), row(ln_ffn_b[1]))
    return x4.reshape(BATCH, SEQ, D_MODEL)
```

```python
import functools
import math

import jax
import jax.numpy as jnp
from jax import lax
from jax.experimental import pallas as pl
from jax.experimental.pallas import tpu as pltpu

F32 = jnp.float32
BF16 = jnp.bfloat16
I32 = jnp.int32

D_MODEL = 1024
BATCH = 8
SEQ = 2048
DEPTH = 2
TOKENS = BATCH * SEQ

D_A = 512
S5_H = 16
S5_G = 32
S5_P = 64
S5_STATE = S5_G * S5_P
S5_CHUNKS = 4
POOL_WINDOWS = (2, 4, 8, 16)
POOL_C = 128
POOL_HALO = 16 * BATCH
SB_BLOCK = 256
ATT_PAIRS = 4
ATT_LANES = 128 * ATT_PAIRS
N_EXPERTS = 16
EXPERTS_PER_GROUP = 4
D_EXPERT = 512
PLE_DIM = 256
ALPHA = (2 * DEPTH) ** 0.25
LN_EPS = 1e-5

TS0 = 64
R0 = TS0 * BATCH
RP = 512
RR = 512
RD = 256
TM = 256
N_SLOTS = 2 * TOKENS + N_EXPERTS * TM
N_MTILES = N_SLOTS // TM
Y_ROWS = 2 * TOKENS + N_SLOTS
TE_LANES = 256

VMEM_LIMIT = 52 * 1024 * 1024
NT_DIMS = (((1,), (1,)), ((), ()))
LOG2E = 1.0 / math.log(2.0)
EXIT_LOG_WEIGHT = -110.0
BOUND_SLACK = 1.01


def _sigmoid(x):
    return 1.0 / (1.0 + jnp.exp(-x))


def _layer_norm(x, g, b):
    mu = jnp.mean(x, axis=-1, keepdims=True)
    xc = x - mu
    var = jnp.mean(xc * xc, axis=-1, keepdims=True)
    return xc * lax.rsqrt(var + LN_EPS) * g + b


def _s5_prep_kernel(lre_ref, lim_ref, ldt_ref, bre_ref, bim_ref,
                    are_ref, aim_ref, bcat_ref):
    lam_re = lre_ref[...]
    lam_im = lim_ref[...]
    dt = jnp.exp(ldt_ref[...])
    mag = jnp.exp(lam_re * dt)
    ang = lam_im * dt
    lb_re = mag * jnp.cos(ang)
    lb_im = mag * jnp.sin(ang)
    are_ref[...] = lb_re
    aim_ref[...] = lb_im
    den = lam_re * lam_re + lam_im * lam_im
    num_re = lb_re - 1.0
    f_re = (num_re * lam_re + lb_im * lam_im) / den
    f_im = (lb_im * lam_re - num_re * lam_im) / den
    for c in range(S5_CHUNKS):
        fr = f_re[:, c * 512:(c + 1) * 512]
        fi = f_im[:, c * 512:(c + 1) * 512]
        br = bre_ref[c]
        bi = bim_ref[c]
        bcat_ref[c, :, 0:512] = (fr * br - fi * bi).astype(BF16)
        bcat_ref[c, :, 512:1024] = (fr * bi + fi * br).astype(BF16)


def _s5_prep(lam_re, lam_im, log_dt, b_re, b_im):
    def blockdiag(b):
        b4 = b.reshape(S5_CHUNKS, 8, S5_P, S5_H).transpose(0, 1, 3, 2)
        eye = jnp.eye(8, dtype=F32)
        full = b4[:, :, :, None, :] * eye[None, :, None, :, None]
        return full.reshape(S5_CHUNKS, 8 * S5_H, 8 * S5_P)

    return pl.pallas_call(
        _s5_prep_kernel,
        out_shape=(jax.ShapeDtypeStruct((1, S5_STATE), F32),
                   jax.ShapeDtypeStruct((1, S5_STATE), F32),
                   jax.ShapeDtypeStruct((S5_CHUNKS, 128, 1024), BF16)),
        name="s5_prep",
    )(lam_re.reshape(1, S5_STATE), lam_im.reshape(1, S5_STATE),
      jnp.repeat(log_dt, S5_P).reshape(1, S5_STATE), blockdiag(b_re), blockdiag(b_im))


def _mixer0_kernel(x_ref, win_ref, bcat_ref, are_ref, aim_ref, ccat_ref, d_ref,
                   wglu_ref, poolw_ref, pscale_ref, o_ref, bus, hst, pe):
    i = pl.program_id(0)

    @pl.when(i == 0)
    def _():
        hst[...] = jnp.zeros_like(hst)
        pe[0:POOL_HALO, :] = jnp.zeros((POOL_HALO, D_A), F32)

    h = jnp.dot(x_ref[...].astype(BF16), win_ref[...], preferred_element_type=F32)
    u = h[:, :D_A]
    v = h[:, D_A:]
    ub = u.astype(BF16)

    for c in range(S5_CHUNKS):
        bus[:, c * 1024:(c + 1) * 1024] = jnp.dot(
            ub[:, c * 128:(c + 1) * 128], bcat_ref[c], preferred_element_type=F32)

    for c in range(S5_CHUNKS):
        re_cols = slice(c * 1024, c * 1024 + 512)
        im_cols = slice(c * 1024 + 512, (c + 1) * 1024)
        ar = jnp.broadcast_to(are_ref[:, c * 512:(c + 1) * 512], (BATCH, 512))
        ai = jnp.broadcast_to(aim_ref[:, c * 512:(c + 1) * 512], (BATCH, 512))

        def step(t, carry, re_cols=re_cols, im_cols=im_cols, ar=ar, ai=ai):
            hr, hi = carry
            rows = pl.ds(pl.multiple_of(t * BATCH, BATCH), BATCH)
            nr = ar * hr - ai * hi + bus[rows, re_cols]
            ni = ar * hi + ai * hr + bus[rows, im_cols]
            bus[rows, re_cols] = nr
            bus[rows, im_cols] = ni
            return nr, ni

        hr, hi = lax.fori_loop(0, TS0, step, (hst[:, re_cols], hst[:, im_cols]), unroll=8)
        hst[:, re_cols] = hr
        hst[:, im_cols] = hi

    ys = [jnp.dot(bus[:, c * 1024:(c + 1) * 1024].astype(BF16), ccat_ref[c],
                  preferred_element_type=F32) for c in range(S5_CHUNKS)]
    y = jnp.concatenate(ys, axis=1) + d_ref[...] * u
    y = 0.5 * y * (1.0 + jnp.tanh(math.sqrt(2.0 / math.pi) * (y + 0.044715 * (y * y * y))))
    ga = y * _sigmoid(jnp.dot(y.astype(BF16), wglu_ref[...], preferred_element_type=F32))
    o_ref[:, 0:D_A] = ga.astype(BF16)

    pe[POOL_HALO:, :] = v
    t_glob = lax.shift_right_logical(
        lax.broadcasted_iota(I32, (R0, 1), 0), int(math.log2(BATCH))) + i * TS0
    for gi, w in enumerate(POOL_WINDOWS):
        cols = slice(gi * POOL_C, (gi + 1) * POOL_C)
        s = pe[:, cols]
        off = BATCH
        while off < BATCH * w:
            s = s[off:] + s[:-off]
            off *= 2
        s = s[POOL_HALO - BATCH * (w - 1):]
        cnt = jnp.minimum(t_glob + 1, w).astype(F32)
        pooled = s / cnt - v[:, cols]
        mixed = jnp.dot(pooled.astype(BF16), poolw_ref[gi], preferred_element_type=F32)
        o_ref[:, D_A + gi * POOL_C:D_A + (gi + 1) * POOL_C] = (
            mixed * pscale_ref[:, cols]).astype(BF16)
    pe[0:POOL_HALO, :] = pe[R0:R0 + POOL_HALO, :]


def _mixer0(x_tm, win, bcat, a_re, a_im, ccat, dskip, wglu, poolw, pscale):
    full = lambda shape: pl.BlockSpec(shape, lambda i: (0,) * len(shape))
    return pl.pallas_call(
        _mixer0_kernel,
        out_shape=jax.ShapeDtypeStruct((TOKENS, D_MODEL), BF16),
        grid=(SEQ // TS0,),
        in_specs=[
            pl.BlockSpec((R0, D_MODEL), lambda i: (i, 0)),
            full((D_MODEL, D_MODEL)),
            full((S5_CHUNKS, 128, 1024)),
            full((1, S5_STATE)),
            full((1, S5_STATE)),
            full((S5_CHUNKS, 1024, 128)),
            full((1, D_A)),
            full((D_A, D_A)),
            full((4, POOL_C, POOL_C)),
            full((1, D_A)),
        ],
        out_specs=pl.BlockSpec((R0, D_MODEL), lambda i: (i, 0)),
        scratch_shapes=[
            pltpu.VMEM((R0, 2 * S5_STATE), F32),
            pltpu.VMEM((BATCH, 2 * S5_STATE), F32),
            pltpu.VMEM((POOL_HALO + R0, D_A), F32),
        ],
        compiler_params=pltpu.CompilerParams(
            dimension_semantics=("arbitrary",), vmem_limit_bytes=VMEM_LIMIT),
        name="mixer0",
    )(x_tm, win, bcat, a_re, a_im, ccat, dskip, wglu, poolw, pscale)


def _proj_ln_route_kernel(a_ref, w_ref, x_ref, g_ref, b_ref, rwh_ref, rwl_ref, rb_ref,
                          x1_ref, ri_ref, wcol_ref):
    mix = jnp.dot(a_ref[...], w_ref[...], preferred_element_type=F32)
    x1 = _layer_norm(ALPHA * x_ref[...] + mix, g_ref[...], b_ref[...])
    x1_ref[...] = x1

    xh = x1.astype(BF16)
    xl = (x1 - xh.astype(F32)).astype(BF16)
    rwh = rwh_ref[...]
    logits = (lax.dot_general(rwh, xh, NT_DIMS, preferred_element_type=F32)
              + lax.dot_general(rwh, xl, NT_DIMS, preferred_element_type=F32)
              + lax.dot_general(rwl_ref[...], xh, NT_DIMS, preferred_element_type=F32))
    scores = _sigmoid(logits)
    sel = scores + rb_ref[...]
    row = lambda a, e: a[e:e + 1, :]

    best = None
    grp = None
    for g in range(N_EXPERTS // EXPERTS_PER_GROUP):
        m = [row(sel, EXPERTS_PER_GROUP * g + k) for k in range(EXPERTS_PER_GROUP)]
        gs = None
        for a in range(EXPERTS_PER_GROUP):
            for b in range(a + 1, EXPERTS_PER_GROUP):
                pair = m[a] + m[b]
                gs = pair if gs is None else jnp.maximum(gs, pair)
        if best is None:
            best, grp = gs, jnp.zeros(gs.shape, I32)
        else:
            better = gs > best
            grp = jnp.where(better, g, grp)
            best = jnp.where(better, gs, best)

    def pick(a, k):
        out = row(a, k)
        for g in range(1, N_EXPERTS // EXPERTS_PER_GROUP):
            out = jnp.where(grp == g, row(a, EXPERTS_PER_GROUP * g + k), out)
        return out

    cs = [pick(sel, k) for k in range(EXPERTS_PER_GROUP)]
    ss = [pick(scores, k) for k in range(EXPERTS_PER_GROUP)]
    m1, i1, s1 = cs[0], jnp.zeros(cs[0].shape, I32), ss[0]
    for k in range(1, EXPERTS_PER_GROUP):
        better = cs[k] > m1
        i1 = jnp.where(better, k, i1)
        s1 = jnp.where(better, ss[k], s1)
        m1 = jnp.where(better, cs[k], m1)
    m2 = jnp.full(m1.shape, -jnp.inf, F32)
    i2 = jnp.zeros(m1.shape, I32)
    s2 = jnp.zeros(m1.shape, F32)
    for k in range(EXPERTS_PER_GROUP):
        better = jnp.logical_and(i1 != k, cs[k] > m2)
        i2 = jnp.where(better, k, i2)
        s2 = jnp.where(better, ss[k], s2)
        m2 = jnp.where(better, cs[k], m2)
    e0 = grp * EXPERTS_PER_GROUP + i1
    e1 = grp * EXPERTS_PER_GROUP + i2
    tot = s1 + s2
    w0 = s1 / tot
    w1 = s2 / tot
    ri_ref[...] = jnp.concatenate([e0, e1, jnp.zeros((6, RP), I32)], axis=0)
    wslab = jnp.concatenate([w0, w1, jnp.zeros((126, RP), F32)], axis=0)
    wcol_ref[...] = wslab.T


def _proj_ln_route(a, w, xres, g, b, rwh, rwl, rb):
    full = lambda shape: pl.BlockSpec(shape, lambda i: (0,) * len(shape))
    return pl.pallas_call(
        _proj_ln_route_kernel,
        out_shape=(jax.ShapeDtypeStruct((TOKENS, D_MODEL), F32),
                   jax.ShapeDtypeStruct((8, TOKENS), I32),
                   jax.ShapeDtypeStruct((TOKENS, 128), F32)),
        grid=(TOKENS // RP,),
        in_specs=[
            pl.BlockSpec((RP, D_MODEL), lambda i: (i, 0)),
            full((D_MODEL, D_MODEL)),
            pl.BlockSpec((RP, D_MODEL), lambda i: (i, 0)),
            full((1, D_MODEL)),
            full((1, D_MODEL)),
            full((N_EXPERTS, D_MODEL)),
            full((N_EXPERTS, D_MODEL)),
            full((N_EXPERTS, 1)),
        ],
        out_specs=(pl.BlockSpec((RP, D_MODEL), lambda i: (i, 0)),
                   pl.BlockSpec((8, RP), lambda i: (0, i)),
                   pl.BlockSpec((RP, 128), lambda i: (i, 0))),
        compiler_params=pltpu.CompilerParams(
            dimension_semantics=("arbitrary",), vmem_limit_bytes=VMEM_LIMIT),
        name="proj_ln_route",
    )(a, w, xres, g, b, rwh, rwl, rb)


def _rank_kernel(ri_ref, pos_ref, te_ref, seg_ref, cnt, offs, tot):
    ph = pl.program_id(0)
    i = pl.program_id(1)
    n_tiles = pl.num_programs(1)
    e_iota = lax.broadcasted_iota(I32, (N_EXPERTS, RR), 0)
    m0 = e_iota == ri_ref[0:1, :]
    m1 = e_iota == ri_ref[1:2, :]
    mask = jnp.where(m0, 1.0, 0.0) + jnp.where(m1, 1.0, 0.0)
    tile_cnt = jnp.sum(mask, axis=1, keepdims=True)

    @pl.when(jnp.logical_and(ph == 0, i == 0))
    def _():
        cnt[...] = jnp.zeros_like(cnt)

    @pl.when(ph == 0)
    def _():
        cnt[...] = cnt[...] + tile_cnt
        pos_ref[...] = jnp.zeros_like(pos_ref)

    @pl.when(jnp.logical_and(ph == 0, i == n_tiles - 1))
    def _():
        c = cnt[...]
        padded = jnp.ceil(c * (1.0 / TM)) * TM
        sub = lax.broadcasted_iota(I32, (N_EXPERTS, 128), 0)
        acc = jnp.zeros((N_EXPERTS, 128), F32)
        for e in range(N_EXPERTS):
            acc = acc + jnp.where(sub > e, padded[e:e + 1, :], 0.0)
        offs[...] = acc
        tot[...] = acc + padded
        cnt[...] = jnp.zeros_like(cnt)

    @pl.when(ph == 1)
    def _():
        r = lax.broadcasted_iota(I32, (RR, RR), 0)
        cidx = lax.broadcasted_iota(I32, (RR, RR), 1)
        tri = jnp.where(r < cidx, 1.0, 0.0).astype(BF16)
        prefix = jnp.dot(mask.astype(BF16), tri, preferred_element_type=F32)
        slot = prefix + cnt[:, 0:1] + offs[:, 0:1]
        p0 = jnp.sum(jnp.where(m0, slot, 0.0), axis=0, keepdims=True)
        p1 = jnp.sum(jnp.where(m1, slot, 0.0), axis=0, keepdims=True)
        pos_ref[...] = jnp.concatenate(
            [p0.astype(I32), p1.astype(I32), jnp.zeros((6, RR), I32)], axis=0)
        cnt[...] = cnt[...] + tile_cnt

    @pl.when(jnp.logical_and(ph == 1, i == n_tiles - 1))
    def _():
        ends = tot[:, 0:1]
        start = (lax.broadcasted_iota(I32, (N_EXPERTS, TE_LANES), 1) * TM).astype(F32)
        te = jnp.sum(jnp.where(start >= ends, 1.0, 0.0), axis=0, keepdims=True)
        te = jnp.minimum(te, N_EXPERTS - 1.0).astype(I32)
        n_used = (tot[N_EXPERTS - 1:N_EXPERTS, :] * (1.0 / TM)).astype(I32)
        n_used = jnp.concatenate([n_used, n_used], axis=1)
        te_ref[...] = jnp.concatenate(
            [te, n_used, jnp.zeros((6, TE_LANES), I32)], axis=0)
        seg_ref[0:N_EXPERTS, :] = (offs[...] + cnt[...]).astype(I32)
        seg_ref[N_EXPERTS:2 * N_EXPERTS, :] = tot[...].astype(I32)


def _rank(ri):
    return pl.pallas_call(
        _rank_kernel,
        out_shape=(jax.ShapeDtypeStruct((8, TOKENS), I32),
                   jax.ShapeDtypeStruct((8, TE_LANES), I32),
                   jax.ShapeDtypeStruct((2 * N_EXPERTS, 128), I32)),
        grid=(2, TOKENS // RR),
        in_specs=[pl.BlockSpec((8, RR), lambda p, i: (0, i))],
        out_specs=(pl.BlockSpec((8, RR), lambda p, i: (0, i * p)),
                   pl.BlockSpec((8, TE_LANES), lambda p, i: (0, 0)),
                   pl.BlockSpec((2 * N_EXPERTS, 128), lambda p, i: (0, 0))),
        scratch_shapes=[pltpu.VMEM((N_EXPERTS, 128), F32),
                        pltpu.VMEM((N_EXPERTS, 128), F32),
                        pltpu.VMEM((N_EXPERTS, 128), F32)],
        compiler_params=pltpu.CompilerParams(
            dimension_semantics=("arbitrary", "arbitrary")),
        name="moe_rank",
    )(ri)


def _expert_kernel(te_ref, nu_ref, pos_ref, seg_ref, x_hbm, w1_ref, w3_ref, w2_ref, y_hbm,
                   w1b, w3b, w2b, xbuf, ybuf, dst, gsem, ssem):
    i = pl.program_id(0)
    n_used = nu_ref[0]
    cur = lax.rem(i, 2)
    oth = 1 - cur

    def gather_row(slot, buf, r):
        tok = dst[slot] & (TOKENS - 1)
        return pltpu.make_async_copy(x_hbm.at[pl.ds(tok, 1)], xbuf.at[buf, pl.ds(r, 1)], gsem)

    def scatter_row(slot, buf, r):
        return pltpu.make_async_copy(ybuf.at[buf, pl.ds(r, 1)], y_hbm.at[pl.ds(dst[slot], 1)],
                                     ssem)

    def start_rows(row_copy, tile, buf):
        base = tile * TM
        for r in range(TM):
            row_copy(base + r, buf, r).start()

    def wait_rows(row_copy, buf):
        for r in range(TM):
            row_copy(0, buf, 0).wait()

    def compute(buf):
        xb = xbuf[buf].astype(BF16)
        h1 = jnp.dot(xb, w1b[...], preferred_element_type=F32)
        h3 = jnp.dot(xb, w3b[...], preferred_element_type=F32)
        h = (h1 * _sigmoid(h1)) * h3
        ybuf[buf] = jnp.dot(h.astype(BF16), w2b[...], preferred_element_type=F32)

    @pl.when(i == 0)
    def _():
        for e in range(N_EXPERTS):
            def pad_body(s, carry):
                dst[s] = 2 * TOKENS + s
                return carry
            lax.fori_loop(seg_ref[e], seg_ref[N_EXPERTS + e], pad_body, 0)

        def inv_body(t, carry):
            dst[pos_ref[t]] = t
            dst[pos_ref[TOKENS + t]] = TOKENS + t
            return carry
        lax.fori_loop(0, TOKENS, inv_body, 0, unroll=8)

        def first_start(r, carry):
            gather_row(r, 0, r).start()
            return carry
        lax.fori_loop(0, TM, first_start, 0, unroll=8)

        def first_wait(r, carry):
            gather_row(0, 0, 0).wait()
            return carry
        lax.fori_loop(0, TM, first_wait, 0, unroll=8)

    e = te_ref[i]
    prev = te_ref[jnp.maximum(i - 1, 0)]

    @pl.when(jnp.logical_and(i < n_used, jnp.logical_or(i == 0, e != prev)))
    def _():
        w1b[...] = w1_ref[...].astype(BF16)
        w3b[...] = w3_ref[...].astype(BF16)
        w2b[...] = w2_ref[...].astype(BF16)

    nxt = jnp.minimum(i + 1, n_used - 1)

    @pl.when(i == 0)
    def _():
        start_rows(gather_row, nxt, 1)
        compute(0)
        wait_rows(gather_row, 1)

    @pl.when(jnp.logical_and(i >= 1, i < n_used))
    def _():
        start_rows(gather_row, nxt, oth)
        start_rows(scatter_row, i - 1, oth)
        compute(cur)
        wait_rows(gather_row, oth)
        wait_rows(scatter_row, oth)

    @pl.when(i == n_used)
    def _():
        def last_start(r, carry):
            scatter_row((i - 1) * TM + r, oth, r).start()
            return carry
        lax.fori_loop(0, TM, last_start, 0, unroll=8)

        def last_wait(r, carry):
            scatter_row(0, oth, 0).wait()
            return carry
        lax.fori_loop(0, TM, last_wait, 0, unroll=8)


def _experts(te, nu, pos_flat, seg, x1, w1, w3, w2, layer):
    wspec = lambda r, c: pl.BlockSpec((None, None, r, c),
                                      lambda i, te, nu, pos, seg: (layer, te[i], 0, 0))
    return pl.pallas_call(
        _expert_kernel,
        out_shape=jax.ShapeDtypeStruct((Y_ROWS, D_MODEL), F32),
        grid_spec=pltpu.PrefetchScalarGridSpec(
            num_scalar_prefetch=4,
            grid=(N_MTILES + 1,),
            in_specs=[pl.BlockSpec(memory_space=pl.ANY),
                      wspec(D_MODEL, D_EXPERT), wspec(D_MODEL, D_EXPERT),
                      wspec(D_EXPERT, D_MODEL)],
            out_specs=pl.BlockSpec(memory_space=pl.ANY),
            scratch_shapes=[pltpu.VMEM((D_MODEL, D_EXPERT), BF16),
                            pltpu.VMEM((D_MODEL, D_EXPERT), BF16),
                            pltpu.VMEM((D_EXPERT, D_MODEL), BF16),
                            pltpu.VMEM((2, TM, D_MODEL), F32),
                            pltpu.VMEM((2, TM, D_MODEL), F32),
                            pltpu.SMEM((N_SLOTS,), I32),
                            pltpu.SemaphoreType.DMA,
                            pltpu.SemaphoreType.DMA],
        ),
        compiler_params=pltpu.CompilerParams(
            dimension_semantics=("arbitrary",), vmem_limit_bytes=VMEM_LIMIT,
            has_side_effects=True),
        name="moe_experts",
    )(te, nu, pos_flat, seg, x1, w1, w3, w2)


def _combine_kernel(x1_ref, p_ref, wcol_ref, y0_ref, y1_ref, wp_ref, wg_ref, g_ref, b_ref,
                    o_ref):
    x1 = x1_ref[...]
    gate = _sigmoid(jnp.dot(x1.astype(BF16), wg_ref[...], preferred_element_type=F32))
    ple = jnp.dot(p_ref[...].astype(BF16), wp_ref[...], preferred_element_type=F32) * gate
    wc = wcol_ref[...]
    ffn = wc[:, 0:1] * y0_ref[...] + wc[:, 1:2] * y1_ref[...]
    o_ref[...] = _layer_norm(ALPHA * x1 + ffn + ple, g_ref[...], b_ref[...])


def _combine(x1, p, wcol, ys, wp, wg, g, b):
    full = lambda shape: pl.BlockSpec(shape, lambda i: (0,) * len(shape))
    return pl.pallas_call(
        _combine_kernel,
        out_shape=jax.ShapeDtypeStruct((TOKENS, D_MODEL), F32),
        grid=(TOKENS // RD,),
        in_specs=[pl.BlockSpec((RD, D_MODEL), lambda i: (i, 0)),
                  pl.BlockSpec((RD, PLE_DIM), lambda i: (i, 0)),
                  pl.BlockSpec((RD, 128), lambda i: (i, 0)),
                  pl.BlockSpec((RD, D_MODEL), lambda i: (i, 0)),
                  pl.BlockSpec((RD, D_MODEL), lambda i: (TOKENS // RD + i, 0)),
                  full((PLE_DIM, D_MODEL)),
                  full((D_MODEL, D_MODEL)),
                  full((1, D_MODEL)),
                  full((1, D_MODEL))],
        out_specs=pl.BlockSpec((RD, D_MODEL), lambda i: (i, 0)),
        compiler_params=pltpu.CompilerParams(
            dimension_semantics=("arbitrary",), vmem_limit_bytes=VMEM_LIMIT),
        name="moe_combine",
    )(x1, p, wcol, ys, ys, wp, wg, g, b)


def _qkv_kernel(x_ref, w_ref, o_ref):
    o_ref[...] = jnp.dot(x_ref[...].astype(BF16), w_ref[...],
                         preferred_element_type=F32).astype(BF16)


def _qkv_proj(x, w):
    return pl.pallas_call(
        _qkv_kernel,
        out_shape=jax.ShapeDtypeStruct((TOKENS, 3 * D_MODEL), BF16),
        grid=(TOKENS // RP, 3),
        in_specs=[pl.BlockSpec((RP, D_MODEL), lambda i, j: (i, 0)),
                  pl.BlockSpec((D_MODEL, D_MODEL), lambda i, j: (0, j))],
        out_specs=pl.BlockSpec((RP, D_MODEL), lambda i, j: (i, j)),
        compiler_params=pltpu.CompilerParams(
            dimension_semantics=("arbitrary", "arbitrary"), vmem_limit_bytes=VMEM_LIMIT),
        name="qkv_proj",
    )(x, w)


def _attn_kernel(q_ref, k_ref, v_ref, o_ref, acc, cbuf, bnd, kinf):
    qb = pl.program_id(2)
    lane = lax.broadcasted_iota(I32, (1, 128), 1)
    half = (lane < 64, lane >= 64)

    @pl.when(qb == 0)
    def _():
        for pr in range(ATT_PAIRS):
            ka = jnp.max(jnp.abs(k_ref[:, pr * 128:(pr + 1) * 128].astype(F32)),
                         axis=0, keepdims=True)
            for hh in range(2):
                m = jnp.max(jnp.where(half[hh], ka, 0.0), axis=1, keepdims=True)
                kinf[2 * pr + hh] = jnp.broadcast_to(m, (8, 128))

    qh = []
    for pr in range(ATT_PAIRS):
        q = q_ref[:, pr * 128:(pr + 1) * 128] * jnp.asarray(0.125, BF16)
        zero = jnp.zeros_like(q)
        qh.append((jnp.where(half[0], q, zero), jnp.where(half[1], q, zero)))
        qa = jnp.abs(q.astype(F32))
        for hh in range(2):
            qn = jnp.sum(jnp.where(half[hh], qa, 0.0), axis=1, keepdims=True)
            bnd[2 * pr + hh] = (jnp.broadcast_to(qn, (SB_BLOCK, 128))
                                * kinf[2 * pr + hh][0:1, :] * BOUND_SLACK)
    r = lax.broadcasted_iota(I32, (SB_BLOCK, SB_BLOCK), 0)
    cidx = lax.broadcasted_iota(I32, (SB_BLOCK, SB_BLOCK), 1)
    neg_suffix = jnp.where(r >= cidx, -1.0, 0.0).astype(BF16)
    causal = cidx < r

    acc[...] = jnp.zeros_like(acc)
    cbuf[...] = jnp.zeros_like(cbuf)

    def block(kb, masked):
        rows = pl.ds(pl.multiple_of(kb * SB_BLOCK, SB_BLOCK), SB_BLOCK)
        heads = [(pr, hh) for pr in range(ATT_PAIRS) for hh in range(2)]
        ks = [k_ref[rows, pr * 128:(pr + 1) * 128] for pr in range(ATT_PAIRS)]
        vs = [v_ref[rows, pr * 128:(pr + 1) * 128] for pr in range(ATT_PAIRS)]
        zs = [lax.dot_general(qh[pr][hh], ks[pr], NT_DIMS, preferred_element_type=F32)
              for pr, hh in heads]
        sps = []
        for z in zs:
            sp = jnp.maximum(z, 0.0) + jnp.log(1.0 + jnp.exp2(jnp.abs(z) * (-LOG2E)))
            if masked:
                sp = jnp.where(causal, sp, 0.0)
            sps.append(sp.astype(BF16))
        rss = []
        for idx, sp in enumerate(sps):
            c = cbuf[idx]
            rss.append(jnp.dot(sp, neg_suffix, preferred_element_type=F32)
                       + jnp.concatenate([c, c], axis=1))
        for idx, (pr, hh) in enumerate(heads):
            w = jnp.exp(zs[idx] + rss[idx])
            if masked:
                w = jnp.where(causal, w, 0.0)
            acc[idx] = acc[idx] + jnp.dot(w.astype(BF16), vs[pr], preferred_element_type=F32)
            cbuf[idx] = jnp.broadcast_to(rss[idx][:, 0:1], (SB_BLOCK, 128))

    def log_weight_bound():
        m = cbuf[0] + bnd[0]
        for idx in range(1, 2 * ATT_PAIRS):
            m = jnp.maximum(m, cbuf[idx] + bnd[idx])
        return jnp.max(m)

    block(qb, True)

    def cond(carry):
        j, m = carry
        return jnp.logical_and(j < qb, m > EXIT_LOG_WEIGHT)

    def body(carry):
        j, _ = carry
        block(qb - 1 - j, False)
        return j + 1, log_weight_bound()

    lax.while_loop(cond, body, (jnp.int32(0), log_weight_bound()))
    for pr in range(ATT_PAIRS):
        o_ref[:, pr * 128:(pr + 1) * 128] = jnp.where(
            lane < 64, acc[2 * pr], acc[2 * pr + 1]).astype(BF16)


def _attention(qkv):
    n_hg = D_MODEL // ATT_LANES
    return pl.pallas_call(
        _attn_kernel,
        out_shape=jax.ShapeDtypeStruct((BATCH, SEQ, D_MODEL), BF16),
        grid=(BATCH, n_hg, SEQ // SB_BLOCK),
        in_specs=[pl.BlockSpec((None, SB_BLOCK, ATT_LANES), lambda b, h, i: (b, i, h)),
                  pl.BlockSpec((None, SEQ, ATT_LANES), lambda b, h, i: (b, 0, n_hg + h)),
                  pl.BlockSpec((None, SEQ, ATT_LANES), lambda b, h, i: (b, 0, 2 * n_hg + h))],
        out_specs=pl.BlockSpec((None, SB_BLOCK, ATT_LANES), lambda b, h, i: (b, i, h)),
        scratch_shapes=[pltpu.VMEM((2 * ATT_PAIRS, SB_BLOCK, 128), F32),
                        pltpu.VMEM((2 * ATT_PAIRS, SB_BLOCK, 128), F32),
                        pltpu.VMEM((2 * ATT_PAIRS, SB_BLOCK, 128), F32),
                        pltpu.VMEM((2 * ATT_PAIRS, 8, 128), F32)],
        compiler_params=pltpu.CompilerParams(
            dimension_semantics=("arbitrary", "arbitrary", "arbitrary"),
            vmem_limit_bytes=VMEM_LIMIT),
        name="sb_attention",
    )(qkv, qkv, qkv)


def _moe_ffn(x1, ri, wcol, p_rows, layer, moe_w1, moe_w3, moe_w2, wp, wg, g, b):
    pos, te, seg = _rank(ri)
    pos_flat = pos[0:2].reshape(2 * TOKENS)
    ys = _experts(te[0, :N_MTILES + 1], te[1, :1], pos_flat, seg[:, 0], x1,
                  moe_w1, moe_w3, moe_w2, layer)
    return _combine(x1, p_rows, wcol, ys, wp, wg, g, b)


def kernel(x, p, ab_w_in, s5_lambda_re, s5_lambda_im, s5_log_dt, s5_b_re, s5_b_im, s5_c_re, s5_c_im, s5_d, s5_w_glu, pool_w, pool_scale, ab_w_out, sb_w_qkv, sb_w_out, ln_mix_g, ln_mix_b, ln_ffn_g, ln_ffn_b, router_w, router_bias, moe_w1, moe_w3, moe_w2, ple_w_proj, ple_w_gate):
    row = lambda a: a.reshape(1, -1)
    rwt = router_w.T
    rwh = rwt.astype(BF16)
    rwl = (rwt - rwh.astype(F32)).astype(BF16)
    rb = router_bias.reshape(N_EXPERTS, 1)

    x_tm = x.transpose(1, 0, 2).reshape(TOKENS, D_MODEL)
    p_tm = p[0].transpose(1, 0, 2).reshape(TOKENS, PLE_DIM)
    a_re, a_im, bcat = _s5_prep(s5_lambda_re[0], s5_lambda_im[0], s5_log_dt[0],
                                s5_b_re[0], s5_b_im[0])

    def c_blockdiag(c):
        c4 = c.reshape(S5_CHUNKS, 8, S5_H, S5_P).transpose(0, 1, 3, 2)
        eye = jnp.eye(8, dtype=F32)
        full = c4[:, :, :, None, :] * eye[None, :, None, :, None]
        return full.reshape(S5_CHUNKS, 8 * S5_P, 8 * S5_H)

    ccat = jnp.concatenate([c_blockdiag(s5_c_re[0]), -c_blockdiag(s5_c_im[0])],
                           axis=1).astype(BF16)
    mix_in = _mixer0(x_tm, ab_w_in[0].astype(BF16), bcat, a_re, a_im, ccat,
                     row(s5_d[0]), s5_w_glu[0].astype(BF16), pool_w[0].astype(BF16),
                     row(pool_scale[0]))
    x1, ri, wcol = _proj_ln_route(mix_in, ab_w_out[0].astype(BF16), x_tm,
                                  row(ln_mix_g[0]), row(ln_mix_b[0]), rwh, rwl, rb)
    x2 = _moe_ffn(x1, ri, wcol, p_tm, 0, moe_w1, moe_w3, moe_w2,
                  ple_w_proj[0].astype(BF16), ple_w_gate[0].astype(BF16),
                  row(ln_ffn_g[0]), row(ln_ffn_b[0]))

    xb = x2.reshape(SEQ, BATCH, D_MODEL).transpose(1, 0, 2).reshape(TOKENS, D_MODEL)
    qkv = _qkv_proj(xb, sb_w_qkv[0].astype(BF16))
    att = _attention(qkv.reshape(BATCH, SEQ, 3 * D_MODEL)).reshape(TOKENS, D_MODEL)
    x3, ri, wcol = _proj_ln_route(att, sb_w_out[0].astype(BF16), xb,
                                  row(ln_mix_g[1]), row(ln_mix_b[1]), rwh, rwl, rb)
    x4 = _moe_ffn(x3, ri, wcol, p[1].reshape(TOKENS, PLE_DIM), 1, moe_w1, moe_w3, moe_w2,
                  ple_w_proj[1].astype(BF16), ple_w_gate[1].astype(BF16),
                  row(ln_ffn_g[1]), row(ln_ffn_b[1]))
    return x4.reshape(BATCH, SEQ, D_MODEL)
```

```python
import functools
import math

import jax
import jax.numpy as jnp
from jax import lax
from jax.experimental import pallas as pl
from jax.experimental.pallas import tpu as pltpu

F32 = jnp.float32
BF16 = jnp.bfloat16
I32 = jnp.int32

D_MODEL = 1024
BATCH = 8
SEQ = 2048
DEPTH = 2
TOKENS = BATCH * SEQ

D_A = 512
S5_H = 16
S5_G = 32
S5_P = 64
S5_STATE = S5_G * S5_P
S5_CHUNKS = 4
POOL_WINDOWS = (2, 4, 8, 16)
POOL_C = 128
POOL_HALO = 16 * BATCH
SB_BLOCK = 256
ATT_PAIRS = 4
ATT_LANES = 128 * ATT_PAIRS
N_EXPERTS = 16
EXPERTS_PER_GROUP = 4
D_EXPERT = 512
PLE_DIM = 256
ALPHA = (2 * DEPTH) ** 0.25
LN_EPS = 1e-5

TS0 = 64
R0 = TS0 * BATCH
RP = 512
PROJ_SPLIT = 2
RR = 512
RD = 256
TM = 256
N_SLOTS = 2 * TOKENS + N_EXPERTS * TM
N_MTILES = N_SLOTS // TM
Y_ROWS = 2 * TOKENS + N_EXPERTS * TM
TE_LANES = 256

VMEM_LIMIT = 52 * 1024 * 1024
NT_DIMS = (((1,), (1,)), ((), ()))
LOG2E = 1.0 / math.log(2.0)
EXIT_LOG_WEIGHT = -110.0
BOUND_SLACK = 1.01


def _sigmoid(x):
    return 1.0 / (1.0 + jnp.exp(-x))


def _layer_norm(x, g, b):
    mu = jnp.mean(x, axis=-1, keepdims=True)
    xc = x - mu
    var = jnp.mean(xc * xc, axis=-1, keepdims=True)
    return xc * lax.rsqrt(var + LN_EPS) * g + b


def _s5_prep_kernel(lre_ref, lim_ref, ldt_ref, bre_ref, bim_ref,
                    are_ref, aim_ref, bcat_ref):
    lam_re = lre_ref[...]
    lam_im = lim_ref[...]
    dt = jnp.exp(ldt_ref[...])
    mag = jnp.exp(lam_re * dt)
    ang = lam_im * dt
    lb_re = mag * jnp.cos(ang)
    lb_im = mag * jnp.sin(ang)
    are_ref[...] = lb_re
    aim_ref[...] = lb_im
    den = lam_re * lam_re + lam_im * lam_im
    num_re = lb_re - 1.0
    f_re = (num_re * lam_re + lb_im * lam_im) / den
    f_im = (lb_im * lam_re - num_re * lam_im) / den
    for c in range(S5_CHUNKS):
        fr = f_re[:, c * 512:(c + 1) * 512]
        fi = f_im[:, c * 512:(c + 1) * 512]
        br = bre_ref[c]
        bi = bim_ref[c]
        bcat_ref[c, :, 0:512] = (fr * br - fi * bi).astype(BF16)
        bcat_ref[c, :, 512:1024] = (fr * bi + fi * br).astype(BF16)


def _s5_prep(lam_re, lam_im, log_dt, b_re, b_im):
    def blockdiag(b):
        b4 = b.reshape(S5_CHUNKS, 8, S5_P, S5_H).transpose(0, 1, 3, 2)
        eye = jnp.eye(8, dtype=F32)
        full = b4[:, :, :, None, :] * eye[None, :, None, :, None]
        return full.reshape(S5_CHUNKS, 8 * S5_H, 8 * S5_P)

    return pl.pallas_call(
        _s5_prep_kernel,
        out_shape=(jax.ShapeDtypeStruct((1, S5_STATE), F32),
                   jax.ShapeDtypeStruct((1, S5_STATE), F32),
                   jax.ShapeDtypeStruct((S5_CHUNKS, 128, 1024), BF16)),
        name="s5_prep",
    )(lam_re.reshape(1, S5_STATE), lam_im.reshape(1, S5_STATE),
      jnp.repeat(log_dt, S5_P).reshape(1, S5_STATE), blockdiag(b_re), blockdiag(b_im))


def _mixer0_kernel(x_ref, win_ref, bcat_ref, are_ref, aim_ref, ccat_ref, d_ref,
                   wglu_ref, poolw_ref, pscale_ref, o_ref, bus, hst, pe):
    i = pl.program_id(0)

    @pl.when(i == 0)
    def _():
        hst[...] = jnp.zeros_like(hst)
        pe[0:POOL_HALO, :] = jnp.zeros((POOL_HALO, D_A), F32)

    h = jnp.dot(x_ref[...].astype(BF16), win_ref[...], preferred_element_type=F32)
    u = h[:, :D_A]
    v = h[:, D_A:]
    ub = u.astype(BF16)

    for c in range(S5_CHUNKS):
        bus[:, c * 1024:(c + 1) * 1024] = jnp.dot(
            ub[:, c * 128:(c + 1) * 128], bcat_ref[c], preferred_element_type=F32)

    for c in range(S5_CHUNKS):
        re_cols = slice(c * 1024, c * 1024 + 512)
        im_cols = slice(c * 1024 + 512, (c + 1) * 1024)
        ar = jnp.broadcast_to(are_ref[:, c * 512:(c + 1) * 512], (BATCH, 512))
        ai = jnp.broadcast_to(aim_ref[:, c * 512:(c + 1) * 512], (BATCH, 512))

        def step(t, carry, re_cols=re_cols, im_cols=im_cols, ar=ar, ai=ai):
            hr, hi = carry
            rows = pl.ds(pl.multiple_of(t * BATCH, BATCH), BATCH)
            nr = ar * hr - ai * hi + bus[rows, re_cols]
            ni = ar * hi + ai * hr + bus[rows, im_cols]
            bus[rows, re_cols] = nr
            bus[rows, im_cols] = ni
            return nr, ni

        hr, hi = lax.fori_loop(0, TS0, step, (hst[:, re_cols], hst[:, im_cols]),
                               unroll=True)
        hst[:, re_cols] = hr
        hst[:, im_cols] = hi

    ys = [jnp.dot(bus[:, c * 1024:(c + 1) * 1024].astype(BF16), ccat_ref[c],
                  preferred_element_type=F32) for c in range(S5_CHUNKS)]
    y = jnp.concatenate(ys, axis=1) + d_ref[...] * u
    y = 0.5 * y * (1.0 + jnp.tanh(math.sqrt(2.0 / math.pi) * (y + 0.044715 * (y * y * y))))
    ga = y * _sigmoid(jnp.dot(y.astype(BF16), wglu_ref[...], preferred_element_type=F32))
    o_ref[:, 0:D_A] = ga.astype(BF16)

    pe[POOL_HALO:, :] = v
    t_glob = lax.shift_right_logical(
        lax.broadcasted_iota(I32, (R0, 1), 0), int(math.log2(BATCH))) + i * TS0
    for gi, w in enumerate(POOL_WINDOWS):
        cols = slice(gi * POOL_C, (gi + 1) * POOL_C)
        s = pe[:, cols]
        off = BATCH
        while off < BATCH * w:
            s = s[off:] + s[:-off]
            off *= 2
        s = s[POOL_HALO - BATCH * (w - 1):]
        cnt = jnp.minimum(t_glob + 1, w).astype(F32)
        pooled = s / cnt - v[:, cols]
        mixed = jnp.dot(pooled.astype(BF16), poolw_ref[gi], preferred_element_type=F32)
        o_ref[:, D_A + gi * POOL_C:D_A + (gi + 1) * POOL_C] = (
            mixed * pscale_ref[:, cols]).astype(BF16)
    pe[0:POOL_HALO, :] = pe[R0:R0 + POOL_HALO, :]


def _mixer0(x_tm, win, bcat, a_re, a_im, ccat, dskip, wglu, poolw, pscale):
    full = lambda shape: pl.BlockSpec(shape, lambda i: (0,) * len(shape))
    return pl.pallas_call(
        _mixer0_kernel,
        out_shape=jax.ShapeDtypeStruct((TOKENS, D_MODEL), BF16),
        grid=(SEQ // TS0,),
        in_specs=[
            pl.BlockSpec((R0, D_MODEL), lambda i: (i, 0)),
            full((D_MODEL, D_MODEL)),
            full((S5_CHUNKS, 128, 1024)),
            full((1, S5_STATE)),
            full((1, S5_STATE)),
            full((S5_CHUNKS, 1024, 128)),
            full((1, D_A)),
            full((D_A, D_A)),
            full((4, POOL_C, POOL_C)),
            full((1, D_A)),
        ],
        out_specs=pl.BlockSpec((R0, D_MODEL), lambda i: (i, 0)),
        scratch_shapes=[
            pltpu.VMEM((R0, 2 * S5_STATE), F32),
            pltpu.VMEM((BATCH, 2 * S5_STATE), F32),
            pltpu.VMEM((POOL_HALO + R0, D_A), F32),
        ],
        compiler_params=pltpu.CompilerParams(
            dimension_semantics=("arbitrary",), vmem_limit_bytes=VMEM_LIMIT),
        name="mixer0",
    )(x_tm, win, bcat, a_re, a_im, ccat, dskip, wglu, poolw, pscale)


def _route(x1, rwh_ref, rwl_ref, rb_ref):
    n = x1.shape[0]
    xh = x1.astype(BF16)
    xl = (x1 - xh.astype(F32)).astype(BF16)
    rwh = rwh_ref[...]
    logits = (lax.dot_general(rwh, xh, NT_DIMS, preferred_element_type=F32)
              + lax.dot_general(rwh, xl, NT_DIMS, preferred_element_type=F32)
              + lax.dot_general(rwl_ref[...], xh, NT_DIMS, preferred_element_type=F32))
    scores = _sigmoid(logits)
    sel = scores + rb_ref[...]
    row = lambda a, e: a[e:e + 1, :]

    best = None
    grp = None
    for g in range(N_EXPERTS // EXPERTS_PER_GROUP):
        m = [row(sel, EXPERTS_PER_GROUP * g + k) for k in range(EXPERTS_PER_GROUP)]
        gs = None
        for a in range(EXPERTS_PER_GROUP):
            for b in range(a + 1, EXPERTS_PER_GROUP):
                pair = m[a] + m[b]
                gs = pair if gs is None else jnp.maximum(gs, pair)
        if best is None:
            best, grp = gs, jnp.zeros(gs.shape, I32)
        else:
            better = gs > best
            grp = jnp.where(better, g, grp)
            best = jnp.where(better, gs, best)

    def pick(a, k):
        out = row(a, k)
        for g in range(1, N_EXPERTS // EXPERTS_PER_GROUP):
            out = jnp.where(grp == g, row(a, EXPERTS_PER_GROUP * g + k), out)
        return out

    cs = [pick(sel, k) for k in range(EXPERTS_PER_GROUP)]
    ss = [pick(scores, k) for k in range(EXPERTS_PER_GROUP)]
    m1, i1, s1 = cs[0], jnp.zeros(cs[0].shape, I32), ss[0]
    for k in range(1, EXPERTS_PER_GROUP):
        better = cs[k] > m1
        i1 = jnp.where(better, k, i1)
        s1 = jnp.where(better, ss[k], s1)
        m1 = jnp.where(better, cs[k], m1)
    m2 = jnp.full(m1.shape, -jnp.inf, F32)
    i2 = jnp.zeros(m1.shape, I32)
    s2 = jnp.zeros(m1.shape, F32)
    for k in range(EXPERTS_PER_GROUP):
        better = jnp.logical_and(i1 != k, cs[k] > m2)
        i2 = jnp.where(better, k, i2)
        s2 = jnp.where(better, ss[k], s2)
        m2 = jnp.where(better, cs[k], m2)
    e0 = grp * EXPERTS_PER_GROUP + i1
    e1 = grp * EXPERTS_PER_GROUP + i2
    tot = s1 + s2
    w0 = s1 / tot
    w1 = s2 / tot
    ri = jnp.concatenate([e0, e1, jnp.zeros((6, n), I32)], axis=0)
    wslab = jnp.concatenate([w0, w1, jnp.zeros((126, n), F32)], axis=0)
    return ri, wslab.T


def _proj_ln_route_kernel(a_ref, w_ref, x_ref, g_ref, b_ref, rwh_ref, rwl_ref, rb_ref,
                          x1_ref, ri_ref, wcol_ref):
    n = RP // PROJ_SPLIT
    parts = [slice(k * n, (k + 1) * n) for k in range(PROJ_SPLIT)]
    mixes = [jnp.dot(a_ref[rows, :], w_ref[...], preferred_element_type=F32)
             for rows in parts]
    for rows, mix in zip(parts, mixes):
        x1 = _layer_norm(ALPHA * x_ref[rows, :] + mix, g_ref[...], b_ref[...])
        x1_ref[rows, :] = x1
        ri, wcol = _route(x1, rwh_ref, rwl_ref, rb_ref)
        ri_ref[:, rows] = ri
        wcol_ref[rows, :] = wcol


def _proj_ln_route(a, w, xres, g, b, rwh, rwl, rb):
    full = lambda shape: pl.BlockSpec(shape, lambda i: (0,) * len(shape))
    return pl.pallas_call(
        _proj_ln_route_kernel,
        out_shape=(jax.ShapeDtypeStruct((TOKENS, D_MODEL), F32),
                   jax.ShapeDtypeStruct((8, TOKENS), I32),
                   jax.ShapeDtypeStruct((TOKENS, 128), F32)),
        grid=(TOKENS // RP,),
        in_specs=[
            pl.BlockSpec((RP, D_MODEL), lambda i: (i, 0)),
            full((D_MODEL, D_MODEL)),
            pl.BlockSpec((RP, D_MODEL), lambda i: (i, 0)),
            full((1, D_MODEL)),
            full((1, D_MODEL)),
            full((N_EXPERTS, D_MODEL)),
            full((N_EXPERTS, D_MODEL)),
            full((N_EXPERTS, 1)),
        ],
        out_specs=(pl.BlockSpec((RP, D_MODEL), lambda i: (i, 0)),
                   pl.BlockSpec((8, RP), lambda i: (0, i)),
                   pl.BlockSpec((RP, 128), lambda i: (i, 0))),
        compiler_params=pltpu.CompilerParams(
            dimension_semantics=("arbitrary",), vmem_limit_bytes=VMEM_LIMIT),
        name="proj_ln_route",
    )(a, w, xres, g, b, rwh, rwl, rb)


def _rank_kernel(ri_ref, pos_ref, te_ref, seg_ref, cnt, offs, tot):
    ph = pl.program_id(0)
    i = pl.program_id(1)
    n_tiles = pl.num_programs(1)
    e_iota = lax.broadcasted_iota(I32, (N_EXPERTS, RR), 0)
    m0 = e_iota == ri_ref[0:1, :]
    m1 = e_iota == ri_ref[1:2, :]
    mask = jnp.where(m0, 1.0, 0.0) + jnp.where(m1, 1.0, 0.0)
    tile_cnt = jnp.sum(mask, axis=1, keepdims=True)

    @pl.when(jnp.logical_and(ph == 0, i == 0))
    def _():
        cnt[...] = jnp.zeros_like(cnt)

    @pl.when(ph == 0)
    def _():
        cnt[...] = cnt[...] + tile_cnt
        pos_ref[...] = jnp.zeros_like(pos_ref)

    @pl.when(jnp.logical_and(ph == 0, i == n_tiles - 1))
    def _():
        c = cnt[...]
        padded = jnp.ceil(c * (1.0 / TM)) * TM
        sub = lax.broadcasted_iota(I32, (N_EXPERTS, 128), 0)
        acc = jnp.zeros((N_EXPERTS, 128), F32)
        for e in range(N_EXPERTS):
            acc = acc + jnp.where(sub > e, padded[e:e + 1, :], 0.0)
        offs[...] = acc
        tot[...] = acc + padded
        cnt[...] = jnp.zeros_like(cnt)

    @pl.when(ph == 1)
    def _():
        r = lax.broadcasted_iota(I32, (RR, RR), 0)
        cidx = lax.broadcasted_iota(I32, (RR, RR), 1)
        tri = jnp.where(r < cidx, 1.0, 0.0).astype(BF16)
        prefix = jnp.dot(mask.astype(BF16), tri, preferred_element_type=F32)
        slot = prefix + cnt[:, 0:1] + offs[:, 0:1]
        p0 = jnp.sum(jnp.where(m0, slot, 0.0), axis=0, keepdims=True)
        p1 = jnp.sum(jnp.where(m1, slot, 0.0), axis=0, keepdims=True)
        pos_ref[...] = jnp.concatenate(
            [p0.astype(I32), p1.astype(I32), jnp.zeros((6, RR), I32)], axis=0)
        cnt[...] = cnt[...] + tile_cnt

    @pl.when(jnp.logical_and(ph == 1, i == n_tiles - 1))
    def _():
        ends = tot[:, 0:1]
        start = (lax.broadcasted_iota(I32, (N_EXPERTS, TE_LANES), 1) * TM).astype(F32)
        te = jnp.sum(jnp.where(start >= ends, 1.0, 0.0), axis=0, keepdims=True)
        te = jnp.minimum(te, N_EXPERTS - 1.0).astype(I32)
        n_used = (tot[N_EXPERTS - 1:N_EXPERTS, :] * (1.0 / TM)).astype(I32)
        n_used = jnp.concatenate([n_used, n_used], axis=1)
        te_ref[...] = jnp.concatenate(
            [te, n_used, jnp.zeros((6, TE_LANES), I32)], axis=0)
        seg_ref[0:N_EXPERTS, :] = (offs[...] + cnt[...]).astype(I32)
        seg_ref[N_EXPERTS:2 * N_EXPERTS, :] = tot[...].astype(I32)


def _rank(ri):
    return pl.pallas_call(
        _rank_kernel,
        out_shape=(jax.ShapeDtypeStruct((8, TOKENS), I32),
                   jax.ShapeDtypeStruct((8, TE_LANES), I32),
                   jax.ShapeDtypeStruct((2 * N_EXPERTS, 128), I32)),
        grid=(2, TOKENS // RR),
        in_specs=[pl.BlockSpec((8, RR), lambda p, i: (0, i))],
        out_specs=(pl.BlockSpec((8, RR), lambda p, i: (0, i * p)),
                   pl.BlockSpec((8, TE_LANES), lambda p, i: (0, 0)),
                   pl.BlockSpec((2 * N_EXPERTS, 128), lambda p, i: (0, 0))),
        scratch_shapes=[pltpu.VMEM((N_EXPERTS, 128), F32),
                        pltpu.VMEM((N_EXPERTS, 128), F32),
                        pltpu.VMEM((N_EXPERTS, 128), F32)],
        compiler_params=pltpu.CompilerParams(
            dimension_semantics=("arbitrary", "arbitrary")),
        name="moe_rank",
    )(ri)


def _expert_kernel(te_ref, nu_ref, pos_ref, seg_ref, x_hbm, w1_ref, w3_ref, w2_ref, y_hbm,
                   w1b, w3b, w2b, xbuf, ybuf, dst, gsem, ssem):
    i = pl.program_id(0)
    n_used = nu_ref[0]
    cur = lax.rem(i, 2)
    oth = 1 - cur

    def gather_row(slot, buf, r):
        tok = dst[slot] & (TOKENS - 1)
        return pltpu.make_async_copy(x_hbm.at[pl.ds(tok, 1)], xbuf.at[buf, pl.ds(r, 1)], gsem)

    def scatter_row(slot, buf, r):
        return pltpu.make_async_copy(ybuf.at[buf, pl.ds(r, 1)], y_hbm.at[pl.ds(dst[slot], 1)],
                                     ssem)

    def start_rows(row_copy, tile, buf):
        base = tile * TM
        for r in range(TM):
            row_copy(base + r, buf, r).start(priority=r % 2)

    def wait_rows(row_copy, buf):
        for r in range(TM):
            row_copy(0, buf, 0).wait()

    def compute(buf):
        xb = xbuf[buf].astype(BF16)
        h1 = jnp.dot(xb, w1b[...], preferred_element_type=F32)
        h3 = jnp.dot(xb, w3b[...], preferred_element_type=F32)
        h = (h1 * _sigmoid(h1)) * h3
        ybuf[buf] = jnp.dot(h.astype(BF16), w2b[...], preferred_element_type=F32)

    @pl.when(i == 0)
    def _():
        ybuf[1] = jnp.zeros((TM, D_MODEL), F32)
        for e in range(N_EXPERTS):
            fill = pltpu.make_async_copy(
                ybuf.at[1], y_hbm.at[pl.ds(2 * TOKENS + e * TM, TM)], ssem)
            fill.start()
            fill.wait()

        for e in range(N_EXPERTS):
            def pad_body(s, carry, e=e):
                dst[s] = 2 * TOKENS + e * TM + (s - seg_ref[e])
                return carry
            lax.fori_loop(seg_ref[e], seg_ref[N_EXPERTS + e], pad_body, 0)

        def inv_body(t, carry):
            dst[pos_ref[t]] = t
            dst[pos_ref[TOKENS + t]] = TOKENS + t
            return carry
        lax.fori_loop(0, TOKENS, inv_body, 0, unroll=8)

        def first_start(r, carry):
            gather_row(r, 0, r).start()
            return carry
        lax.fori_loop(0, TM, first_start, 0, unroll=8)

        def first_wait(r, carry):
            gather_row(0, 0, 0).wait()
            return carry
        lax.fori_loop(0, TM, first_wait, 0, unroll=8)

    e = te_ref[i]
    prev = te_ref[jnp.maximum(i - 1, 0)]

    @pl.when(jnp.logical_and(i < n_used, jnp.logical_or(i == 0, e != prev)))
    def _():
        w1b[...] = w1_ref[...].astype(BF16)
        w3b[...] = w3_ref[...].astype(BF16)
        w2b[...] = w2_ref[...].astype(BF16)

    nxt = jnp.minimum(i + 1, n_used - 1)

    @pl.when(i == 0)
    def _():
        start_rows(gather_row, nxt, 1)
        compute(0)
        wait_rows(gather_row, 1)

    @pl.when(jnp.logical_and(i >= 1, i < n_used))
    def _():
        start_rows(gather_row, nxt, oth)
        start_rows(scatter_row, i - 1, oth)
        compute(cur)
        wait_rows(gather_row, oth)
        wait_rows(scatter_row, oth)

    @pl.when(i == n_used)
    def _():
        def last_start(r, carry):
            scatter_row((i - 1) * TM + r, oth, r).start()
            return carry
        lax.fori_loop(0, TM, last_start, 0, unroll=8)

        def last_wait(r, carry):
            scatter_row(0, oth, 0).wait()
            return carry
        lax.fori_loop(0, TM, last_wait, 0, unroll=8)


def _experts(te, nu, pos_flat, seg, x1, w1, w3, w2, layer):
    wspec = lambda r, c: pl.BlockSpec((None, None, r, c),
                                      lambda i, te, nu, pos, seg: (layer, te[i], 0, 0))
    return pl.pallas_call(
        _expert_kernel,
        out_shape=jax.ShapeDtypeStruct((Y_ROWS, D_MODEL), F32),
        grid_spec=pltpu.PrefetchScalarGridSpec(
            num_scalar_prefetch=4,
            grid=(N_MTILES + 1,),
            in_specs=[pl.BlockSpec(memory_space=pl.ANY),
                      wspec(D_MODEL, D_EXPERT), wspec(D_MODEL, D_EXPERT),
                      wspec(D_EXPERT, D_MODEL)],
            out_specs=pl.BlockSpec(memory_space=pl.ANY),
            scratch_shapes=[pltpu.VMEM((D_MODEL, D_EXPERT), BF16),
                            pltpu.VMEM((D_MODEL, D_EXPERT), BF16),
                            pltpu.VMEM((D_EXPERT, D_MODEL), BF16),
                            pltpu.VMEM((2, TM, D_MODEL), F32),
                            pltpu.VMEM((2, TM, D_MODEL), F32),
                            pltpu.SMEM((N_SLOTS,), I32),
                            pltpu.SemaphoreType.DMA,
                            pltpu.SemaphoreType.DMA],
        ),
        compiler_params=pltpu.CompilerParams(
            dimension_semantics=("arbitrary",), vmem_limit_bytes=VMEM_LIMIT,
            has_side_effects=True),
        name="moe_experts",
    )(te, nu, pos_flat, seg, x1, w1, w3, w2)


def _combine_kernel(x1_ref, p_ref, wcol_ref, y0_ref, y1_ref, wp_ref, wg_ref, g_ref, b_ref,
                    o_ref):
    x1 = x1_ref[...]
    gate = _sigmoid(jnp.dot(x1.astype(BF16), wg_ref[...], preferred_element_type=F32))
    ple = jnp.dot(p_ref[...].astype(BF16), wp_ref[...], preferred_element_type=F32) * gate
    wc = wcol_ref[...]
    ffn = wc[:, 0:1] * y0_ref[...] + wc[:, 1:2] * y1_ref[...]
    o_ref[...] = _layer_norm(ALPHA * x1 + ffn + ple, g_ref[...], b_ref[...])


def _combine(x1, p, wcol, ys, wp, wg, g, b):
    full = lambda shape: pl.BlockSpec(shape, lambda i: (0,) * len(shape))
    return pl.pallas_call(
        _combine_kernel,
        out_shape=jax.ShapeDtypeStruct((TOKENS, D_MODEL), F32),
        grid=(TOKENS // RD,),
        in_specs=[pl.BlockSpec((RD, D_MODEL), lambda i: (i, 0)),
                  pl.BlockSpec((RD, PLE_DIM), lambda i: (i, 0)),
                  pl.BlockSpec((RD, 128), lambda i: (i, 0)),
                  pl.BlockSpec((RD, D_MODEL), lambda i: (i, 0)),
                  pl.BlockSpec((RD, D_MODEL), lambda i: (TOKENS // RD + i, 0)),
                  full((PLE_DIM, D_MODEL)),
                  full((D_MODEL, D_MODEL)),
                  full((1, D_MODEL)),
                  full((1, D_MODEL))],
        out_specs=pl.BlockSpec((RD, D_MODEL), lambda i: (i, 0)),
        compiler_params=pltpu.CompilerParams(
            dimension_semantics=("arbitrary",), vmem_limit_bytes=VMEM_LIMIT),
        name="moe_combine",
    )(x1, p, wcol, ys, ys, wp, wg, g, b)


def _qkv_kernel(x_ref, w_ref, o_ref):
    o_ref[...] = jnp.dot(x_ref[...].astype(BF16), w_ref[...],
                         preferred_element_type=F32).astype(BF16)


def _qkv_proj(x, w):
    return pl.pallas_call(
        _qkv_kernel,
        out_shape=jax.ShapeDtypeStruct((TOKENS, 3 * D_MODEL), BF16),
        grid=(TOKENS // RP,),
        in_specs=[pl.BlockSpec((RP, D_MODEL), lambda i: (i, 0)),
                  pl.BlockSpec((D_MODEL, 3 * D_MODEL), lambda i: (0, 0))],
        out_specs=pl.BlockSpec((RP, 3 * D_MODEL), lambda i: (i, 0)),
        compiler_params=pltpu.CompilerParams(
            dimension_semantics=("arbitrary",), vmem_limit_bytes=VMEM_LIMIT),
        name="qkv_proj",
    )(x, w)


def _attn_kernel(q_ref, k_ref, v_ref, o_ref, acc, cbuf, bnd, kinf):
    qb = pl.program_id(2)
    lane = lax.broadcasted_iota(I32, (1, 128), 1)
    half = (lane < 64, lane >= 64)

    @pl.when(qb == 0)
    def _():
        for pr in range(ATT_PAIRS):
            ka = jnp.max(jnp.abs(k_ref[:, pr * 128:(pr + 1) * 128].astype(F32)),
                         axis=0, keepdims=True)
            for hh in range(2):
                m = jnp.max(jnp.where(half[hh], ka, 0.0), axis=1, keepdims=True)
                kinf[2 * pr + hh] = jnp.broadcast_to(m, (8, 128))

    qh = []
    for pr in range(ATT_PAIRS):
        q = q_ref[:, pr * 128:(pr + 1) * 128] * jnp.asarray(0.125, BF16)
        zero = jnp.zeros_like(q)
        qh.append((jnp.where(half[0], q, zero), jnp.where(half[1], q, zero)))
        qa = jnp.abs(q.astype(F32))
        for hh in range(2):
            qn = jnp.sum(jnp.where(half[hh], qa, 0.0), axis=1, keepdims=True)
            bnd[2 * pr + hh] = (jnp.broadcast_to(qn, (SB_BLOCK, 128))
                                * kinf[2 * pr + hh][0:1, :] * BOUND_SLACK)
    r = lax.broadcasted_iota(I32, (SB_BLOCK, SB_BLOCK), 0)
    cidx = lax.broadcasted_iota(I32, (SB_BLOCK, SB_BLOCK), 1)
    neg_suffix = jnp.where(r >= cidx, -1.0, 0.0).astype(BF16)
    causal = cidx < r

    acc[...] = jnp.zeros_like(acc)
    cbuf[...] = jnp.zeros_like(cbuf)

    def block(kb, masked):
        rows = pl.ds(pl.multiple_of(kb * SB_BLOCK, SB_BLOCK), SB_BLOCK)
        heads = [(pr, hh) for pr in range(ATT_PAIRS) for hh in range(2)]
        ks = [k_ref[rows, pr * 128:(pr + 1) * 128] for pr in range(ATT_PAIRS)]
        vs = [v_ref[rows, pr * 128:(pr + 1) * 128] for pr in range(ATT_PAIRS)]
        zs = [lax.dot_general(qh[pr][hh], ks[pr], NT_DIMS, preferred_element_type=F32)
              for pr, hh in heads]
        sps = []
        for z in zs:
            sp = jnp.maximum(z, 0.0) + jnp.log(1.0 + jnp.exp2(jnp.abs(z) * (-LOG2E)))
            if masked:
                sp = jnp.where(causal, sp, 0.0)
            sps.append(sp.astype(BF16))
        rss = []
        for idx, sp in enumerate(sps):
            c = cbuf[idx]
            rss.append(jnp.dot(sp, neg_suffix, preferred_element_type=F32)
                       + jnp.concatenate([c, c], axis=1))
        for idx, (pr, hh) in enumerate(heads):
            w = jnp.exp(zs[idx] + rss[idx])
            if masked:
                w = jnp.where(causal, w, 0.0)
            acc[idx] = acc[idx] + jnp.dot(w.astype(BF16), vs[pr], preferred_element_type=F32)
            cbuf[idx] = jnp.broadcast_to(rss[idx][:, 0:1], (SB_BLOCK, 128))

    def log_weight_bound():
        m = cbuf[0] + bnd[0]
        for idx in range(1, 2 * ATT_PAIRS):
            m = jnp.maximum(m, cbuf[idx] + bnd[idx])
        return jnp.max(m)

    block(qb, True)

    def cond(carry):
        j, m = carry
        return jnp.logical_and(j < qb, m > EXIT_LOG_WEIGHT)

    def body(carry):
        j, _ = carry
        block(qb - 1 - j, False)
        return j + 1, log_weight_bound()

    lax.while_loop(cond, body, (jnp.int32(0), log_weight_bound()))
    for pr in range(ATT_PAIRS):
        o_ref[:, pr * 128:(pr + 1) * 128] = jnp.where(
            lane < 64, acc[2 * pr], acc[2 * pr + 1]).astype(BF16)


def _attention(qkv):
    n_hg = D_MODEL // ATT_LANES
    return pl.pallas_call(
        _attn_kernel,
        out_shape=jax.ShapeDtypeStruct((BATCH, SEQ, D_MODEL), BF16),
        grid=(BATCH, n_hg, SEQ // SB_BLOCK),
        in_specs=[pl.BlockSpec((None, SB_BLOCK, ATT_LANES), lambda b, h, i: (b, i, h)),
                  pl.BlockSpec((None, SEQ, ATT_LANES), lambda b, h, i: (b, 0, n_hg + h)),
                  pl.BlockSpec((None, SEQ, ATT_LANES), lambda b, h, i: (b, 0, 2 * n_hg + h))],
        out_specs=pl.BlockSpec((None, SB_BLOCK, ATT_LANES), lambda b, h, i: (b, i, h)),
        scratch_shapes=[pltpu.VMEM((2 * ATT_PAIRS, SB_BLOCK, 128), F32),
                        pltpu.VMEM((2 * ATT_PAIRS, SB_BLOCK, 128), F32),
                        pltpu.VMEM((2 * ATT_PAIRS, SB_BLOCK, 128), F32),
                        pltpu.VMEM((2 * ATT_PAIRS, 8, 128), F32)],
        compiler_params=pltpu.CompilerParams(
            dimension_semantics=("arbitrary", "arbitrary", "arbitrary"),
            vmem_limit_bytes=VMEM_LIMIT),
        name="sb_attention",
    )(qkv, qkv, qkv)


def _moe_ffn(x1, ri, wcol, p_rows, layer, moe_w1, moe_w3, moe_w2, wp, wg, g, b):
    pos, te, seg = _rank(ri)
    pos_flat = pos[0:2].reshape(2 * TOKENS)
    ys = _experts(te[0, :N_MTILES + 1], te[1, :1], pos_flat, seg[:, 0], x1,
                  moe_w1, moe_w3, moe_w2, layer)
    return _combine(x1, p_rows, wcol, ys, wp, wg, g, b)


def kernel(x, p, ab_w_in, s5_lambda_re, s5_lambda_im, s5_log_dt, s5_b_re, s5_b_im, s5_c_re, s5_c_im, s5_d, s5_w_glu, pool_w, pool_scale, ab_w_out, sb_w_qkv, sb_w_out, ln_mix_g, ln_mix_b, ln_ffn_g, ln_ffn_b, router_w, router_bias, moe_w1, moe_w3, moe_w2, ple_w_proj, ple_w_gate):
    row = lambda a: a.reshape(1, -1)
    rwt = router_w.T
    rwh = rwt.astype(BF16)
    rwl = (rwt - rwh.astype(F32)).astype(BF16)
    rb = router_bias.reshape(N_EXPERTS, 1)

    x_tm = x.transpose(1, 0, 2).reshape(TOKENS, D_MODEL)
    p_tm = p[0].transpose(1, 0, 2).reshape(TOKENS, PLE_DIM)
    a_re, a_im, bcat = _s5_prep(s5_lambda_re[0], s5_lambda_im[0], s5_log_dt[0],
                                s5_b_re[0], s5_b_im[0])

    def c_blockdiag(c):
        c4 = c.reshape(S5_CHUNKS, 8, S5_H, S5_P).transpose(0, 1, 3, 2)
        eye = jnp.eye(8, dtype=F32)
        full = c4[:, :, :, None, :] * eye[None, :, None, :, None]
        return full.reshape(S5_CHUNKS, 8 * S5_P, 8 * S5_H)

    ccat = jnp.concatenate([c_blockdiag(s5_c_re[0]), -c_blockdiag(s5_c_im[0])],
                           axis=1).astype(BF16)
    mix_in = _mixer0(x_tm, ab_w_in[0].astype(BF16), bcat, a_re, a_im, ccat,
                     row(s5_d[0]), s5_w_glu[0].astype(BF16), pool_w[0].astype(BF16),
                     row(pool_scale[0]))
    x1, ri, wcol = _proj_ln_route(mix_in, ab_w_out[0].astype(BF16), x_tm,
                                  row(ln_mix_g[0]), row(ln_mix_b[0]), rwh, rwl, rb)
    x2 = _moe_ffn(x1, ri, wcol, p_tm, 0, moe_w1, moe_w3, moe_w2,
                  ple_w_proj[0].astype(BF16), ple_w_gate[0].astype(BF16),
                  row(ln_ffn_g[0]), row(ln_ffn_b[0]))

    xb = x2.reshape(SEQ, BATCH, D_MODEL).transpose(1, 0, 2).reshape(TOKENS, D_MODEL)
    qkv = _qkv_proj(xb, sb_w_qkv[0].astype(BF16))
    att = _attention(qkv.reshape(BATCH, SEQ, 3 * D_MODEL)).reshape(TOKENS, D_MODEL)
    x3, ri, wcol = _proj_ln_route(att, sb_w_out[0].astype(BF16), xb,
                                  row(ln_mix_g[1]), row(ln_mix_b[1]), rwh, rwl, rb)
    x4 = _moe_ffn(x3, ri, wcol, p[1].reshape(TOKENS, PLE_DIM), 1, moe_w1, moe_w3, moe_w2,
                  ple_w_proj[1].astype(BF16), ple_w_gate[1].astype(BF16),
                  row(ln_ffn_g[1]), row(ln_ffn_b[1]))
    return x4.reshape(BATCH, SEQ, D_MODEL)
```

```python
import functools
import math

import jax
import jax.numpy as jnp
from jax import lax
from jax.experimental import pallas as pl
from jax.experimental.pallas import tpu as pltpu

F32 = jnp.float32
BF16 = jnp.bfloat16
I32 = jnp.int32

D_MODEL = 1024
BATCH = 8
SEQ = 2048
DEPTH = 2
TOKENS = BATCH * SEQ

D_A = 512
S5_H = 16
S5_G = 32
S5_P = 64
S5_STATE = S5_G * S5_P
S5_CHUNKS = 4
POOL_WINDOWS = (2, 4, 8, 16)
POOL_C = 128
POOL_HALO = 16 * BATCH
SB_BLOCK = 256
ATT_PAIRS = 4
ATT_LANES = 128 * ATT_PAIRS
N_EXPERTS = 16
EXPERTS_PER_GROUP = 4
D_EXPERT = 512
PLE_DIM = 256
ALPHA = (2 * DEPTH) ** 0.25
LN_EPS = 1e-5

TS0 = 64
R0 = TS0 * BATCH
RP = 512
PROJ_SPLIT = 2
RR = 512
RD = 256
TM = 256
PAIRS_PER_GROUP = 6
PAIR_A = (0, 2, 2, 3, 3, 3)
PAIR_B = (1, 1, 0, 0, 1, 2)
N_SEG = (N_EXPERTS // EXPERTS_PER_GROUP) * PAIRS_PER_GROUP
SEG_ROWS = 32
XW = D_MODEL + 128
N_SLOTS = TOKENS + N_SEG * TM
N_MTILES = N_SLOTS // TM
TE_LANES = 256

VMEM_LIMIT = 52 * 1024 * 1024
NT_DIMS = (((1,), (1,)), ((), ()))
LOG2E = 1.0 / math.log(2.0)
EXIT_LOG_WEIGHT = -110.0
BOUND_SLACK = 1.01


def _sigmoid(x):
    return 1.0 / (1.0 + jnp.exp(-x))


def _layer_norm(x, g, b):
    mu = jnp.mean(x, axis=-1, keepdims=True)
    xc = x - mu
    var = jnp.mean(xc * xc, axis=-1, keepdims=True)
    return xc * lax.rsqrt(var + LN_EPS) * g + b


def _s5_prep_kernel(lre_ref, lim_ref, ldt_ref, bre_ref, bim_ref,
                    are_ref, aim_ref, bcat_ref):
    lam_re = lre_ref[...]
    lam_im = lim_ref[...]
    dt = jnp.exp(ldt_ref[...])
    mag = jnp.exp(lam_re * dt)
    ang = lam_im * dt
    lb_re = mag * jnp.cos(ang)
    lb_im = mag * jnp.sin(ang)
    are_ref[...] = lb_re
    aim_ref[...] = lb_im
    den = lam_re * lam_re + lam_im * lam_im
    num_re = lb_re - 1.0
    f_re = (num_re * lam_re + lb_im * lam_im) / den
    f_im = (lb_im * lam_re - num_re * lam_im) / den
    for c in range(S5_CHUNKS):
        fr = f_re[:, c * 512:(c + 1) * 512]
        fi = f_im[:, c * 512:(c + 1) * 512]
        br = bre_ref[c]
        bi = bim_ref[c]
        bcat_ref[c, :, 0:512] = (fr * br - fi * bi).astype(BF16)
        bcat_ref[c, :, 512:1024] = (fr * bi + fi * br).astype(BF16)


def _s5_prep(lam_re, lam_im, log_dt, b_re, b_im):
    def blockdiag(b):
        b4 = b.reshape(S5_CHUNKS, 8, S5_P, S5_H).transpose(0, 1, 3, 2)
        eye = jnp.eye(8, dtype=F32)
        full = b4[:, :, :, None, :] * eye[None, :, None, :, None]
        return full.reshape(S5_CHUNKS, 8 * S5_H, 8 * S5_P)

    return pl.pallas_call(
        _s5_prep_kernel,
        out_shape=(jax.ShapeDtypeStruct((1, S5_STATE), F32),
                   jax.ShapeDtypeStruct((1, S5_STATE), F32),
                   jax.ShapeDtypeStruct((S5_CHUNKS, 128, 1024), BF16)),
        name="s5_prep",
    )(lam_re.reshape(1, S5_STATE), lam_im.reshape(1, S5_STATE),
      jnp.repeat(log_dt, S5_P).reshape(1, S5_STATE), blockdiag(b_re), blockdiag(b_im))


def _mixer0_kernel(x_ref, win_ref, bcat_ref, are_ref, aim_ref, ccat_ref, d_ref,
                   wglu_ref, poolw_ref, pscale_ref, o_ref, bus, hst, pe):
    i = pl.program_id(0)

    @pl.when(i == 0)
    def _():
        hst[...] = jnp.zeros_like(hst)
        pe[0:POOL_HALO, :] = jnp.zeros((POOL_HALO, D_A), F32)

    h = jnp.dot(x_ref[...].astype(BF16), win_ref[...], preferred_element_type=F32)
    u = h[:, :D_A]
    v = h[:, D_A:]
    ub = u.astype(BF16)

    for c in range(S5_CHUNKS):
        bus[:, c * 1024:(c + 1) * 1024] = jnp.dot(
            ub[:, c * 128:(c + 1) * 128], bcat_ref[c], preferred_element_type=F32)

    for c in range(S5_CHUNKS):
        re_cols = slice(c * 1024, c * 1024 + 512)
        im_cols = slice(c * 1024 + 512, (c + 1) * 1024)
        ar = jnp.broadcast_to(are_ref[:, c * 512:(c + 1) * 512], (BATCH, 512))
        ai = jnp.broadcast_to(aim_ref[:, c * 512:(c + 1) * 512], (BATCH, 512))

        def step(t, carry, re_cols=re_cols, im_cols=im_cols, ar=ar, ai=ai):
            hr, hi = carry
            rows = pl.ds(pl.multiple_of(t * BATCH, BATCH), BATCH)
            nr = ar * hr - ai * hi + bus[rows, re_cols]
            ni = ar * hi + ai * hr + bus[rows, im_cols]
            bus[rows, re_cols] = nr
            bus[rows, im_cols] = ni
            return nr, ni

        hr, hi = lax.fori_loop(0, TS0, step, (hst[:, re_cols], hst[:, im_cols]),
                               unroll=True)
        hst[:, re_cols] = hr
        hst[:, im_cols] = hi

    ys = [jnp.dot(bus[:, c * 1024:(c + 1) * 1024].astype(BF16), ccat_ref[c],
                  preferred_element_type=F32) for c in range(S5_CHUNKS)]
    y = jnp.concatenate(ys, axis=1) + d_ref[...] * u
    y = 0.5 * y * (1.0 + jnp.tanh(math.sqrt(2.0 / math.pi) * (y + 0.044715 * (y * y * y))))
    ga = y * _sigmoid(jnp.dot(y.astype(BF16), wglu_ref[...], preferred_element_type=F32))
    o_ref[:, 0:D_A] = ga.astype(BF16)

    pe[POOL_HALO:, :] = v
    t_glob = lax.shift_right_logical(
        lax.broadcasted_iota(I32, (R0, 1), 0), int(math.log2(BATCH))) + i * TS0
    for gi, w in enumerate(POOL_WINDOWS):
        cols = slice(gi * POOL_C, (gi + 1) * POOL_C)
        s = pe[:, cols]
        off = BATCH
        while off < BATCH * w:
            s = s[off:] + s[:-off]
            off *= 2
        s = s[POOL_HALO - BATCH * (w - 1):]
        cnt = jnp.minimum(t_glob + 1, w).astype(F32)
        pooled = s / cnt - v[:, cols]
        mixed = jnp.dot(pooled.astype(BF16), poolw_ref[gi], preferred_element_type=F32)
        o_ref[:, D_A + gi * POOL_C:D_A + (gi + 1) * POOL_C] = (
            mixed * pscale_ref[:, cols]).astype(BF16)
    pe[0:POOL_HALO, :] = pe[R0:R0 + POOL_HALO, :]


def _mixer0(x_tm, win, bcat, a_re, a_im, ccat, dskip, wglu, poolw, pscale):
    full = lambda shape: pl.BlockSpec(shape, lambda i: (0,) * len(shape))
    return pl.pallas_call(
        _mixer0_kernel,
        out_shape=jax.ShapeDtypeStruct((TOKENS, D_MODEL), BF16),
        grid=(SEQ // TS0,),
        in_specs=[
            pl.BlockSpec((R0, D_MODEL), lambda i: (i, 0)),
            full((D_MODEL, D_MODEL)),
            full((S5_CHUNKS, 128, 1024)),
            full((1, S5_STATE)),
            full((1, S5_STATE)),
            full((S5_CHUNKS, 1024, 128)),
            full((1, D_A)),
            full((D_A, D_A)),
            full((4, POOL_C, POOL_C)),
            full((1, D_A)),
        ],
        out_specs=pl.BlockSpec((R0, D_MODEL), lambda i: (i, 0)),
        scratch_shapes=[
            pltpu.VMEM((R0, 2 * S5_STATE), F32),
            pltpu.VMEM((BATCH, 2 * S5_STATE), F32),
            pltpu.VMEM((POOL_HALO + R0, D_A), F32),
        ],
        compiler_params=pltpu.CompilerParams(
            dimension_semantics=("arbitrary",), vmem_limit_bytes=VMEM_LIMIT),
        name="mixer0",
    )(x_tm, win, bcat, a_re, a_im, ccat, dskip, wglu, poolw, pscale)


def _route(x1, rwh_ref, rwl_ref, rb_ref):
    n = x1.shape[0]
    xh = x1.astype(BF16)
    xl = (x1 - xh.astype(F32)).astype(BF16)
    rwh = rwh_ref[...]
    logits = (lax.dot_general(rwh, xh, NT_DIMS, preferred_element_type=F32)
              + lax.dot_general(rwh, xl, NT_DIMS, preferred_element_type=F32)
              + lax.dot_general(rwl_ref[...], xh, NT_DIMS, preferred_element_type=F32))
    scores = _sigmoid(logits)
    sel = scores + rb_ref[...]
    row = lambda a, e: a[e:e + 1, :]

    best = None
    grp = None
    for g in range(N_EXPERTS // EXPERTS_PER_GROUP):
        m = [row(sel, EXPERTS_PER_GROUP * g + k) for k in range(EXPERTS_PER_GROUP)]
        gs = None
        for a in range(EXPERTS_PER_GROUP):
            for b in range(a + 1, EXPERTS_PER_GROUP):
                pair = m[a] + m[b]
                gs = pair if gs is None else jnp.maximum(gs, pair)
        if best is None:
            best, grp = gs, jnp.zeros(gs.shape, I32)
        else:
            better = gs > best
            grp = jnp.where(better, g, grp)
            best = jnp.where(better, gs, best)

    def pick(a, k):
        out = row(a, k)
        for g in range(1, N_EXPERTS // EXPERTS_PER_GROUP):
            out = jnp.where(grp == g, row(a, EXPERTS_PER_GROUP * g + k), out)
        return out

    cs = [pick(sel, k) for k in range(EXPERTS_PER_GROUP)]
    ss = [pick(scores, k) for k in range(EXPERTS_PER_GROUP)]
    m1, i1, s1 = cs[0], jnp.zeros(cs[0].shape, I32), ss[0]
    for k in range(1, EXPERTS_PER_GROUP):
        better = cs[k] > m1
        i1 = jnp.where(better, k, i1)
        s1 = jnp.where(better, ss[k], s1)
        m1 = jnp.where(better, cs[k], m1)
    m2 = jnp.full(m1.shape, -jnp.inf, F32)
    i2 = jnp.zeros(m1.shape, I32)
    s2 = jnp.zeros(m1.shape, F32)
    for k in range(EXPERTS_PER_GROUP):
        better = jnp.logical_and(i1 != k, cs[k] > m2)
        i2 = jnp.where(better, k, i2)
        s2 = jnp.where(better, ss[k], s2)
        m2 = jnp.where(better, cs[k], m2)
    tot = s1 + s2
    w0 = s1 / tot
    w1 = s2 / tot
    lo = jnp.minimum(i1, i2)
    hi = jnp.maximum(i1, i2)
    k = jnp.zeros(lo.shape, I32)
    in_a = jnp.zeros(lo.shape, I32)
    for kk, (a, b) in enumerate(zip(PAIR_A, PAIR_B)):
        hit = jnp.logical_and(lo == min(a, b), hi == max(a, b))
        k = jnp.where(hit, kk, k)
        in_a = jnp.where(hit, a, in_a)
    combo = grp * PAIRS_PER_GROUP + k
    first_in_a = i1 == in_a
    wa = jnp.where(first_in_a, w0, w1)
    wb = jnp.where(first_in_a, w1, w0)
    ri = jnp.concatenate([combo, jnp.zeros((7, n), I32)], axis=0)
    wslab = jnp.concatenate([wa, wb, jnp.zeros((126, n), F32)], axis=0)
    return ri, wslab.T


def _proj_ln_route_kernel(a_ref, w_ref, x_ref, g_ref, b_ref, rwh_ref, rwl_ref, rb_ref,
                          x1_ref, ri_ref):
    n = RP // PROJ_SPLIT
    parts = [slice(k * n, (k + 1) * n) for k in range(PROJ_SPLIT)]
    mixes = [jnp.dot(a_ref[rows, :], w_ref[...], preferred_element_type=F32)
             for rows in parts]
    for rows, mix in zip(parts, mixes):
        x1 = _layer_norm(ALPHA * x_ref[rows, :] + mix, g_ref[...], b_ref[...])
        x1_ref[rows, 0:D_MODEL] = x1
        ri, wcol = _route(x1, rwh_ref, rwl_ref, rb_ref)
        ri_ref[:, rows] = ri
        x1_ref[rows, D_MODEL:XW] = wcol


def _proj_ln_route(a, w, xres, g, b, rwh, rwl, rb):
    full = lambda shape: pl.BlockSpec(shape, lambda i: (0,) * len(shape))
    return pl.pallas_call(
        _proj_ln_route_kernel,
        out_shape=(jax.ShapeDtypeStruct((TOKENS, XW), F32),
                   jax.ShapeDtypeStruct((8, TOKENS), I32)),
        grid=(TOKENS // RP,),
        in_specs=[
            pl.BlockSpec((RP, D_MODEL), lambda i: (i, 0)),
            full((D_MODEL, D_MODEL)),
            pl.BlockSpec((RP, D_MODEL), lambda i: (i, 0)),
            full((1, D_MODEL)),
            full((1, D_MODEL)),
            full((N_EXPERTS, D_MODEL)),
            full((N_EXPERTS, D_MODEL)),
            full((N_EXPERTS, 1)),
        ],
        out_specs=(pl.BlockSpec((RP, XW), lambda i: (i, 0)),
                   pl.BlockSpec((8, RP), lambda i: (0, i))),
        compiler_params=pltpu.CompilerParams(
            dimension_semantics=("arbitrary",), vmem_limit_bytes=VMEM_LIMIT),
        name="proj_ln_route",
    )(a, w, xres, g, b, rwh, rwl, rb)


def _rank_kernel(ri_ref, pos_ref, te_ref, seg_ref, cnt, offs, tot):
    ph = pl.program_id(0)
    i = pl.program_id(1)
    n_tiles = pl.num_programs(1)
    s_iota = lax.broadcasted_iota(I32, (SEG_ROWS, RR), 0)
    hit = s_iota == ri_ref[0:1, :]
    mask = jnp.where(hit, 1.0, 0.0)
    tile_cnt = jnp.sum(mask, axis=1, keepdims=True)

    @pl.when(jnp.logical_and(ph == 0, i == 0))
    def _():
        cnt[...] = jnp.zeros_like(cnt)

    @pl.when(ph == 0)
    def _():
        cnt[...] = cnt[...] + tile_cnt
        pos_ref[...] = jnp.zeros_like(pos_ref)

    @pl.when(jnp.logical_and(ph == 0, i == n_tiles - 1))
    def _():
        c = cnt[...]
        padded = jnp.ceil(c * (1.0 / TM)) * TM
        sub = lax.broadcasted_iota(I32, (SEG_ROWS, 128), 0)
        acc = jnp.zeros((SEG_ROWS, 128), F32)
        for e in range(N_SEG):
            acc = acc + jnp.where(sub > e, padded[e:e + 1, :], 0.0)
        offs[...] = acc
        tot[...] = acc + padded
        cnt[...] = jnp.zeros_like(cnt)

    @pl.when(ph == 1)
    def _():
        r = lax.broadcasted_iota(I32, (RR, RR), 0)
        cidx = lax.broadcasted_iota(I32, (RR, RR), 1)
        tri = jnp.where(r < cidx, 1.0, 0.0).astype(BF16)
        prefix = jnp.dot(mask.astype(BF16), tri, preferred_element_type=F32)
        slot = prefix + cnt[:, 0:1] + offs[:, 0:1]
        p0 = jnp.sum(jnp.where(hit, slot, 0.0), axis=0, keepdims=True)
        pos_ref[...] = jnp.concatenate([p0.astype(I32), jnp.zeros((7, RR), I32)], axis=0)
        cnt[...] = cnt[...] + tile_cnt

    @pl.when(jnp.logical_and(ph == 1, i == n_tiles - 1))
    def _():
        ends = tot[:, 0:1]
        start = (lax.broadcasted_iota(I32, (SEG_ROWS, TE_LANES), 1) * TM).astype(F32)
        te = jnp.sum(jnp.where(start >= ends, 1.0, 0.0), axis=0, keepdims=True)
        te = jnp.minimum(te, N_SEG - 1.0).astype(I32)
        n_used = (tot[SEG_ROWS - 1:SEG_ROWS, :] * (1.0 / TM)).astype(I32)
        n_used = jnp.concatenate([n_used, n_used], axis=1)
        te_ref[...] = jnp.concatenate(
            [te, n_used, jnp.zeros((6, TE_LANES), I32)], axis=0)
        seg_ref[0:SEG_ROWS, :] = (offs[...] + cnt[...]).astype(I32)
        seg_ref[SEG_ROWS:2 * SEG_ROWS, :] = tot[...].astype(I32)


def _rank(ri):
    return pl.pallas_call(
        _rank_kernel,
        out_shape=(jax.ShapeDtypeStruct((8, TOKENS), I32),
                   jax.ShapeDtypeStruct((8, TE_LANES), I32),
                   jax.ShapeDtypeStruct((2 * SEG_ROWS, 128), I32)),
        grid=(2, TOKENS // RR),
        in_specs=[pl.BlockSpec((8, RR), lambda p, i: (0, i))],
        out_specs=(pl.BlockSpec((8, RR), lambda p, i: (0, i * p)),
                   pl.BlockSpec((8, TE_LANES), lambda p, i: (0, 0)),
                   pl.BlockSpec((2 * SEG_ROWS, 128), lambda p, i: (0, 0))),
        scratch_shapes=[pltpu.VMEM((SEG_ROWS, 128), F32),
                        pltpu.VMEM((SEG_ROWS, 128), F32),
                        pltpu.VMEM((SEG_ROWS, 128), F32)],
        compiler_params=pltpu.CompilerParams(
            dimension_semantics=("arbitrary", "arbitrary")),
        name="moe_rank",
    )(ri)


def _expert_kernel(tea_ref, teb_ref, nu_ref, pos_ref, seg_ref, x_hbm,
                   w1a_ref, w3a_ref, w2a_ref, w1b_ref, w3b_ref, w2b_ref, y_hbm,
                   wa1, wa3, wa2, wb1, wb3, wb2, xbuf, ybuf, dst, gsem, ssem):
    i = pl.program_id(0)
    n_used = nu_ref[0]
    cur = lax.rem(i, 2)
    oth = 1 - cur

    def gather_row(slot, buf, r):
        tok = dst[slot] & (TOKENS - 1)
        return pltpu.make_async_copy(x_hbm.at[pl.ds(tok, 1)], xbuf.at[buf, pl.ds(r, 1)], gsem)

    def scatter_row(slot, buf, r):
        return pltpu.make_async_copy(ybuf.at[buf, pl.ds(r, 1)], y_hbm.at[pl.ds(dst[slot], 1)],
                                     ssem)

    def start_rows(row_copy, tile, buf):
        base = tile * TM
        for r in range(TM):
            row_copy(base + r, buf, r).start(priority=r % 2)

    def wait_rows(row_copy, buf):
        for r in range(TM):
            row_copy(0, buf, 0).wait()

    def compute(buf):
        xf = xbuf[buf]
        xb = xf[:, 0:D_MODEL].astype(BF16)
        h1a = jnp.dot(xb, wa1[...], preferred_element_type=F32)
        h3a = jnp.dot(xb, wa3[...], preferred_element_type=F32)
        h1b = jnp.dot(xb, wb1[...], preferred_element_type=F32)
        h3b = jnp.dot(xb, wb3[...], preferred_element_type=F32)
        ha = (h1a * _sigmoid(h1a)) * h3a * xf[:, D_MODEL:D_MODEL + 1]
        hb = (h1b * _sigmoid(h1b)) * h3b * xf[:, D_MODEL + 1:D_MODEL + 2]
        ybuf[buf] = (jnp.dot(ha.astype(BF16), wa2[...], preferred_element_type=F32)
                     + jnp.dot(hb.astype(BF16), wb2[...], preferred_element_type=F32))

    @pl.when(i == 0)
    def _():
        ybuf[1] = jnp.zeros((TM, D_MODEL), F32)
        for e in range(N_SEG):
            fill = pltpu.make_async_copy(
                ybuf.at[1], y_hbm.at[pl.ds(TOKENS + e * TM, TM)], ssem)
            fill.start()
            fill.wait()

        for e in range(N_SEG):
            def pad_body(s, carry, e=e):
                dst[s] = TOKENS + e * TM + (s - seg_ref[e])
                return carry
            lax.fori_loop(seg_ref[e], seg_ref[SEG_ROWS + e], pad_body, 0)

        def inv_body(t, carry):
            dst[pos_ref[t]] = t
            return carry
        lax.fori_loop(0, TOKENS, inv_body, 0, unroll=8)

        def first_start(r, carry):
            gather_row(r, 0, r).start()
            return carry
        lax.fori_loop(0, TM, first_start, 0, unroll=8)

        def first_wait(r, carry):
            gather_row(0, 0, 0).wait()
            return carry
        lax.fori_loop(0, TM, first_wait, 0, unroll=8)

    prev = jnp.maximum(i - 1, 0)

    @pl.when(jnp.logical_and(i < n_used,
                             jnp.logical_or(i == 0, tea_ref[i] != tea_ref[prev])))
    def _():
        wa1[...] = w1a_ref[...].astype(BF16)
        wa3[...] = w3a_ref[...].astype(BF16)
        wa2[...] = w2a_ref[...].astype(BF16)

    @pl.when(jnp.logical_and(i < n_used,
                             jnp.logical_or(i == 0, teb_ref[i] != teb_ref[prev])))
    def _():
        wb1[...] = w1b_ref[...].astype(BF16)
        wb3[...] = w3b_ref[...].astype(BF16)
        wb2[...] = w2b_ref[...].astype(BF16)

    nxt = jnp.minimum(i + 1, n_used - 1)

    @pl.when(i == 0)
    def _():
        start_rows(gather_row, nxt, 1)
        compute(0)
        wait_rows(gather_row, 1)

    @pl.when(jnp.logical_and(i >= 1, i < n_used))
    def _():
        start_rows(gather_row, nxt, oth)
        start_rows(scatter_row, i - 1, oth)
        compute(cur)
        wait_rows(gather_row, oth)
        wait_rows(scatter_row, oth)

    @pl.when(i == n_used)
    def _():
        def last_start(r, carry):
            scatter_row((i - 1) * TM + r, oth, r).start()
            return carry
        lax.fori_loop(0, TM, last_start, 0, unroll=8)

        def last_wait(r, carry):
            scatter_row(0, oth, 0).wait()
            return carry
        lax.fori_loop(0, TM, last_wait, 0, unroll=8)


def _experts(tea, teb, nu, pos, seg, x1, w1, w3, w2, layer):
    def wspec(r, c, slot):
        return pl.BlockSpec(
            (None, None, r, c),
            lambda i, tea, teb, nu, pos, seg: (layer, (tea, teb)[slot][i], 0, 0))

    return pl.pallas_call(
        _expert_kernel,
        out_shape=jax.ShapeDtypeStruct((N_SLOTS, D_MODEL), F32),
        grid_spec=pltpu.PrefetchScalarGridSpec(
            num_scalar_prefetch=5,
            grid=(N_MTILES + 1,),
            in_specs=[pl.BlockSpec(memory_space=pl.ANY),
                      wspec(D_MODEL, D_EXPERT, 0), wspec(D_MODEL, D_EXPERT, 0),
                      wspec(D_EXPERT, D_MODEL, 0),
                      wspec(D_MODEL, D_EXPERT, 1), wspec(D_MODEL, D_EXPERT, 1),
                      wspec(D_EXPERT, D_MODEL, 1)],
            out_specs=pl.BlockSpec(memory_space=pl.ANY),
            scratch_shapes=[pltpu.VMEM((D_MODEL, D_EXPERT), BF16),
                            pltpu.VMEM((D_MODEL, D_EXPERT), BF16),
                            pltpu.VMEM((D_EXPERT, D_MODEL), BF16),
                            pltpu.VMEM((D_MODEL, D_EXPERT), BF16),
                            pltpu.VMEM((D_MODEL, D_EXPERT), BF16),
                            pltpu.VMEM((D_EXPERT, D_MODEL), BF16),
                            pltpu.VMEM((2, TM, XW), F32),
                            pltpu.VMEM((2, TM, D_MODEL), F32),
                            pltpu.SMEM((N_SLOTS,), I32),
                            pltpu.SemaphoreType.DMA,
                            pltpu.SemaphoreType.DMA],
        ),
        compiler_params=pltpu.CompilerParams(
            dimension_semantics=("arbitrary",), vmem_limit_bytes=VMEM_LIMIT,
            has_side_effects=True),
        name="moe_experts",
    )(tea, teb, nu, pos, seg, x1, w1, w3, w2, w1, w3, w2)


def _combine_kernel(x1_ref, p_ref, y_ref, wp_ref, wg_ref, g_ref, b_ref, o_ref):
    x1 = x1_ref[...]
    gate = _sigmoid(jnp.dot(x1.astype(BF16), wg_ref[...], preferred_element_type=F32))
    ple = jnp.dot(p_ref[...].astype(BF16), wp_ref[...], preferred_element_type=F32) * gate
    o_ref[...] = _layer_norm(ALPHA * x1 + y_ref[...] + ple, g_ref[...], b_ref[...])


def _combine(x1, p, ys, wp, wg, g, b):
    full = lambda shape: pl.BlockSpec(shape, lambda i: (0,) * len(shape))
    return pl.pallas_call(
        _combine_kernel,
        out_shape=jax.ShapeDtypeStruct((TOKENS, D_MODEL), F32),
        grid=(TOKENS // RD,),
        in_specs=[pl.BlockSpec((RD, D_MODEL), lambda i: (i, 0)),
                  pl.BlockSpec((RD, PLE_DIM), lambda i: (i, 0)),
                  pl.BlockSpec((RD, D_MODEL), lambda i: (i, 0)),
                  full((PLE_DIM, D_MODEL)),
                  full((D_MODEL, D_MODEL)),
                  full((1, D_MODEL)),
                  full((1, D_MODEL))],
        out_specs=pl.BlockSpec((RD, D_MODEL), lambda i: (i, 0)),
        compiler_params=pltpu.CompilerParams(
            dimension_semantics=("arbitrary",), vmem_limit_bytes=VMEM_LIMIT),
        name="moe_combine",
    )(x1, p, ys, wp, wg, g, b)


def _qkv_kernel(x_ref, w_ref, o_ref):
    o_ref[...] = jnp.dot(x_ref[...].astype(BF16), w_ref[...],
                         preferred_element_type=F32).astype(BF16)


def _qkv_proj(x, w):
    return pl.pallas_call(
        _qkv_kernel,
        out_shape=jax.ShapeDtypeStruct((TOKENS, 3 * D_MODEL), BF16),
        grid=(TOKENS // RP,),
        in_specs=[pl.BlockSpec((RP, D_MODEL), lambda i: (i, 0)),
                  pl.BlockSpec((D_MODEL, 3 * D_MODEL), lambda i: (0, 0))],
        out_specs=pl.BlockSpec((RP, 3 * D_MODEL), lambda i: (i, 0)),
        compiler_params=pltpu.CompilerParams(
            dimension_semantics=("arbitrary",), vmem_limit_bytes=VMEM_LIMIT),
        name="qkv_proj",
    )(x, w)


def _attn_kernel(q_ref, k_ref, v_ref, o_ref, acc, cbuf, bnd, kinf):
    qb = pl.program_id(2)
    lane = lax.broadcasted_iota(I32, (1, 128), 1)
    half = (lane < 64, lane >= 64)

    @pl.when(qb == 0)
    def _():
        for pr in range(ATT_PAIRS):
            ka = jnp.max(jnp.abs(k_ref[:, pr * 128:(pr + 1) * 128].astype(F32)),
                         axis=0, keepdims=True)
            for hh in range(2):
                m = jnp.max(jnp.where(half[hh], ka, 0.0), axis=1, keepdims=True)
                kinf[2 * pr + hh] = jnp.broadcast_to(m, (8, 128))

    qh = []
    for pr in range(ATT_PAIRS):
        q = q_ref[:, pr * 128:(pr + 1) * 128] * jnp.asarray(0.125, BF16)
        zero = jnp.zeros_like(q)
        qh.append((jnp.where(half[0], q, zero), jnp.where(half[1], q, zero)))
        qa = jnp.abs(q.astype(F32))
        for hh in range(2):
            qn = jnp.sum(jnp.where(half[hh], qa, 0.0), axis=1, keepdims=True)
            bnd[2 * pr + hh] = (jnp.broadcast_to(qn, (SB_BLOCK, 128))
                                * kinf[2 * pr + hh][0:1, :] * BOUND_SLACK)
    r = lax.broadcasted_iota(I32, (SB_BLOCK, SB_BLOCK), 0)
    cidx = lax.broadcasted_iota(I32, (SB_BLOCK, SB_BLOCK), 1)
    neg_suffix = jnp.where(r >= cidx, -1.0, 0.0).astype(BF16)
    causal = cidx < r

    acc[...] = jnp.zeros_like(acc)
    cbuf[...] = jnp.zeros_like(cbuf)

    def block(kb, masked):
        rows = pl.ds(pl.multiple_of(kb * SB_BLOCK, SB_BLOCK), SB_BLOCK)
        heads = [(pr, hh) for pr in range(ATT_PAIRS) for hh in range(2)]
        ks = [k_ref[rows, pr * 128:(pr + 1) * 128] for pr in range(ATT_PAIRS)]
        vs = [v_ref[rows, pr * 128:(pr + 1) * 128] for pr in range(ATT_PAIRS)]
        zs = [lax.dot_general(qh[pr][hh], ks[pr], NT_DIMS, preferred_element_type=F32)
              for pr, hh in heads]
        sps = []
        for z in zs:
            sp = jnp.maximum(z, 0.0) + jnp.log(1.0 + jnp.exp2(jnp.abs(z) * (-LOG2E)))
            if masked:
                sp = jnp.where(causal, sp, 0.0)
            sps.append(sp.astype(BF16))
        rss = []
        for idx, sp in enumerate(sps):
            c = cbuf[idx]
            rss.append(jnp.dot(sp, neg_suffix, preferred_element_type=F32)
                       + jnp.concatenate([c, c], axis=1))
        for idx, (pr, hh) in enumerate(heads):
            w = jnp.exp(zs[idx] + rss[idx])
            if masked:
                w = jnp.where(causal, w, 0.0)
            acc[idx] = acc[idx] + jnp.dot(w.astype(BF16), vs[pr], preferred_element_type=F32)
            cbuf[idx] = jnp.broadcast_to(rss[idx][:, 0:1], (SB_BLOCK, 128))

    def log_weight_bound():
        m = cbuf[0] + bnd[0]
        for idx in range(1, 2 * ATT_PAIRS):
            m = jnp.maximum(m, cbuf[idx] + bnd[idx])
        return jnp.max(m)

    block(qb, True)

    def cond(carry):
        j, m = carry
        return jnp.logical_and(j < qb, m > EXIT_LOG_WEIGHT)

    def body(carry):
        j, _ = carry
        block(qb - 1 - j, False)
        return j + 1, log_weight_bound()

    lax.while_loop(cond, body, (jnp.int32(0), log_weight_bound()))
    for pr in range(ATT_PAIRS):
        o_ref[:, pr * 128:(pr + 1) * 128] = jnp.where(
            lane < 64, acc[2 * pr], acc[2 * pr + 1]).astype(BF16)


def _attention(qkv):
    n_hg = D_MODEL // ATT_LANES
    return pl.pallas_call(
        _attn_kernel,
        out_shape=jax.ShapeDtypeStruct((BATCH, SEQ, D_MODEL), BF16),
        grid=(BATCH, n_hg, SEQ // SB_BLOCK),
        in_specs=[pl.BlockSpec((None, SB_BLOCK, ATT_LANES), lambda b, h, i: (b, i, h)),
                  pl.BlockSpec((None, SEQ, ATT_LANES), lambda b, h, i: (b, 0, n_hg + h)),
                  pl.BlockSpec((None, SEQ, ATT_LANES), lambda b, h, i: (b, 0, 2 * n_hg + h))],
        out_specs=pl.BlockSpec((None, SB_BLOCK, ATT_LANES), lambda b, h, i: (b, i, h)),
        scratch_shapes=[pltpu.VMEM((2 * ATT_PAIRS, SB_BLOCK, 128), F32),
                        pltpu.VMEM((2 * ATT_PAIRS, SB_BLOCK, 128), F32),
                        pltpu.VMEM((2 * ATT_PAIRS, SB_BLOCK, 128), F32),
                        pltpu.VMEM((2 * ATT_PAIRS, 8, 128), F32)],
        compiler_params=pltpu.CompilerParams(
            dimension_semantics=("arbitrary", "arbitrary", "arbitrary"),
            vmem_limit_bytes=VMEM_LIMIT),
        name="sb_attention",
    )(qkv, qkv, qkv)


def _moe_ffn(x1, ri, p_rows, layer, moe_w1, moe_w3, moe_w2, wp, wg, g, b):
    pos, te, seg = _rank(ri)
    tile_seg = te[0, :N_MTILES + 1]
    grp = tile_seg // PAIRS_PER_GROUP
    k = tile_seg % PAIRS_PER_GROUP
    tea = grp * EXPERTS_PER_GROUP + sum(jnp.where(k == kk, a, 0) for kk, a in enumerate(PAIR_A))
    teb = grp * EXPERTS_PER_GROUP + sum(jnp.where(k == kk, b_, 0) for kk, b_ in enumerate(PAIR_B))
    ys = _experts(tea, teb, te[1, :1], pos[0], seg[:, 0], x1, moe_w1, moe_w3, moe_w2, layer)
    return _combine(x1, p_rows, ys, wp, wg, g, b)


def kernel(x, p, ab_w_in, s5_lambda_re, s5_lambda_im, s5_log_dt, s5_b_re, s5_b_im, s5_c_re, s5_c_im, s5_d, s5_w_glu, pool_w, pool_scale, ab_w_out, sb_w_qkv, sb_w_out, ln_mix_g, ln_mix_b, ln_ffn_g, ln_ffn_b, router_w, router_bias, moe_w1, moe_w3, moe_w2, ple_w_proj, ple_w_gate):
    row = lambda a: a.reshape(1, -1)
    rwt = router_w.T
    rwh = rwt.astype(BF16)
    rwl = (rwt - rwh.astype(F32)).astype(BF16)
    rb = router_bias.reshape(N_EXPERTS, 1)

    x_tm = x.transpose(1, 0, 2).reshape(TOKENS, D_MODEL)
    p_tm = p[0].transpose(1, 0, 2).reshape(TOKENS, PLE_DIM)
    a_re, a_im, bcat = _s5_prep(s5_lambda_re[0], s5_lambda_im[0], s5_log_dt[0],
                                s5_b_re[0], s5_b_im[0])

    def c_blockdiag(c):
        c4 = c.reshape(S5_CHUNKS, 8, S5_H, S5_P).transpose(0, 1, 3, 2)
        eye = jnp.eye(8, dtype=F32)
        full = c4[:, :, :, None, :] * eye[None, :, None, :, None]
        return full.reshape(S5_CHUNKS, 8 * S5_P, 8 * S5_H)

    ccat = jnp.concatenate([c_blockdiag(s5_c_re[0]), -c_blockdiag(s5_c_im[0])],
                           axis=1).astype(BF16)
    mix_in = _mixer0(x_tm, ab_w_in[0].astype(BF16), bcat, a_re, a_im, ccat,
                     row(s5_d[0]), s5_w_glu[0].astype(BF16), pool_w[0].astype(BF16),
                     row(pool_scale[0]))
    x1, ri = _proj_ln_route(mix_in, ab_w_out[0].astype(BF16), x_tm,
                            row(ln_mix_g[0]), row(ln_mix_b[0]), rwh, rwl, rb)
    x2 = _moe_ffn(x1, ri, p_tm, 0, moe_w1, moe_w3, moe_w2,
                  ple_w_proj[0].astype(BF16), ple_w_gate[0].astype(BF16),
                  row(ln_ffn_g[0]), row(ln_ffn_b[0]))

    xb = x2.reshape(SEQ, BATCH, D_MODEL).transpose(1, 0, 2).reshape(TOKENS, D_MODEL)
    qkv = _qkv_proj(xb, sb_w_qkv[0].astype(BF16))
    att = _attention(qkv.reshape(BATCH, SEQ, 3 * D_MODEL)).reshape(TOKENS, D_MODEL)
    x3, ri = _proj_ln_route(att, sb_w_out[0].astype(BF16), xb,
                            row(ln_mix_g[1]), row(ln_mix_b[1]), rwh, rwl, rb)
    x4 = _moe_ffn(x3, ri, p[1].reshape(TOKENS, PLE_DIM), 1, moe_w1, moe_w3, moe_w2,
                  ple_w_proj[1].astype(BF16), ple_w_gate[1].astype(BF16),
                  row(ln_ffn_g[1]), row(ln_ffn_b[1]))
    return x4.reshape(BATCH, SEQ, D_MODEL)
```

```python
import functools
import math

import jax
import jax.numpy as jnp
from jax import lax
from jax.experimental import pallas as pl
from jax.experimental.pallas import tpu as pltpu

F32 = jnp.float32
BF16 = jnp.bfloat16
I32 = jnp.int32

D_MODEL = 1024
LANE_TILES = D_MODEL // 128
BATCH = 8
SEQ = 2048
DEPTH = 2
TOKENS = BATCH * SEQ

D_A = 512
S5_H = 16
S5_G = 32
S5_P = 64
S5_STATE = S5_G * S5_P
S5_CHUNKS = 4
POOL_WINDOWS = (2, 4, 8, 16)
POOL_C = 128
POOL_HALO = 16 * BATCH
SB_BLOCK = 256
ATT_PAIRS = 4
ATT_LANES = 128 * ATT_PAIRS
N_EXPERTS = 16
EXPERTS_PER_GROUP = 4
D_EXPERT = 512
PLE_DIM = 256
ALPHA = (2 * DEPTH) ** 0.25
LN_EPS = 1e-5

TS0 = 64
R0 = TS0 * BATCH
RP = 512
PROJ_SPLIT = 2
RR = 512
RD = 512
TM = 256
N_BUF = 3
PAIRS_PER_GROUP = 6
PAIR_A = (0, 2, 2, 3, 3, 3)
PAIR_B = (1, 1, 0, 0, 1, 2)
N_SEG = (N_EXPERTS // EXPERTS_PER_GROUP) * PAIRS_PER_GROUP
SEG_ROWS = 32
XW = D_MODEL + 128
N_SLOTS = TOKENS + N_SEG * TM
N_MTILES = N_SLOTS // TM
TE_LANES = 256

VMEM_LIMIT = 52 * 1024 * 1024
NT_DIMS = (((1,), (1,)), ((), ()))
LOG2E = 1.0 / math.log(2.0)
EXIT_LOG_WEIGHT = -110.0
BOUND_SLACK = 1.01


def _sigmoid(x):
    return 1.0 / (1.0 + jnp.exp(-x))


def _layer_norm(x, g, b):
    mu = jnp.mean(x, axis=-1, keepdims=True)
    xc = x - mu
    var = jnp.mean(xc * xc, axis=-1, keepdims=True)
    return xc * lax.rsqrt(var + LN_EPS) * g + b


def _s5_prep_kernel(lre_ref, lim_ref, ldt_ref, bre_ref, bim_ref,
                    are_ref, aim_ref, bcat_ref):
    lam_re = lre_ref[...]
    lam_im = lim_ref[...]
    dt = jnp.exp(ldt_ref[...])
    mag = jnp.exp(lam_re * dt)
    ang = lam_im * dt
    lb_re = mag * jnp.cos(ang)
    lb_im = mag * jnp.sin(ang)
    are_ref[...] = lb_re
    aim_ref[...] = lb_im
    den = lam_re * lam_re + lam_im * lam_im
    num_re = lb_re - 1.0
    f_re = (num_re * lam_re + lb_im * lam_im) / den
    f_im = (lb_im * lam_re - num_re * lam_im) / den
    for c in range(S5_CHUNKS):
        fr = f_re[:, c * 512:(c + 1) * 512]
        fi = f_im[:, c * 512:(c + 1) * 512]
        br = bre_ref[c]
        bi = bim_ref[c]
        bcat_ref[c, :, 0:512] = (fr * br - fi * bi).astype(BF16)
        bcat_ref[c, :, 512:1024] = (fr * bi + fi * br).astype(BF16)


def _s5_prep(lam_re, lam_im, log_dt, b_re, b_im):
    def blockdiag(b):
        b4 = b.reshape(S5_CHUNKS, 8, S5_P, S5_H).transpose(0, 1, 3, 2)
        eye = jnp.eye(8, dtype=F32)
        full = b4[:, :, :, None, :] * eye[None, :, None, :, None]
        return full.reshape(S5_CHUNKS, 8 * S5_H, 8 * S5_P)

    return pl.pallas_call(
        _s5_prep_kernel,
        out_shape=(jax.ShapeDtypeStruct((1, S5_STATE), F32),
                   jax.ShapeDtypeStruct((1, S5_STATE), F32),
                   jax.ShapeDtypeStruct((S5_CHUNKS, 128, 1024), BF16)),
        name="s5_prep",
    )(lam_re.reshape(1, S5_STATE), lam_im.reshape(1, S5_STATE),
      jnp.repeat(log_dt, S5_P).reshape(1, S5_STATE), blockdiag(b_re), blockdiag(b_im))


def _mixer0_kernel(x_ref, win_ref, bcat_ref, are_ref, aim_ref, ccat_ref, d_ref,
                   wglu_ref, poolw_ref, pscale_ref, o_ref, bus, hst, pe, hbm, htm, otm):
    i = pl.program_id(0)

    @pl.when(i == 0)
    def _():
        hst[...] = jnp.zeros_like(hst)
        pe[0:POOL_HALO, :] = jnp.zeros((POOL_HALO, D_A), F32)

    h_bm = jnp.dot(x_ref[...].reshape(R0, D_MODEL).astype(BF16), win_ref[...],
                   preferred_element_type=F32)
    for c in range(LANE_TILES):
        hbm[c] = h_bm[:, c * 128:(c + 1) * 128]
    for t in range(TS0):
        for c in range(LANE_TILES):
            htm[t * BATCH:(t + 1) * BATCH, c * 128:(c + 1) * 128] = (
                hbm[c, pl.ds(t, BATCH, stride=TS0), :])
    h = htm[...]
    u = h[:, :D_A]
    v = h[:, D_A:]
    ub = u.astype(BF16)

    for c in range(S5_CHUNKS):
        bus[:, c * 1024:(c + 1) * 1024] = jnp.dot(
            ub[:, c * 128:(c + 1) * 128], bcat_ref[c], preferred_element_type=F32)

    for c in range(S5_CHUNKS):
        re_cols = slice(c * 1024, c * 1024 + 512)
        im_cols = slice(c * 1024 + 512, (c + 1) * 1024)
        ar = jnp.broadcast_to(are_ref[:, c * 512:(c + 1) * 512], (BATCH, 512))
        ai = jnp.broadcast_to(aim_ref[:, c * 512:(c + 1) * 512], (BATCH, 512))

        def step(t, carry, re_cols=re_cols, im_cols=im_cols, ar=ar, ai=ai):
            hr, hi = carry
            rows = pl.ds(pl.multiple_of(t * BATCH, BATCH), BATCH)
            nr = ar * hr - ai * hi + bus[rows, re_cols]
            ni = ar * hi + ai * hr + bus[rows, im_cols]
            bus[rows, re_cols] = nr
            bus[rows, im_cols] = ni
            return nr, ni

        hr, hi = lax.fori_loop(0, TS0, step, (hst[:, re_cols], hst[:, im_cols]),
                               unroll=True)
        hst[:, re_cols] = hr
        hst[:, im_cols] = hi

    ys = [jnp.dot(bus[:, c * 1024:(c + 1) * 1024].astype(BF16), ccat_ref[c],
                  preferred_element_type=F32) for c in range(S5_CHUNKS)]
    y = jnp.concatenate(ys, axis=1) + d_ref[...] * u
    y = 0.5 * y * (1.0 + jnp.tanh(math.sqrt(2.0 / math.pi) * (y + 0.044715 * (y * y * y))))
    ga = y * _sigmoid(jnp.dot(y.astype(BF16), wglu_ref[...], preferred_element_type=F32))
    for c in range(D_A // 128):
        otm[c] = ga[:, c * 128:(c + 1) * 128]

    pe[POOL_HALO:, :] = v
    t_glob = lax.shift_right_logical(
        lax.broadcasted_iota(I32, (R0, 1), 0), int(math.log2(BATCH))) + i * TS0
    for gi, w in enumerate(POOL_WINDOWS):
        cols = slice(gi * POOL_C, (gi + 1) * POOL_C)
        s = pe[:, cols]
        off = BATCH
        while off < BATCH * w:
            s = s[off:] + s[:-off]
            off *= 2
        s = s[POOL_HALO - BATCH * (w - 1):]
        cnt = jnp.minimum(t_glob + 1, w).astype(F32)
        pooled = s / cnt - v[:, cols]
        mixed = jnp.dot(pooled.astype(BF16), poolw_ref[gi], preferred_element_type=F32)
        otm[D_A // 128 + gi] = mixed * pscale_ref[:, cols]
    pe[0:POOL_HALO, :] = pe[R0:R0 + POOL_HALO, :]
    for b in range(BATCH):
        for c in range(LANE_TILES):
            o_ref[b, :, c * 128:(c + 1) * 128] = (
                otm[c, pl.ds(b, TS0, stride=BATCH), :].astype(BF16))


def _mixer0(x, win, bcat, a_re, a_im, ccat, dskip, wglu, poolw, pscale):
    full = lambda shape: pl.BlockSpec(shape, lambda i: (0,) * len(shape))
    return pl.pallas_call(
        _mixer0_kernel,
        out_shape=jax.ShapeDtypeStruct((BATCH, SEQ, D_MODEL), BF16),
        grid=(SEQ // TS0,),
        in_specs=[
            pl.BlockSpec((BATCH, TS0, D_MODEL), lambda i: (0, i, 0)),
            full((D_MODEL, D_MODEL)),
            full((S5_CHUNKS, 128, 1024)),
            full((1, S5_STATE)),
            full((1, S5_STATE)),
            full((S5_CHUNKS, 1024, 128)),
            full((1, D_A)),
            full((D_A, D_A)),
            full((4, POOL_C, POOL_C)),
            full((1, D_A)),
        ],
        out_specs=pl.BlockSpec((BATCH, TS0, D_MODEL), lambda i: (0, i, 0)),
        scratch_shapes=[
            pltpu.VMEM((R0, 2 * S5_STATE), F32),
            pltpu.VMEM((BATCH, 2 * S5_STATE), F32),
            pltpu.VMEM((POOL_HALO + R0, D_A), F32),
            pltpu.VMEM((LANE_TILES, R0, 128), F32),
            pltpu.VMEM((R0, D_MODEL), F32),
            pltpu.VMEM((LANE_TILES, R0, 128), F32),
        ],
        compiler_params=pltpu.CompilerParams(
            dimension_semantics=("arbitrary",), vmem_limit_bytes=VMEM_LIMIT),
        name="mixer0",
    )(x, win, bcat, a_re, a_im, ccat, dskip, wglu, poolw, pscale)


def _route(x1, rwh_ref, rwl_ref, rb_ref):
    n = x1.shape[0]
    xh = x1.astype(BF16)
    xl = (x1 - xh.astype(F32)).astype(BF16)
    rwh = rwh_ref[...]
    logits = (lax.dot_general(rwh, xh, NT_DIMS, preferred_element_type=F32)
              + lax.dot_general(rwh, xl, NT_DIMS, preferred_element_type=F32)
              + lax.dot_general(rwl_ref[...], xh, NT_DIMS, preferred_element_type=F32))
    scores = _sigmoid(logits)
    sel = scores + rb_ref[...]
    row = lambda a, e: a[e:e + 1, :]

    best = None
    grp = None
    for g in range(N_EXPERTS // EXPERTS_PER_GROUP):
        m = [row(sel, EXPERTS_PER_GROUP * g + k) for k in range(EXPERTS_PER_GROUP)]
        gs = None
        for a in range(EXPERTS_PER_GROUP):
            for b in range(a + 1, EXPERTS_PER_GROUP):
                pair = m[a] + m[b]
                gs = pair if gs is None else jnp.maximum(gs, pair)
        if best is None:
            best, grp = gs, jnp.zeros(gs.shape, I32)
        else:
            better = gs > best
            grp = jnp.where(better, g, grp)
            best = jnp.where(better, gs, best)

    def pick(a, k):
        out = row(a, k)
        for g in range(1, N_EXPERTS // EXPERTS_PER_GROUP):
            out = jnp.where(grp == g, row(a, EXPERTS_PER_GROUP * g + k), out)
        return out

    cs = [pick(sel, k) for k in range(EXPERTS_PER_GROUP)]
    ss = [pick(scores, k) for k in range(EXPERTS_PER_GROUP)]
    m1, i1, s1 = cs[0], jnp.zeros(cs[0].shape, I32), ss[0]
    for k in range(1, EXPERTS_PER_GROUP):
        better = cs[k] > m1
        i1 = jnp.where(better, k, i1)
        s1 = jnp.where(better, ss[k], s1)
        m1 = jnp.where(better, cs[k], m1)
    m2 = jnp.full(m1.shape, -jnp.inf, F32)
    i2 = jnp.zeros(m1.shape, I32)
    s2 = jnp.zeros(m1.shape, F32)
    for k in range(EXPERTS_PER_GROUP):
        better = jnp.logical_and(i1 != k, cs[k] > m2)
        i2 = jnp.where(better, k, i2)
        s2 = jnp.where(better, ss[k], s2)
        m2 = jnp.where(better, cs[k], m2)
    tot = s1 + s2
    w0 = s1 / tot
    w1 = s2 / tot
    lo = jnp.minimum(i1, i2)
    hi = jnp.maximum(i1, i2)
    k = jnp.zeros(lo.shape, I32)
    in_a = jnp.zeros(lo.shape, I32)
    for kk, (a, b) in enumerate(zip(PAIR_A, PAIR_B)):
        hit = jnp.logical_and(lo == min(a, b), hi == max(a, b))
        k = jnp.where(hit, kk, k)
        in_a = jnp.where(hit, a, in_a)
    combo = grp * PAIRS_PER_GROUP + k
    first_in_a = i1 == in_a
    wa = jnp.where(first_in_a, w0, w1)
    wb = jnp.where(first_in_a, w1, w0)
    ri = jnp.concatenate([combo, jnp.zeros((7, n), I32)], axis=0)
    wslab = jnp.concatenate([wa, wb, jnp.zeros((126, n), F32)], axis=0)
    return ri, wslab.T


def _proj_ln_route_kernel(a_ref, w_ref, x_ref, g_ref, b_ref, rwh_ref, rwl_ref, rb_ref,
                          x1_ref, ri_ref):
    n = RP // PROJ_SPLIT
    parts = [slice(k * n, (k + 1) * n) for k in range(PROJ_SPLIT)]
    mixes = [jnp.dot(a_ref[rows, :], w_ref[...], preferred_element_type=F32)
             for rows in parts]
    for rows, mix in zip(parts, mixes):
        x1 = _layer_norm(ALPHA * x_ref[rows, :] + mix, g_ref[...], b_ref[...])
        x1_ref[rows, 0:D_MODEL] = x1
        ri, wcol = _route(x1, rwh_ref, rwl_ref, rb_ref)
        ri_ref[:, rows] = ri
        x1_ref[rows, D_MODEL:XW] = wcol


def _proj_ln_route(a, w, xres, g, b, rwh, rwl, rb):
    full = lambda shape: pl.BlockSpec(shape, lambda i: (0,) * len(shape))
    return pl.pallas_call(
        _proj_ln_route_kernel,
        out_shape=(jax.ShapeDtypeStruct((TOKENS, XW), F32),
                   jax.ShapeDtypeStruct((8, TOKENS), I32)),
        grid=(TOKENS // RP,),
        in_specs=[
            pl.BlockSpec((RP, D_MODEL), lambda i: (i, 0)),
            full((D_MODEL, D_MODEL)),
            pl.BlockSpec((RP, D_MODEL), lambda i: (i, 0)),
            full((1, D_MODEL)),
            full((1, D_MODEL)),
            full((N_EXPERTS, D_MODEL)),
            full((N_EXPERTS, D_MODEL)),
            full((N_EXPERTS, 1)),
        ],
        out_specs=(pl.BlockSpec((RP, XW), lambda i: (i, 0)),
                   pl.BlockSpec((8, RP), lambda i: (0, i))),
        compiler_params=pltpu.CompilerParams(
            dimension_semantics=("arbitrary",), vmem_limit_bytes=VMEM_LIMIT),
        name="proj_ln_route",
    )(a, w, xres, g, b, rwh, rwl, rb)


def _rank_kernel(ri_ref, pos_ref, te_ref, seg_ref, cnt, offs, tot):
    ph = pl.program_id(0)
    i = pl.program_id(1)
    n_tiles = pl.num_programs(1)
    s_iota = lax.broadcasted_iota(I32, (SEG_ROWS, RR), 0)
    hit = s_iota == ri_ref[0:1, :]
    mask = jnp.where(hit, 1.0, 0.0)
    tile_cnt = jnp.sum(mask, axis=1, keepdims=True)

    @pl.when(jnp.logical_and(ph == 0, i == 0))
    def _():
        cnt[...] = jnp.zeros_like(cnt)

    @pl.when(ph == 0)
    def _():
        cnt[...] = cnt[...] + tile_cnt
        pos_ref[...] = jnp.zeros_like(pos_ref)

    @pl.when(jnp.logical_and(ph == 0, i == n_tiles - 1))
    def _():
        c = cnt[...]
        padded = jnp.ceil(c * (1.0 / TM)) * TM
        sub = lax.broadcasted_iota(I32, (SEG_ROWS, 128), 0)
        acc = jnp.zeros((SEG_ROWS, 128), F32)
        for e in range(N_SEG):
            acc = acc + jnp.where(sub > e, padded[e:e + 1, :], 0.0)
        offs[...] = acc
        tot[...] = acc + padded
        cnt[...] = jnp.zeros_like(cnt)

    @pl.when(ph == 1)
    def _():
        r = lax.broadcasted_iota(I32, (RR, RR), 0)
        cidx = lax.broadcasted_iota(I32, (RR, RR), 1)
        tri = jnp.where(r < cidx, 1.0, 0.0).astype(BF16)
        prefix = jnp.dot(mask.astype(BF16), tri, preferred_element_type=F32)
        slot = prefix + cnt[:, 0:1] + offs[:, 0:1]
        p0 = jnp.sum(jnp.where(hit, slot, 0.0), axis=0, keepdims=True)
        pos_ref[...] = jnp.concatenate([p0.astype(I32), jnp.zeros((7, RR), I32)], axis=0)
        cnt[...] = cnt[...] + tile_cnt

    @pl.when(jnp.logical_and(ph == 1, i == n_tiles - 1))
    def _():
        ends = tot[:, 0:1]
        start = (lax.broadcasted_iota(I32, (SEG_ROWS, TE_LANES), 1) * TM).astype(F32)
        te = jnp.sum(jnp.where(start >= ends, 1.0, 0.0), axis=0, keepdims=True)
        te = jnp.minimum(te, N_SEG - 1.0).astype(I32)
        n_used = (tot[SEG_ROWS - 1:SEG_ROWS, :] * (1.0 / TM)).astype(I32)
        n_used = jnp.concatenate([n_used, n_used], axis=1)
        te_ref[...] = jnp.concatenate(
            [te, n_used, jnp.zeros((6, TE_LANES), I32)], axis=0)
        seg_ref[0:SEG_ROWS, :] = (offs[...] + cnt[...]).astype(I32)
        seg_ref[SEG_ROWS:2 * SEG_ROWS, :] = tot[...].astype(I32)


def _rank(ri):
    return pl.pallas_call(
        _rank_kernel,
        out_shape=(jax.ShapeDtypeStruct((8, TOKENS), I32),
                   jax.ShapeDtypeStruct((8, TE_LANES), I32),
                   jax.ShapeDtypeStruct((2 * SEG_ROWS, 128), I32)),
        grid=(2, TOKENS // RR),
        in_specs=[pl.BlockSpec((8, RR), lambda p, i: (0, i))],
        out_specs=(pl.BlockSpec((8, RR), lambda p, i: (0, i * p)),
                   pl.BlockSpec((8, TE_LANES), lambda p, i: (0, 0)),
                   pl.BlockSpec((2 * SEG_ROWS, 128), lambda p, i: (0, 0))),
        scratch_shapes=[pltpu.VMEM((SEG_ROWS, 128), F32),
                        pltpu.VMEM((SEG_ROWS, 128), F32),
                        pltpu.VMEM((SEG_ROWS, 128), F32)],
        compiler_params=pltpu.CompilerParams(
            dimension_semantics=("arbitrary", "arbitrary")),
        name="moe_rank",
    )(ri)


def _expert_kernel(tea_ref, teb_ref, nu_ref, pos_ref, seg_ref, x_hbm,
                   w1a_ref, w3a_ref, w2a_ref, w1b_ref, w3b_ref, w2b_ref, y_hbm,
                   wa1, wa3, wa2, wb1, wb3, wb2, xbuf, ybuf, dst, gsem, ssem):
    i = pl.program_id(0)
    n_used = nu_ref[0]
    b0 = lax.rem(i, N_BUF)
    b1 = lax.rem(i + 1, N_BUF)
    b2 = lax.rem(i + 2, N_BUF)

    def gather_row(slot, buf, r):
        tok = dst[slot] & (TOKENS - 1)
        return pltpu.make_async_copy(x_hbm.at[pl.ds(tok, 1)], xbuf.at[buf, pl.ds(r, 1)],
                                     gsem.at[buf])

    def scatter_row(slot, buf, r):
        return pltpu.make_async_copy(ybuf.at[buf, pl.ds(r, 1)], y_hbm.at[pl.ds(dst[slot], 1)],
                                     ssem.at[buf])

    def start_rows(row_copy, tile, buf):
        base = tile * TM
        for r in range(TM):
            row_copy(base + r, buf, r).start(priority=r % 2)

    def wait_rows(row_copy, buf):
        for r in range(TM):
            row_copy(0, buf, 0).wait()

    def compute(buf):
        xf = xbuf[buf]
        xb = xf[:, 0:D_MODEL].astype(BF16)
        h1a = jnp.dot(xb, wa1[...], preferred_element_type=F32)
        h3a = jnp.dot(xb, wa3[...], preferred_element_type=F32)
        h1b = jnp.dot(xb, wb1[...], preferred_element_type=F32)
        h3b = jnp.dot(xb, wb3[...], preferred_element_type=F32)
        ha = (h1a * _sigmoid(h1a)) * h3a * xf[:, D_MODEL:D_MODEL + 1]
        hb = (h1b * _sigmoid(h1b)) * h3b * xf[:, D_MODEL + 1:D_MODEL + 2]
        ybuf[buf] = (jnp.dot(ha.astype(BF16), wa2[...], preferred_element_type=F32)
                     + jnp.dot(hb.astype(BF16), wb2[...], preferred_element_type=F32))

    @pl.when(i == 0)
    def _():
        ybuf[1] = jnp.zeros((TM, D_MODEL), F32)
        for e in range(N_SEG):
            fill = pltpu.make_async_copy(
                ybuf.at[1], y_hbm.at[pl.ds(TOKENS + e * TM, TM)], ssem.at[1])
            fill.start()
            fill.wait()

        for e in range(N_SEG):
            def pad_body(s, carry, e=e):
                dst[s] = TOKENS + e * TM + (s - seg_ref[e])
                return carry
            lax.fori_loop(seg_ref[e], seg_ref[SEG_ROWS + e], pad_body, 0)

        def inv_body(t, carry):
            dst[pos_ref[t]] = t
            return carry
        lax.fori_loop(0, TOKENS, inv_body, 0, unroll=8)

        def first_start(r, carry):
            gather_row(r, 0, r).start()
            gather_row(TM + r, 1, r).start()
            return carry
        lax.fori_loop(0, TM, first_start, 0, unroll=8)

        def first_wait(r, carry):
            gather_row(0, 0, 0).wait()
            return carry
        lax.fori_loop(0, TM, first_wait, 0, unroll=8)

    prev = jnp.maximum(i - 1, 0)

    @pl.when(jnp.logical_and(i < n_used,
                             jnp.logical_or(i == 0, tea_ref[i] != tea_ref[prev])))
    def _():
        wa1[...] = w1a_ref[...].astype(BF16)
        wa3[...] = w3a_ref[...].astype(BF16)
        wa2[...] = w2a_ref[...].astype(BF16)

    @pl.when(jnp.logical_and(i < n_used,
                             jnp.logical_or(i == 0, teb_ref[i] != teb_ref[prev])))
    def _():
        wb1[...] = w1b_ref[...].astype(BF16)
        wb3[...] = w3b_ref[...].astype(BF16)
        wb2[...] = w2b_ref[...].astype(BF16)

    ahead = jnp.minimum(i + 2, n_used - 1)

    @pl.when(i == 0)
    def _():
        start_rows(gather_row, ahead, 2)
        compute(0)
        wait_rows(gather_row, 1)

    @pl.when(i == 1)
    def _():
        start_rows(gather_row, ahead, 0)
        start_rows(scatter_row, 0, 0)
        compute(1)
        wait_rows(gather_row, 2)

    @pl.when(jnp.logical_and(i >= 2, i < n_used))
    def _():
        start_rows(gather_row, ahead, b2)
        start_rows(scatter_row, i - 1, b2)
        compute(b0)
        wait_rows(gather_row, b1)
        wait_rows(scatter_row, b1)

    @pl.when(i == n_used)
    def _():
        def last_start(r, carry):
            scatter_row((i - 1) * TM + r, b2, r).start()
            return carry
        lax.fori_loop(0, TM, last_start, 0, unroll=8)
        wait_rows(gather_row, b1)
        wait_rows(scatter_row, b1)

    @pl.when(i == n_used + 1)
    def _():
        wait_rows(scatter_row, b1)


def _experts(tea, teb, nu, pos, seg, x1, w1, w3, w2, layer):
    def wspec(r, c, slot):
        return pl.BlockSpec(
            (None, None, r, c),
            lambda i, tea, teb, nu, pos, seg: (layer, (tea, teb)[slot][i], 0, 0))

    return pl.pallas_call(
        _expert_kernel,
        out_shape=jax.ShapeDtypeStruct((N_SLOTS, D_MODEL), F32),
        grid_spec=pltpu.PrefetchScalarGridSpec(
            num_scalar_prefetch=5,
            grid=(N_MTILES + N_BUF - 1,),
            in_specs=[pl.BlockSpec(memory_space=pl.ANY),
                      wspec(D_MODEL, D_EXPERT, 0), wspec(D_MODEL, D_EXPERT, 0),
                      wspec(D_EXPERT, D_MODEL, 0),
                      wspec(D_MODEL, D_EXPERT, 1), wspec(D_MODEL, D_EXPERT, 1),
                      wspec(D_EXPERT, D_MODEL, 1)],
            out_specs=pl.BlockSpec(memory_space=pl.ANY),
            scratch_shapes=[pltpu.VMEM((D_MODEL, D_EXPERT), BF16),
                            pltpu.VMEM((D_MODEL, D_EXPERT), BF16),
                            pltpu.VMEM((D_EXPERT, D_MODEL), BF16),
                            pltpu.VMEM((D_MODEL, D_EXPERT), BF16),
                            pltpu.VMEM((D_MODEL, D_EXPERT), BF16),
                            pltpu.VMEM((D_EXPERT, D_MODEL), BF16),
                            pltpu.VMEM((N_BUF, TM, XW), F32),
                            pltpu.VMEM((N_BUF, TM, D_MODEL), F32),
                            pltpu.SMEM((N_SLOTS,), I32),
                            pltpu.SemaphoreType.DMA((N_BUF,)),
                            pltpu.SemaphoreType.DMA((N_BUF,))],
        ),
        compiler_params=pltpu.CompilerParams(
            dimension_semantics=("arbitrary",), vmem_limit_bytes=VMEM_LIMIT,
            has_side_effects=True),
        name="moe_experts",
    )(tea, teb, nu, pos, seg, x1, w1, w3, w2, w1, w3, w2)


def _combine_kernel(x1_ref, p_ref, y_ref, wp_ref, wg_ref, g_ref, b_ref, o_ref):
    x1 = x1_ref[...]
    gate = _sigmoid(jnp.dot(x1.astype(BF16), wg_ref[...], preferred_element_type=F32))
    ple = jnp.dot(p_ref[...].astype(BF16), wp_ref[...], preferred_element_type=F32) * gate
    o_ref[...] = _layer_norm(ALPHA * x1 + y_ref[...] + ple, g_ref[...], b_ref[...])


def _combine(x1, p, ys, wp, wg, g, b):
    full = lambda shape: pl.BlockSpec(shape, lambda i: (0,) * len(shape))
    return pl.pallas_call(
        _combine_kernel,
        out_shape=jax.ShapeDtypeStruct((TOKENS, D_MODEL), F32),
        grid=(TOKENS // RD,),
        in_specs=[pl.BlockSpec((RD, D_MODEL), lambda i: (i, 0)),
                  pl.BlockSpec((RD, PLE_DIM), lambda i: (i, 0)),
                  pl.BlockSpec((RD, D_MODEL), lambda i: (i, 0)),
                  full((PLE_DIM, D_MODEL)),
                  full((D_MODEL, D_MODEL)),
                  full((1, D_MODEL)),
                  full((1, D_MODEL))],
        out_specs=pl.BlockSpec((RD, D_MODEL), lambda i: (i, 0)),
        compiler_params=pltpu.CompilerParams(
            dimension_semantics=("arbitrary",), vmem_limit_bytes=VMEM_LIMIT),
        name="moe_combine",
    )(x1, p, ys, wp, wg, g, b)


def _qkv_kernel(x_ref, w_ref, o_ref):
    o_ref[...] = jnp.dot(x_ref[...].astype(BF16), w_ref[...],
                         preferred_element_type=F32).astype(BF16)


def _qkv_proj(x, w):
    return pl.pallas_call(
        _qkv_kernel,
        out_shape=jax.ShapeDtypeStruct((TOKENS, 3 * D_MODEL), BF16),
        grid=(TOKENS // RP,),
        in_specs=[pl.BlockSpec((RP, D_MODEL), lambda i: (i, 0)),
                  pl.BlockSpec((D_MODEL, 3 * D_MODEL), lambda i: (0, 0))],
        out_specs=pl.BlockSpec((RP, 3 * D_MODEL), lambda i: (i, 0)),
        compiler_params=pltpu.CompilerParams(
            dimension_semantics=("arbitrary",), vmem_limit_bytes=VMEM_LIMIT),
        name="qkv_proj",
    )(x, w)


def _attn_kernel(q_ref, k_ref, v_ref, o_ref, acc, cbuf, bnd, kinf):
    qb = pl.program_id(2)
    lane = lax.broadcasted_iota(I32, (1, 128), 1)
    half = (lane < 64, lane >= 64)

    @pl.when(qb == 0)
    def _():
        for pr in range(ATT_PAIRS):
            ka = jnp.max(jnp.abs(k_ref[:, pr * 128:(pr + 1) * 128].astype(F32)),
                         axis=0, keepdims=True)
            for hh in range(2):
                m = jnp.max(jnp.where(half[hh], ka, 0.0), axis=1, keepdims=True)
                kinf[2 * pr + hh] = jnp.broadcast_to(m, (8, 128))

    qh = []
    for pr in range(ATT_PAIRS):
        q = q_ref[:, pr * 128:(pr + 1) * 128] * jnp.asarray(0.125, BF16)
        zero = jnp.zeros_like(q)
        qh.append((jnp.where(half[0], q, zero), jnp.where(half[1], q, zero)))
        qa = jnp.abs(q.astype(F32))
        for hh in range(2):
            qn = jnp.sum(jnp.where(half[hh], qa, 0.0), axis=1, keepdims=True)
            bnd[2 * pr + hh] = (jnp.broadcast_to(qn, (SB_BLOCK, 128))
                                * kinf[2 * pr + hh][0:1, :] * BOUND_SLACK)
    r = lax.broadcasted_iota(I32, (SB_BLOCK, SB_BLOCK), 0)
    cidx = lax.broadcasted_iota(I32, (SB_BLOCK, SB_BLOCK), 1)
    neg_suffix = jnp.where(r >= cidx, -1.0, 0.0).astype(BF16)
    causal = cidx < r

    acc[...] = jnp.zeros_like(acc)
    cbuf[...] = jnp.zeros_like(cbuf)

    def block(kb, masked):
        rows = pl.ds(pl.multiple_of(kb * SB_BLOCK, SB_BLOCK), SB_BLOCK)
        heads = [(pr, hh) for pr in range(ATT_PAIRS) for hh in range(2)]
        ks = [k_ref[rows, pr * 128:(pr + 1) * 128] for pr in range(ATT_PAIRS)]
        vs = [v_ref[rows, pr * 128:(pr + 1) * 128] for pr in range(ATT_PAIRS)]
        zs = [lax.dot_general(qh[pr][hh], ks[pr], NT_DIMS, preferred_element_type=F32)
              for pr, hh in heads]
        sps = []
        for z in zs:
            sp = jnp.maximum(z, 0.0) + jnp.log(1.0 + jnp.exp2(jnp.abs(z) * (-LOG2E)))
            if masked:
                sp = jnp.where(causal, sp, 0.0)
            sps.append(sp.astype(BF16))
        rss = []
        for idx, sp in enumerate(sps):
            c = cbuf[idx]
            rss.append(jnp.dot(sp, neg_suffix, preferred_element_type=F32)
                       + jnp.concatenate([c, c], axis=1))
        for idx, (pr, hh) in enumerate(heads):
            w = jnp.exp(zs[idx] + rss[idx])
            if masked:
                w = jnp.where(causal, w, 0.0)
            acc[idx] = acc[idx] + jnp.dot(w.astype(BF16), vs[pr], preferred_element_type=F32)
            cbuf[idx] = jnp.broadcast_to(rss[idx][:, 0:1], (SB_BLOCK, 128))

    def log_weight_bound():
        m = cbuf[0] + bnd[0]
        for idx in range(1, 2 * ATT_PAIRS):
            m = jnp.maximum(m, cbuf[idx] + bnd[idx])
        return jnp.max(m)

    block(qb, True)

    def cond(carry):
        j, m = carry
        return jnp.logical_and(j < qb, m > EXIT_LOG_WEIGHT)

    def body(carry):
        j, _ = carry
        block(qb - 1 - j, False)
        return j + 1, log_weight_bound()

    lax.while_loop(cond, body, (jnp.int32(0), log_weight_bound()))
    for pr in range(ATT_PAIRS):
        o_ref[:, pr * 128:(pr + 1) * 128] = jnp.where(
            lane < 64, acc[2 * pr], acc[2 * pr + 1]).astype(BF16)


def _attention(qkv):
    n_hg = D_MODEL // ATT_LANES
    return pl.pallas_call(
        _attn_kernel,
        out_shape=jax.ShapeDtypeStruct((BATCH, SEQ, D_MODEL), BF16),
        grid=(BATCH, n_hg, SEQ // SB_BLOCK),
        in_specs=[pl.BlockSpec((None, SB_BLOCK, ATT_LANES), lambda b, h, i: (b, i, h)),
                  pl.BlockSpec((None, SEQ, ATT_LANES), lambda b, h, i: (b, 0, n_hg + h)),
                  pl.BlockSpec((None, SEQ, ATT_LANES), lambda b, h, i: (b, 0, 2 * n_hg + h))],
        out_specs=pl.BlockSpec((None, SB_BLOCK, ATT_LANES), lambda b, h, i: (b, i, h)),
        scratch_shapes=[pltpu.VMEM((2 * ATT_PAIRS, SB_BLOCK, 128), F32),
                        pltpu.VMEM((2 * ATT_PAIRS, SB_BLOCK, 128), F32),
                        pltpu.VMEM((2 * ATT_PAIRS, SB_BLOCK, 128), F32),
                        pltpu.VMEM((2 * ATT_PAIRS, 8, 128), F32)],
        compiler_params=pltpu.CompilerParams(
            dimension_semantics=("arbitrary", "arbitrary", "arbitrary"),
            vmem_limit_bytes=VMEM_LIMIT),
        name="sb_attention",
    )(qkv, qkv, qkv)


def _moe_ffn(x1, ri, p_rows, layer, moe_w1, moe_w3, moe_w2, wp, wg, g, b):
    pos, te, seg = _rank(ri)
    tile_seg = te[0, :N_MTILES + N_BUF - 1]
    grp = tile_seg // PAIRS_PER_GROUP
    k = tile_seg % PAIRS_PER_GROUP
    tea = grp * EXPERTS_PER_GROUP + sum(jnp.where(k == kk, a, 0) for kk, a in enumerate(PAIR_A))
    teb = grp * EXPERTS_PER_GROUP + sum(jnp.where(k == kk, b_, 0) for kk, b_ in enumerate(PAIR_B))
    ys = _experts(tea, teb, te[1, :1], pos[0], seg[:, 0], x1, moe_w1, moe_w3, moe_w2, layer)
    return _combine(x1, p_rows, ys, wp, wg, g, b)


def kernel(x, p, ab_w_in, s5_lambda_re, s5_lambda_im, s5_log_dt, s5_b_re, s5_b_im, s5_c_re, s5_c_im, s5_d, s5_w_glu, pool_w, pool_scale, ab_w_out, sb_w_qkv, sb_w_out, ln_mix_g, ln_mix_b, ln_ffn_g, ln_ffn_b, router_w, router_bias, moe_w1, moe_w3, moe_w2, ple_w_proj, ple_w_gate):
    row = lambda a: a.reshape(1, -1)
    rwt = router_w.T
    rwh = rwt.astype(BF16)
    rwl = (rwt - rwh.astype(F32)).astype(BF16)
    rb = router_bias.reshape(N_EXPERTS, 1)

    x_rows = x.reshape(TOKENS, D_MODEL)
    a_re, a_im, bcat = _s5_prep(s5_lambda_re[0], s5_lambda_im[0], s5_log_dt[0],
                                s5_b_re[0], s5_b_im[0])

    def c_blockdiag(c):
        c4 = c.reshape(S5_CHUNKS, 8, S5_H, S5_P).transpose(0, 1, 3, 2)
        eye = jnp.eye(8, dtype=F32)
        full = c4[:, :, :, None, :] * eye[None, :, None, :, None]
        return full.reshape(S5_CHUNKS, 8 * S5_P, 8 * S5_H)

    ccat = jnp.concatenate([c_blockdiag(s5_c_re[0]), -c_blockdiag(s5_c_im[0])],
                           axis=1).astype(BF16)
    mix_in = _mixer0(x, ab_w_in[0].astype(BF16), bcat, a_re, a_im, ccat,
                     row(s5_d[0]), s5_w_glu[0].astype(BF16), pool_w[0].astype(BF16),
                     row(pool_scale[0])).reshape(TOKENS, D_MODEL)
    x1, ri = _proj_ln_route(mix_in, ab_w_out[0].astype(BF16), x_rows,
                            row(ln_mix_g[0]), row(ln_mix_b[0]), rwh, rwl, rb)
    xb = _moe_ffn(x1, ri, p[0].reshape(TOKENS, PLE_DIM), 0, moe_w1, moe_w3, moe_w2,
                  ple_w_proj[0].astype(BF16), ple_w_gate[0].astype(BF16),
                  row(ln_ffn_g[0]), row(ln_ffn_b[0]))

    qkv = _qkv_proj(xb, sb_w_qkv[0].astype(BF16))
    att = _attention(qkv.reshape(BATCH, SEQ, 3 * D_MODEL)).reshape(TOKENS, D_MODEL)
    x3, ri = _proj_ln_route(att, sb_w_out[0].astype(BF16), xb,
                            row(ln_mix_g[1]), row(ln_mix_b[1]), rwh, rwl, rb)
    x4 = _moe_ffn(x3, ri, p[1].reshape(TOKENS, PLE_DIM), 1, moe_w1, moe_w3, moe_w2,
                  ple_w_proj[1].astype(BF16), ple_w_gate[1].astype(BF16),
                  row(ln_ffn_g[1]), row(ln_ffn_b[1]))
    return x4.reshape(BATCH, SEQ, D_MODEL)
```

```python
import functools
import math

import jax
import jax.numpy as jnp
from jax import lax
from jax.experimental import pallas as pl
from jax.experimental.pallas import tpu as pltpu

F32 = jnp.float32
BF16 = jnp.bfloat16
I32 = jnp.int32

D_MODEL = 1024
LANE_TILES = D_MODEL // 128
BATCH = 8
SEQ = 2048
DEPTH = 2
TOKENS = BATCH * SEQ

D_A = 512
S5_H = 16
S5_G = 32
S5_P = 64
S5_STATE = S5_G * S5_P
S5_CHUNKS = 4
POOL_WINDOWS = (2, 4, 8, 16)
POOL_C = 128
POOL_HALO = 16 * BATCH
SB_BLOCK = 256
ATT_PAIRS = 4
ATT_LANES = 128 * ATT_PAIRS
N_EXPERTS = 16
EXPERTS_PER_GROUP = 4
D_EXPERT = 512
PLE_DIM = 256
ALPHA = (2 * DEPTH) ** 0.25
LN_EPS = 1e-5

TS0 = 64
R0 = TS0 * BATCH
RP = 512
PROJ_SPLIT = 2
RR = 1024
RD = 512
TM = 256
N_BUF = 3
ISSUE_STAGES = 8
PAIRS_PER_GROUP = 6
PAIR_A = (0, 2, 2, 3, 3, 3)
PAIR_B = (1, 1, 0, 0, 1, 2)
N_SEG = (N_EXPERTS // EXPERTS_PER_GROUP) * PAIRS_PER_GROUP
SEG_ROWS = 32
XW = D_MODEL + 128
N_SLOTS = TOKENS + N_SEG * TM
N_MTILES = N_SLOTS // TM
TE_LANES = 256

VMEM_LIMIT = 52 * 1024 * 1024
NT_DIMS = (((1,), (1,)), ((), ()))
LOG2E = 1.0 / math.log(2.0)
EXIT_LOG_WEIGHT = -110.0
BOUND_SLACK = 1.01


def _sigmoid(x):
    return 1.0 / (1.0 + jnp.exp(-x))


def _layer_norm(x, g, b):
    mu = jnp.mean(x, axis=-1, keepdims=True)
    xc = x - mu
    var = jnp.mean(xc * xc, axis=-1, keepdims=True)
    return xc * lax.rsqrt(var + LN_EPS) * g + b


def _s5_prep_kernel(lre_ref, lim_ref, ldt_ref, bre_ref, bim_ref,
                    are_ref, aim_ref, bcat_ref):
    lam_re = lre_ref[...]
    lam_im = lim_ref[...]
    dt = jnp.exp(ldt_ref[...])
    mag = jnp.exp(lam_re * dt)
    ang = lam_im * dt
    lb_re = mag * jnp.cos(ang)
    lb_im = mag * jnp.sin(ang)
    are_ref[...] = lb_re
    aim_ref[...] = lb_im
    den = lam_re * lam_re + lam_im * lam_im
    num_re = lb_re - 1.0
    f_re = (num_re * lam_re + lb_im * lam_im) / den
    f_im = (lb_im * lam_re - num_re * lam_im) / den
    for c in range(S5_CHUNKS):
        fr = f_re[:, c * 512:(c + 1) * 512]
        fi = f_im[:, c * 512:(c + 1) * 512]
        br = bre_ref[c]
        bi = bim_ref[c]
        bcat_ref[c, :, 0:512] = (fr * br - fi * bi).astype(BF16)
        bcat_ref[c, :, 512:1024] = (fr * bi + fi * br).astype(BF16)


def _s5_prep(lam_re, lam_im, log_dt, b_re, b_im):
    def blockdiag(b):
        b4 = b.reshape(S5_CHUNKS, 8, S5_P, S5_H).transpose(0, 1, 3, 2)
        eye = jnp.eye(8, dtype=F32)
        full = b4[:, :, :, None, :] * eye[None, :, None, :, None]
        return full.reshape(S5_CHUNKS, 8 * S5_H, 8 * S5_P)

    return pl.pallas_call(
        _s5_prep_kernel,
        out_shape=(jax.ShapeDtypeStruct((1, S5_STATE), F32),
                   jax.ShapeDtypeStruct((1, S5_STATE), F32),
                   jax.ShapeDtypeStruct((S5_CHUNKS, 128, 1024), BF16)),
        name="s5_prep",
    )(lam_re.reshape(1, S5_STATE), lam_im.reshape(1, S5_STATE),
      jnp.repeat(log_dt, S5_P).reshape(1, S5_STATE), blockdiag(b_re), blockdiag(b_im))


def _mixer0_kernel(x_ref, win_ref, bcat_ref, are_ref, aim_ref, ccat_ref, d_ref,
                   wglu_ref, poolw_ref, pscale_ref, o_ref, bus, hst, pe, hbm, htm, otm):
    i = pl.program_id(0)

    @pl.when(i == 0)
    def _():
        hst[...] = jnp.zeros_like(hst)
        pe[0:POOL_HALO, :] = jnp.zeros((POOL_HALO, D_A), F32)

    h_bm = jnp.dot(x_ref[...].reshape(R0, D_MODEL).astype(BF16), win_ref[...],
                   preferred_element_type=F32)
    for c in range(LANE_TILES):
        hbm[c] = h_bm[:, c * 128:(c + 1) * 128]
    for t in range(TS0):
        for c in range(LANE_TILES):
            htm[t * BATCH:(t + 1) * BATCH, c * 128:(c + 1) * 128] = (
                hbm[c, pl.ds(t, BATCH, stride=TS0), :])
    h = htm[...]
    u = h[:, :D_A]
    v = h[:, D_A:]
    ub = u.astype(BF16)

    for c in range(S5_CHUNKS):
        bus[:, c * 1024:(c + 1) * 1024] = jnp.dot(
            ub[:, c * 128:(c + 1) * 128], bcat_ref[c], preferred_element_type=F32)

    for c in range(S5_CHUNKS):
        re_cols = slice(c * 1024, c * 1024 + 512)
        im_cols = slice(c * 1024 + 512, (c + 1) * 1024)
        ar = jnp.broadcast_to(are_ref[:, c * 512:(c + 1) * 512], (BATCH, 512))
        ai = jnp.broadcast_to(aim_ref[:, c * 512:(c + 1) * 512], (BATCH, 512))

        def step(t, carry, re_cols=re_cols, im_cols=im_cols, ar=ar, ai=ai):
            hr, hi = carry
            rows = pl.ds(pl.multiple_of(t * BATCH, BATCH), BATCH)
            nr = ar * hr - ai * hi + bus[rows, re_cols]
            ni = ar * hi + ai * hr + bus[rows, im_cols]
            bus[rows, re_cols] = nr
            bus[rows, im_cols] = ni
            return nr, ni

        hr, hi = lax.fori_loop(0, TS0, step, (hst[:, re_cols], hst[:, im_cols]),
                               unroll=True)
        hst[:, re_cols] = hr
        hst[:, im_cols] = hi

    ys = [jnp.dot(bus[:, c * 1024:(c + 1) * 1024].astype(BF16), ccat_ref[c],
                  preferred_element_type=F32) for c in range(S5_CHUNKS)]
    y = jnp.concatenate(ys, axis=1) + d_ref[...] * u
    y = 0.5 * y * (1.0 + jnp.tanh(math.sqrt(2.0 / math.pi) * (y + 0.044715 * (y * y * y))))
    ga = y * _sigmoid(jnp.dot(y.astype(BF16), wglu_ref[...], preferred_element_type=F32))
    for c in range(D_A // 128):
        otm[c] = ga[:, c * 128:(c + 1) * 128]

    pe[POOL_HALO:, :] = v
    t_glob = lax.shift_right_logical(
        lax.broadcasted_iota(I32, (R0, 1), 0), int(math.log2(BATCH))) + i * TS0
    for gi, w in enumerate(POOL_WINDOWS):
        cols = slice(gi * POOL_C, (gi + 1) * POOL_C)
        s = pe[:, cols]
        off = BATCH
        while off < BATCH * w:
            s = s[off:] + s[:-off]
            off *= 2
        s = s[POOL_HALO - BATCH * (w - 1):]
        cnt = jnp.minimum(t_glob + 1, w).astype(F32)
        pooled = s / cnt - v[:, cols]
        mixed = jnp.dot(pooled.astype(BF16), poolw_ref[gi], preferred_element_type=F32)
        otm[D_A // 128 + gi] = mixed * pscale_ref[:, cols]
    pe[0:POOL_HALO, :] = pe[R0:R0 + POOL_HALO, :]
    for b in range(BATCH):
        for c in range(LANE_TILES):
            o_ref[b, :, c * 128:(c + 1) * 128] = (
                otm[c, pl.ds(b, TS0, stride=BATCH), :].astype(BF16))


def _mixer0(x, win, bcat, a_re, a_im, ccat, dskip, wglu, poolw, pscale):
    full = lambda shape: pl.BlockSpec(shape, lambda i: (0,) * len(shape))
    return pl.pallas_call(
        _mixer0_kernel,
        out_shape=jax.ShapeDtypeStruct((BATCH, SEQ, D_MODEL), BF16),
        grid=(SEQ // TS0,),
        in_specs=[
            pl.BlockSpec((BATCH, TS0, D_MODEL), lambda i: (0, i, 0)),
            full((D_MODEL, D_MODEL)),
            full((S5_CHUNKS, 128, 1024)),
            full((1, S5_STATE)),
            full((1, S5_STATE)),
            full((S5_CHUNKS, 1024, 128)),
            full((1, D_A)),
            full((D_A, D_A)),
            full((4, POOL_C, POOL_C)),
            full((1, D_A)),
        ],
        out_specs=pl.BlockSpec((BATCH, TS0, D_MODEL), lambda i: (0, i, 0)),
        scratch_shapes=[
            pltpu.VMEM((R0, 2 * S5_STATE), F32),
            pltpu.VMEM((BATCH, 2 * S5_STATE), F32),
            pltpu.VMEM((POOL_HALO + R0, D_A), F32),
            pltpu.VMEM((LANE_TILES, R0, 128), F32),
            pltpu.VMEM((R0, D_MODEL), F32),
            pltpu.VMEM((LANE_TILES, R0, 128), F32),
        ],
        compiler_params=pltpu.CompilerParams(
            dimension_semantics=("arbitrary",), vmem_limit_bytes=VMEM_LIMIT),
        name="mixer0",
    )(x, win, bcat, a_re, a_im, ccat, dskip, wglu, poolw, pscale)


def _route(x1, rwh_ref, rwl_ref, rb_ref):
    n = x1.shape[0]
    xh = x1.astype(BF16)
    xl = (x1 - xh.astype(F32)).astype(BF16)
    rwh = rwh_ref[...]
    logits = (lax.dot_general(rwh, xh, NT_DIMS, preferred_element_type=F32)
              + lax.dot_general(rwh, xl, NT_DIMS, preferred_element_type=F32)
              + lax.dot_general(rwl_ref[...], xh, NT_DIMS, preferred_element_type=F32))
    scores = _sigmoid(logits)
    sel = scores + rb_ref[...]
    row = lambda a, e: a[e:e + 1, :]

    best = None
    grp = None
    for g in range(N_EXPERTS // EXPERTS_PER_GROUP):
        m = [row(sel, EXPERTS_PER_GROUP * g + k) for k in range(EXPERTS_PER_GROUP)]
        gs = None
        for a in range(EXPERTS_PER_GROUP):
            for b in range(a + 1, EXPERTS_PER_GROUP):
                pair = m[a] + m[b]
                gs = pair if gs is None else jnp.maximum(gs, pair)
        if best is None:
            best, grp = gs, jnp.zeros(gs.shape, I32)
        else:
            better = gs > best
            grp = jnp.where(better, g, grp)
            best = jnp.where(better, gs, best)

    def pick(a, k):
        out = row(a, k)
        for g in range(1, N_EXPERTS // EXPERTS_PER_GROUP):
            out = jnp.where(grp == g, row(a, EXPERTS_PER_GROUP * g + k), out)
        return out

    cs = [pick(sel, k) for k in range(EXPERTS_PER_GROUP)]
    ss = [pick(scores, k) for k in range(EXPERTS_PER_GROUP)]
    m1, i1, s1 = cs[0], jnp.zeros(cs[0].shape, I32), ss[0]
    for k in range(1, EXPERTS_PER_GROUP):
        better = cs[k] > m1
        i1 = jnp.where(better, k, i1)
        s1 = jnp.where(better, ss[k], s1)
        m1 = jnp.where(better, cs[k], m1)
    m2 = jnp.full(m1.shape, -jnp.inf, F32)
    i2 = jnp.zeros(m1.shape, I32)
    s2 = jnp.zeros(m1.shape, F32)
    for k in range(EXPERTS_PER_GROUP):
        better = jnp.logical_and(i1 != k, cs[k] > m2)
        i2 = jnp.where(better, k, i2)
        s2 = jnp.where(better, ss[k], s2)
        m2 = jnp.where(better, cs[k], m2)
    tot = s1 + s2
    w0 = s1 / tot
    w1 = s2 / tot
    lo = jnp.minimum(i1, i2)
    hi = jnp.maximum(i1, i2)
    k = jnp.zeros(lo.shape, I32)
    in_a = jnp.zeros(lo.shape, I32)
    for kk, (a, b) in enumerate(zip(PAIR_A, PAIR_B)):
        hit = jnp.logical_and(lo == min(a, b), hi == max(a, b))
        k = jnp.where(hit, kk, k)
        in_a = jnp.where(hit, a, in_a)
    combo = grp * PAIRS_PER_GROUP + k
    first_in_a = i1 == in_a
    wa = jnp.where(first_in_a, w0, w1)
    wb = jnp.where(first_in_a, w1, w0)
    ri = jnp.concatenate([combo, jnp.zeros((7, n), I32)], axis=0)
    wslab = jnp.concatenate([wa, wb, jnp.zeros((126, n), F32)], axis=0)
    return ri, wslab.T


def _proj_ln_route_kernel(a_ref, w_ref, x_ref, g_ref, b_ref, rwh_ref, rwl_ref, rb_ref,
                          x1_ref, ri_ref):
    n = RP // PROJ_SPLIT
    parts = [slice(k * n, (k + 1) * n) for k in range(PROJ_SPLIT)]
    mixes = [jnp.dot(a_ref[rows, :], w_ref[...], preferred_element_type=F32)
             for rows in parts]
    for rows, mix in zip(parts, mixes):
        x1 = _layer_norm(ALPHA * x_ref[rows, :] + mix, g_ref[...], b_ref[...])
        x1_ref[rows, 0:D_MODEL] = x1
        ri, wcol = _route(x1, rwh_ref, rwl_ref, rb_ref)
        ri_ref[:, rows] = ri
        x1_ref[rows, D_MODEL:XW] = wcol


def _proj_ln_route(a, w, xres, g, b, rwh, rwl, rb):
    full = lambda shape: pl.BlockSpec(shape, lambda i: (0,) * len(shape))
    return pl.pallas_call(
        _proj_ln_route_kernel,
        out_shape=(jax.ShapeDtypeStruct((TOKENS, XW), F32),
                   jax.ShapeDtypeStruct((8, TOKENS), I32)),
        grid=(TOKENS // RP,),
        in_specs=[
            pl.BlockSpec((RP, D_MODEL), lambda i: (i, 0)),
            full((D_MODEL, D_MODEL)),
            pl.BlockSpec((RP, D_MODEL), lambda i: (i, 0)),
            full((1, D_MODEL)),
            full((1, D_MODEL)),
            full((N_EXPERTS, D_MODEL)),
            full((N_EXPERTS, D_MODEL)),
            full((N_EXPERTS, 1)),
        ],
        out_specs=(pl.BlockSpec((RP, XW), lambda i: (i, 0)),
                   pl.BlockSpec((8, RP), lambda i: (0, i))),
        compiler_params=pltpu.CompilerParams(
            dimension_semantics=("arbitrary",), vmem_limit_bytes=VMEM_LIMIT),
        name="proj_ln_route",
    )(a, w, xres, g, b, rwh, rwl, rb)


def _rank_kernel(ri_ref, pos_ref, te_ref, seg_ref, cnt, offs, tot):
    ph = pl.program_id(0)
    i = pl.program_id(1)
    n_tiles = pl.num_programs(1)
    s_iota = lax.broadcasted_iota(I32, (SEG_ROWS, RR), 0)
    hit = s_iota == ri_ref[0:1, :]
    mask = jnp.where(hit, 1.0, 0.0)
    tile_cnt = jnp.sum(mask, axis=1, keepdims=True)

    @pl.when(jnp.logical_and(ph == 0, i == 0))
    def _():
        cnt[...] = jnp.zeros_like(cnt)

    @pl.when(ph == 0)
    def _():
        cnt[...] = cnt[...] + tile_cnt
        pos_ref[...] = jnp.zeros_like(pos_ref)

    @pl.when(jnp.logical_and(ph == 0, i == n_tiles - 1))
    def _():
        c = cnt[...]
        padded = jnp.ceil(c * (1.0 / TM)) * TM
        sub = lax.broadcasted_iota(I32, (SEG_ROWS, 128), 0)
        acc = jnp.zeros((SEG_ROWS, 128), F32)
        for e in range(N_SEG):
            acc = acc + jnp.where(sub > e, padded[e:e + 1, :], 0.0)
        offs[...] = acc
        tot[...] = acc + padded
        cnt[...] = jnp.zeros_like(cnt)

    @pl.when(ph == 1)
    def _():
        r = lax.broadcasted_iota(I32, (RR, RR), 0)
        cidx = lax.broadcasted_iota(I32, (RR, RR), 1)
        tri = jnp.where(r < cidx, 1.0, 0.0).astype(BF16)
        prefix = jnp.dot(mask.astype(BF16), tri, preferred_element_type=F32)
        slot = prefix + cnt[:, 0:1] + offs[:, 0:1]
        p0 = jnp.sum(jnp.where(hit, slot, 0.0), axis=0, keepdims=True)
        pos_ref[...] = jnp.concatenate([p0.astype(I32), jnp.zeros((7, RR), I32)], axis=0)
        cnt[...] = cnt[...] + tile_cnt

    @pl.when(jnp.logical_and(ph == 1, i == n_tiles - 1))
    def _():
        ends = tot[:, 0:1]
        start = (lax.broadcasted_iota(I32, (SEG_ROWS, TE_LANES), 1) * TM).astype(F32)
        te = jnp.sum(jnp.where(start >= ends, 1.0, 0.0), axis=0, keepdims=True)
        te = jnp.minimum(te, N_SEG - 1.0).astype(I32)
        n_used = (tot[SEG_ROWS - 1:SEG_ROWS, :] * (1.0 / TM)).astype(I32)
        n_used = jnp.concatenate([n_used, n_used], axis=1)
        te_ref[...] = jnp.concatenate(
            [te, n_used, jnp.zeros((6, TE_LANES), I32)], axis=0)
        seg_ref[0:SEG_ROWS, :] = (offs[...] + cnt[...]).astype(I32)
        seg_ref[SEG_ROWS:2 * SEG_ROWS, :] = tot[...].astype(I32)


def _rank(ri):
    return pl.pallas_call(
        _rank_kernel,
        out_shape=(jax.ShapeDtypeStruct((8, TOKENS), I32),
                   jax.ShapeDtypeStruct((8, TE_LANES), I32),
                   jax.ShapeDtypeStruct((2 * SEG_ROWS, 128), I32)),
        grid=(2, TOKENS // RR),
        in_specs=[pl.BlockSpec((8, RR), lambda p, i: (0, i))],
        out_specs=(pl.BlockSpec((8, RR), lambda p, i: (0, i * p)),
                   pl.BlockSpec((8, TE_LANES), lambda p, i: (0, 0)),
                   pl.BlockSpec((2 * SEG_ROWS, 128), lambda p, i: (0, 0))),
        scratch_shapes=[pltpu.VMEM((SEG_ROWS, 128), F32),
                        pltpu.VMEM((SEG_ROWS, 128), F32),
                        pltpu.VMEM((SEG_ROWS, 128), F32)],
        compiler_params=pltpu.CompilerParams(
            dimension_semantics=("arbitrary", "arbitrary")),
        name="moe_rank",
    )(ri)


def _expert_kernel(tea_ref, teb_ref, nu_ref, pos_ref, seg_ref, x_hbm,
                   w1a_ref, w3a_ref, w2a_ref, w1b_ref, w3b_ref, w2b_ref, y_hbm,
                   wa1, wa3, wa2, wb1, wb3, wb2, xbuf, ybuf, dst, gsem, ssem):
    i = pl.program_id(0)
    n_used = nu_ref[0]
    b0 = lax.rem(i, N_BUF)
    b1 = lax.rem(i + 1, N_BUF)
    b2 = lax.rem(i + 2, N_BUF)

    def gather_row(slot, buf, r):
        tok = dst[slot] & (TOKENS - 1)
        return pltpu.make_async_copy(x_hbm.at[pl.ds(tok, 1)], xbuf.at[buf, pl.ds(r, 1)],
                                     gsem.at[buf])

    def scatter_row(slot, buf, r):
        return pltpu.make_async_copy(ybuf.at[buf, pl.ds(r, 1)], y_hbm.at[pl.ds(dst[slot], 1)],
                                     ssem.at[buf])

    def wait_rows(row_copy, buf):
        for r in range(TM):
            row_copy(0, buf, 0).wait()

    def compute(buf, copies):
        def issue(stage):
            lo, hi = stage * TM // ISSUE_STAGES, (stage + 1) * TM // ISSUE_STAGES
            for row_copy, tile, cbuf in copies:
                for r in range(lo, hi):
                    row_copy(tile * TM + r, cbuf, r).start(priority=r % 2)

        xf = xbuf[buf]
        xb = xf[:, 0:D_MODEL].astype(BF16)
        issue(0)
        h1a = jnp.dot(xb, wa1[...], preferred_element_type=F32)
        issue(1)
        h3a = jnp.dot(xb, wa3[...], preferred_element_type=F32)
        issue(2)
        h1b = jnp.dot(xb, wb1[...], preferred_element_type=F32)
        issue(3)
        h3b = jnp.dot(xb, wb3[...], preferred_element_type=F32)
        issue(4)
        ha = ((h1a * _sigmoid(h1a)) * h3a * xf[:, D_MODEL:D_MODEL + 1]).astype(BF16)
        hb = ((h1b * _sigmoid(h1b)) * h3b * xf[:, D_MODEL + 1:D_MODEL + 2]).astype(BF16)
        issue(5)
        ya = jnp.dot(ha, wa2[...], preferred_element_type=F32)
        issue(6)
        ybuf[buf] = ya + jnp.dot(hb, wb2[...], preferred_element_type=F32)
        issue(7)

    @pl.when(i == 0)
    def _():
        ybuf[1] = jnp.zeros((TM, D_MODEL), F32)
        fills = [pltpu.make_async_copy(
            ybuf.at[1], y_hbm.at[pl.ds(TOKENS + e * TM, TM)], ssem.at[1])
            for e in range(N_SEG)]
        for fill in fills:
            fill.start()
        for fill in fills:
            fill.wait()

        for e in range(N_SEG):
            def pad_body(s, carry, e=e):
                dst[s] = TOKENS + e * TM + (s - seg_ref[e])
                return carry
            lax.fori_loop(seg_ref[e], seg_ref[SEG_ROWS + e], pad_body, 0)

        def inv_body(t, carry):
            dst[pos_ref[t]] = t
            return carry
        lax.fori_loop(0, TOKENS, inv_body, 0, unroll=8)

        def first_start(r, carry):
            gather_row(r, 0, r).start()
            gather_row(TM + r, 1, r).start()
            return carry
        lax.fori_loop(0, TM, first_start, 0, unroll=8)

        def first_wait(r, carry):
            gather_row(0, 0, 0).wait()
            return carry
        lax.fori_loop(0, TM, first_wait, 0, unroll=8)

    prev = jnp.maximum(i - 1, 0)

    @pl.when(jnp.logical_and(i < n_used,
                             jnp.logical_or(i == 0, tea_ref[i] != tea_ref[prev])))
    def _():
        wa1[...] = w1a_ref[...].astype(BF16)
        wa3[...] = w3a_ref[...].astype(BF16)
        wa2[...] = w2a_ref[...].astype(BF16)

    @pl.when(jnp.logical_and(i < n_used,
                             jnp.logical_or(i == 0, teb_ref[i] != teb_ref[prev])))
    def _():
        wb1[...] = w1b_ref[...].astype(BF16)
        wb3[...] = w3b_ref[...].astype(BF16)
        wb2[...] = w2b_ref[...].astype(BF16)

    ahead = jnp.minimum(i + 2, n_used - 1)

    @pl.when(i == 0)
    def _():
        compute(0, [(gather_row, ahead, 2)])
        wait_rows(gather_row, 1)

    @pl.when(i == 1)
    def _():
        compute(1, [(gather_row, ahead, 0), (scatter_row, 0, 0)])
        wait_rows(gather_row, 2)

    for m in range(N_BUF):
        m1, m2 = (m + 1) % N_BUF, (m + 2) % N_BUF

        @pl.when(jnp.logical_and(jnp.logical_and(i >= 2, i < n_used), b0 == m))
        def _(m=m, m1=m1, m2=m2):
            compute(m, [(gather_row, ahead, m2), (scatter_row, i - 1, m2)])
            wait_rows(gather_row, m1)
            wait_rows(scatter_row, m1)

    @pl.when(i == n_used)
    def _():
        def last_start(r, carry):
            scatter_row((i - 1) * TM + r, b2, r).start()
            return carry
        lax.fori_loop(0, TM, last_start, 0, unroll=8)
        wait_rows(gather_row, b1)
        wait_rows(scatter_row, b1)

    @pl.when(i == n_used + 1)
    def _():
        wait_rows(scatter_row, b1)


def _experts(tea, teb, nu, pos, seg, x1, w1, w3, w2, layer):
    def wspec(r, c, slot):
        return pl.BlockSpec(
            (None, None, r, c),
            lambda i, tea, teb, nu, pos, seg: (layer, (tea, teb)[slot][i], 0, 0))

    return pl.pallas_call(
        _expert_kernel,
        out_shape=jax.ShapeDtypeStruct((N_SLOTS, D_MODEL), F32),
        grid_spec=pltpu.PrefetchScalarGridSpec(
            num_scalar_prefetch=5,
            grid=(N_MTILES + N_BUF - 1,),
            in_specs=[pl.BlockSpec(memory_space=pl.ANY),
                      wspec(D_MODEL, D_EXPERT, 0), wspec(D_MODEL, D_EXPERT, 0),
                      wspec(D_EXPERT, D_MODEL, 0),
                      wspec(D_MODEL, D_EXPERT, 1), wspec(D_MODEL, D_EXPERT, 1),
                      wspec(D_EXPERT, D_MODEL, 1)],
            out_specs=pl.BlockSpec(memory_space=pl.ANY),
            scratch_shapes=[pltpu.VMEM((D_MODEL, D_EXPERT), BF16),
                            pltpu.VMEM((D_MODEL, D_EXPERT), BF16),
                            pltpu.VMEM((D_EXPERT, D_MODEL), BF16),
                            pltpu.VMEM((D_MODEL, D_EXPERT), BF16),
                            pltpu.VMEM((D_MODEL, D_EXPERT), BF16),
                            pltpu.VMEM((D_EXPERT, D_MODEL), BF16),
                            pltpu.VMEM((N_BUF, TM, XW), F32),
                            pltpu.VMEM((N_BUF, TM, D_MODEL), F32),
                            pltpu.SMEM((N_SLOTS,), I32),
                            pltpu.SemaphoreType.DMA((N_BUF,)),
                            pltpu.SemaphoreType.DMA((N_BUF,))],
        ),
        compiler_params=pltpu.CompilerParams(
            dimension_semantics=("arbitrary",), vmem_limit_bytes=VMEM_LIMIT,
            has_side_effects=True),
        name="moe_experts",
    )(tea, teb, nu, pos, seg, x1, w1, w3, w2, w1, w3, w2)


def _combine_kernel(x1_ref, p_ref, y_ref, wp_ref, wg_ref, g_ref, b_ref, o_ref):
    x1 = x1_ref[...]
    gate = _sigmoid(jnp.dot(x1.astype(BF16), wg_ref[...], preferred_element_type=F32))
    ple = jnp.dot(p_ref[...].astype(BF16), wp_ref[...], preferred_element_type=F32) * gate
    o_ref[...] = _layer_norm(ALPHA * x1 + y_ref[...] + ple, g_ref[...], b_ref[...])


def _combine(x1, p, ys, wp, wg, g, b):
    full = lambda shape: pl.BlockSpec(shape, lambda i: (0,) * len(shape))
    return pl.pallas_call(
        _combine_kernel,
        out_shape=jax.ShapeDtypeStruct((TOKENS, D_MODEL), F32),
        grid=(TOKENS // RD,),
        in_specs=[pl.BlockSpec((RD, D_MODEL), lambda i: (i, 0)),
                  pl.BlockSpec((RD, PLE_DIM), lambda i: (i, 0)),
                  pl.BlockSpec((RD, D_MODEL), lambda i: (i, 0)),
                  full((PLE_DIM, D_MODEL)),
                  full((D_MODEL, D_MODEL)),
                  full((1, D_MODEL)),
                  full((1, D_MODEL))],
        out_specs=pl.BlockSpec((RD, D_MODEL), lambda i: (i, 0)),
        compiler_params=pltpu.CompilerParams(
            dimension_semantics=("arbitrary",), vmem_limit_bytes=VMEM_LIMIT),
        name="moe_combine",
    )(x1, p, ys, wp, wg, g, b)


def _qkv_kernel(x_ref, w_ref, o_ref):
    o_ref[...] = jnp.dot(x_ref[...].astype(BF16), w_ref[...],
                         preferred_element_type=F32).astype(BF16)


def _qkv_proj(x, w):
    return pl.pallas_call(
        _qkv_kernel,
        out_shape=jax.ShapeDtypeStruct((TOKENS, 3 * D_MODEL), BF16),
        grid=(TOKENS // RP,),
        in_specs=[pl.BlockSpec((RP, D_MODEL), lambda i: (i, 0)),
                  pl.BlockSpec((D_MODEL, 3 * D_MODEL), lambda i: (0, 0))],
        out_specs=pl.BlockSpec((RP, 3 * D_MODEL), lambda i: (i, 0)),
        compiler_params=pltpu.CompilerParams(
            dimension_semantics=("arbitrary",), vmem_limit_bytes=VMEM_LIMIT),
        name="qkv_proj",
    )(x, w)


def _attn_kernel(q_ref, k_ref, v_ref, o_ref, acc, cbuf, bnd, kinf):
    qb = pl.program_id(2)
    lane = lax.broadcasted_iota(I32, (1, 128), 1)
    half = (lane < 64, lane >= 64)

    @pl.when(qb == 0)
    def _():
        for pr in range(ATT_PAIRS):
            ka = jnp.max(jnp.abs(k_ref[:, pr * 128:(pr + 1) * 128].astype(F32)),
                         axis=0, keepdims=True)
            for hh in range(2):
                m = jnp.max(jnp.where(half[hh], ka, 0.0), axis=1, keepdims=True)
                kinf[2 * pr + hh] = jnp.broadcast_to(m, (8, 128))

    qh = []
    for pr in range(ATT_PAIRS):
        q = q_ref[:, pr * 128:(pr + 1) * 128] * jnp.asarray(0.125, BF16)
        zero = jnp.zeros_like(q)
        qh.append((jnp.where(half[0], q, zero), jnp.where(half[1], q, zero)))
        qa = jnp.abs(q.astype(F32))
        for hh in range(2):
            qn = jnp.sum(jnp.where(half[hh], qa, 0.0), axis=1, keepdims=True)
            bnd[2 * pr + hh] = (jnp.broadcast_to(qn, (SB_BLOCK, 128))
                                * kinf[2 * pr + hh][0:1, :] * BOUND_SLACK)
    r = lax.broadcasted_iota(I32, (SB_BLOCK, SB_BLOCK), 0)
    cidx = lax.broadcasted_iota(I32, (SB_BLOCK, SB_BLOCK), 1)
    neg_suffix = jnp.where(r >= cidx, -1.0, 0.0).astype(BF16)
    causal = cidx < r

    acc[...] = jnp.zeros_like(acc)
    cbuf[...] = jnp.zeros_like(cbuf)

    def block(kb, masked):
        rows = pl.ds(pl.multiple_of(kb * SB_BLOCK, SB_BLOCK), SB_BLOCK)
        heads = [(pr, hh) for pr in range(ATT_PAIRS) for hh in range(2)]
        ks = [k_ref[rows, pr * 128:(pr + 1) * 128] for pr in range(ATT_PAIRS)]
        vs = [v_ref[rows, pr * 128:(pr + 1) * 128] for pr in range(ATT_PAIRS)]
        zs = [lax.dot_general(qh[pr][hh], ks[pr], NT_DIMS, preferred_element_type=F32)
              for pr, hh in heads]
        sps = []
        for z in zs:
            sp = jnp.maximum(z, 0.0) + jnp.log(1.0 + jnp.exp2(jnp.abs(z) * (-LOG2E)))
            if masked:
                sp = jnp.where(causal, sp, 0.0)
            sps.append(sp.astype(BF16))
        rss = []
        for idx, sp in enumerate(sps):
            c = cbuf[idx]
            rss.append(jnp.dot(sp, neg_suffix, preferred_element_type=F32)
                       + jnp.concatenate([c, c], axis=1))
        for idx, (pr, hh) in enumerate(heads):
            w = jnp.exp(zs[idx] + rss[idx])
            if masked:
                w = jnp.where(causal, w, 0.0)
            acc[idx] = acc[idx] + jnp.dot(w.astype(BF16), vs[pr], preferred_element_type=F32)
            cbuf[idx] = jnp.broadcast_to(rss[idx][:, 0:1], (SB_BLOCK, 128))

    def log_weight_bound():
        m = cbuf[0] + bnd[0]
        for idx in range(1, 2 * ATT_PAIRS):
            m = jnp.maximum(m, cbuf[idx] + bnd[idx])
        return jnp.max(m)

    block(qb, True)

    def cond(carry):
        j, m = carry
        return jnp.logical_and(j < qb, m > EXIT_LOG_WEIGHT)

    def body(carry):
        j, _ = carry
        block(qb - 1 - j, False)
        return j + 1, log_weight_bound()

    lax.while_loop(cond, body, (jnp.int32(0), log_weight_bound()))
    for pr in range(ATT_PAIRS):
        o_ref[:, pr * 128:(pr + 1) * 128] = jnp.where(
            lane < 64, acc[2 * pr], acc[2 * pr + 1]).astype(BF16)


def _attention(qkv):
    n_hg = D_MODEL // ATT_LANES
    return pl.pallas_call(
        _attn_kernel,
        out_shape=jax.ShapeDtypeStruct((BATCH, SEQ, D_MODEL), BF16),
        grid=(BATCH, n_hg, SEQ // SB_BLOCK),
        in_specs=[pl.BlockSpec((None, SB_BLOCK, ATT_LANES), lambda b, h, i: (b, i, h)),
                  pl.BlockSpec((None, SEQ, ATT_LANES), lambda b, h, i: (b, 0, n_hg + h)),
                  pl.BlockSpec((None, SEQ, ATT_LANES), lambda b, h, i: (b, 0, 2 * n_hg + h))],
        out_specs=pl.BlockSpec((None, SB_BLOCK, ATT_LANES), lambda b, h, i: (b, i, h)),
        scratch_shapes=[pltpu.VMEM((2 * ATT_PAIRS, SB_BLOCK, 128), F32),
                        pltpu.VMEM((2 * ATT_PAIRS, SB_BLOCK, 128), F32),
                        pltpu.VMEM((2 * ATT_PAIRS, SB_BLOCK, 128), F32),
                        pltpu.VMEM((2 * ATT_PAIRS, 8, 128), F32)],
        compiler_params=pltpu.CompilerParams(
            dimension_semantics=("arbitrary", "arbitrary", "arbitrary"),
            vmem_limit_bytes=VMEM_LIMIT),
        name="sb_attention",
    )(qkv, qkv, qkv)


def _moe_ffn(x1, ri, p_rows, layer, moe_w1, moe_w3, moe_w2, wp, wg, g, b):
    pos, te, seg = _rank(ri)
    tile_seg = te[0, :N_MTILES + N_BUF - 1]
    grp = tile_seg // PAIRS_PER_GROUP
    k = tile_seg % PAIRS_PER_GROUP
    tea = grp * EXPERTS_PER_GROUP + sum(jnp.where(k == kk, a, 0) for kk, a in enumerate(PAIR_A))
    teb = grp * EXPERTS_PER_GROUP + sum(jnp.where(k == kk, b_, 0) for kk, b_ in enumerate(PAIR_B))
    ys = _experts(tea, teb, te[1, :1], pos[0], seg[:, 0], x1, moe_w1, moe_w3, moe_w2, layer)
    return _combine(x1, p_rows, ys, wp, wg, g, b)


def kernel(x, p, ab_w_in, s5_lambda_re, s5_lambda_im, s5_log_dt, s5_b_re, s5_b_im, s5_c_re, s5_c_im, s5_d, s5_w_glu, pool_w, pool_scale, ab_w_out, sb_w_qkv, sb_w_out, ln_mix_g, ln_mix_b, ln_ffn_g, ln_ffn_b, router_w, router_bias, moe_w1, moe_w3, moe_w2, ple_w_proj, ple_w_gate):
    row = lambda a: a.reshape(1, -1)
    rwt = router_w.T
    rwh = rwt.astype(BF16)
    rwl = (rwt - rwh.astype(F32)).astype(BF16)
    rb = router_bias.reshape(N_EXPERTS, 1)

    x_rows = x.reshape(TOKENS, D_MODEL)
    a_re, a_im, bcat = _s5_prep(s5_lambda_re[0], s5_lambda_im[0], s5_log_dt[0],
                                s5_b_re[0], s5_b_im[0])

    def c_blockdiag(c):
        c4 = c.reshape(S5_CHUNKS, 8, S5_H, S5_P).transpose(0, 1, 3, 2)
        eye = jnp.eye(8, dtype=F32)
        full = c4[:, :, :, None, :] * eye[None, :, None, :, None]
        return full.reshape(S5_CHUNKS, 8 * S5_P, 8 * S5_H)

    ccat = jnp.concatenate([c_blockdiag(s5_c_re[0]), -c_blockdiag(s5_c_im[0])],
                           axis=1).astype(BF16)
    mix_in = _mixer0(x, ab_w_in[0].astype(BF16), bcat, a_re, a_im, ccat,
                     row(s5_d[0]), s5_w_glu[0].astype(BF16), pool_w[0].astype(BF16),
                     row(pool_scale[0])).reshape(TOKENS, D_MODEL)
    x1, ri = _proj_ln_route(mix_in, ab_w_out[0].astype(BF16), x_rows,
                            row(ln_mix_g[0]), row(ln_mix_b[0]), rwh, rwl, rb)
    xb = _moe_ffn(x1, ri, p[0].reshape(TOKENS, PLE_DIM), 0, moe_w1, moe_w3, moe_w2,
                  ple_w_proj[0].astype(BF16), ple_w_gate[0].astype(BF16),
                  row(ln_ffn_g[0]), row(ln_ffn_b[0]))

    qkv = _qkv_proj(xb, sb_w_qkv[0].astype(BF16))
    att = _attention(qkv.reshape(BATCH, SEQ, 3 * D_MODEL)).reshape(TOKENS, D_MODEL)
    x3, ri = _proj_ln_route(att, sb_w_out[0].astype(BF16), xb,
                            row(ln_mix_g[1]), row(ln_mix_b[1]), rwh, rwl, rb)
    x4 = _moe_ffn(x3, ri, p[1].reshape(TOKENS, PLE_DIM), 1, moe_w1, moe_w3, moe_w2,
                  ple_w_proj[1].astype(BF16), ple_w_gate[1].astype(BF16),
                  row(ln_ffn_g[1]), row(ln_ffn_b[1]))
    return x4.reshape(BATCH, SEQ, D_MODEL)
```

```python
import functools
import math

import jax
import jax.numpy as jnp
from jax import lax
from jax.experimental import pallas as pl
from jax.experimental.pallas import tpu as pltpu

F32 = jnp.float32
BF16 = jnp.bfloat16
I32 = jnp.int32

D_MODEL = 1024
LANE_TILES = D_MODEL // 128
BATCH = 8
SEQ = 2048
DEPTH = 2
TOKENS = BATCH * SEQ

D_A = 512
S5_H = 16
S5_G = 32
S5_P = 64
S5_STATE = S5_G * S5_P
S5_CHUNKS = 4
POOL_WINDOWS = (2, 4, 8, 16)
POOL_C = 128
POOL_HALO = 16 * BATCH
SB_BLOCK = 256
ATT_PAIRS = 4
ATT_LANES = 128 * ATT_PAIRS
N_EXPERTS = 16
EXPERTS_PER_GROUP = 4
D_EXPERT = 512
PLE_DIM = 256
ALPHA = (2 * DEPTH) ** 0.25
LN_EPS = 1e-5

TS0 = 64
R0 = TS0 * BATCH
RP = 512
RR = 1024
RD = 512
TM = 256
N_BUF = 3
ISSUE_STAGES = 8
PAIRS_PER_GROUP = 6
PAIR_A = (0, 2, 2, 3, 3, 3)
PAIR_B = (1, 1, 0, 0, 1, 2)
N_SEG = (N_EXPERTS // EXPERTS_PER_GROUP) * PAIRS_PER_GROUP
SEG_ROWS = 32
XW = D_MODEL + 128
N_SLOTS = TOKENS + N_SEG * TM
N_MTILES = N_SLOTS // TM
TE_LANES = 256

VMEM_LIMIT = 52 * 1024 * 1024
NT_DIMS = (((1,), (1,)), ((), ()))
LOG2E = 1.0 / math.log(2.0)
EXIT_LOG_WEIGHT = -110.0
BOUND_SLACK = 1.01


def _sigmoid(x):
    return 1.0 / (1.0 + jnp.exp(-x))


def _layer_norm(x, g, b):
    mu = jnp.mean(x, axis=-1, keepdims=True)
    xc = x - mu
    var = jnp.mean(xc * xc, axis=-1, keepdims=True)
    return xc * lax.rsqrt(var + LN_EPS) * g + b


def _s5_prep_kernel(lre_ref, lim_ref, ldt_ref, bre_ref, bim_ref,
                    are_ref, aim_ref, bcat_ref):
    lam_re = lre_ref[...]
    lam_im = lim_ref[...]
    dt = jnp.exp(ldt_ref[...])
    mag = jnp.exp(lam_re * dt)
    ang = lam_im * dt
    lb_re = mag * jnp.cos(ang)
    lb_im = mag * jnp.sin(ang)
    are_ref[...] = lb_re
    aim_ref[...] = lb_im
    den = lam_re * lam_re + lam_im * lam_im
    num_re = lb_re - 1.0
    f_re = (num_re * lam_re + lb_im * lam_im) / den
    f_im = (lb_im * lam_re - num_re * lam_im) / den
    for c in range(S5_CHUNKS):
        fr = f_re[:, c * 512:(c + 1) * 512]
        fi = f_im[:, c * 512:(c + 1) * 512]
        br = bre_ref[c]
        bi = bim_ref[c]
        bcat_ref[c, :, 0:512] = (fr * br - fi * bi).astype(BF16)
        bcat_ref[c, :, 512:1024] = (fr * bi + fi * br).astype(BF16)


def _s5_prep(lam_re, lam_im, log_dt, b_re, b_im):
    def blockdiag(b):
        b4 = b.reshape(S5_CHUNKS, 8, S5_P, S5_H).transpose(0, 1, 3, 2)
        eye = jnp.eye(8, dtype=F32)
        full = b4[:, :, :, None, :] * eye[None, :, None, :, None]
        return full.reshape(S5_CHUNKS, 8 * S5_H, 8 * S5_P)

    return pl.pallas_call(
        _s5_prep_kernel,
        out_shape=(jax.ShapeDtypeStruct((1, S5_STATE), F32),
                   jax.ShapeDtypeStruct((1, S5_STATE), F32),
                   jax.ShapeDtypeStruct((S5_CHUNKS, 128, 1024), BF16)),
        name="s5_prep",
    )(lam_re.reshape(1, S5_STATE), lam_im.reshape(1, S5_STATE),
      jnp.repeat(log_dt, S5_P).reshape(1, S5_STATE), blockdiag(b_re), blockdiag(b_im))


def _mixer0_kernel(x_ref, win_ref, bcat_ref, are_ref, aim_ref, ccat_ref, d_ref,
                   wglu_ref, poolw_ref, pscale_ref, o_ref, bus, hst, pe, hbm, htm, otm):
    i = pl.program_id(0)

    @pl.when(i == 0)
    def _():
        hst[...] = jnp.zeros_like(hst)
        pe[0:POOL_HALO, :] = jnp.zeros((POOL_HALO, D_A), F32)

    h_bm = jnp.dot(x_ref[...].reshape(R0, D_MODEL).astype(BF16), win_ref[...],
                   preferred_element_type=F32)
    for c in range(LANE_TILES):
        hbm[c] = h_bm[:, c * 128:(c + 1) * 128]
    for t in range(TS0):
        for c in range(LANE_TILES):
            htm[t * BATCH:(t + 1) * BATCH, c * 128:(c + 1) * 128] = (
                hbm[c, pl.ds(t, BATCH, stride=TS0), :])
    h = htm[...]
    u = h[:, :D_A]
    v = h[:, D_A:]
    ub = u.astype(BF16)

    for c in range(S5_CHUNKS):
        bus[:, c * 1024:(c + 1) * 1024] = jnp.dot(
            ub[:, c * 128:(c + 1) * 128], bcat_ref[c], preferred_element_type=F32)

    for c in range(S5_CHUNKS):
        re_cols = slice(c * 1024, c * 1024 + 512)
        im_cols = slice(c * 1024 + 512, (c + 1) * 1024)
        ar = jnp.broadcast_to(are_ref[:, c * 512:(c + 1) * 512], (BATCH, 512))
        ai = jnp.broadcast_to(aim_ref[:, c * 512:(c + 1) * 512], (BATCH, 512))

        def step(t, carry, re_cols=re_cols, im_cols=im_cols, ar=ar, ai=ai):
            hr, hi = carry
            rows = pl.ds(pl.multiple_of(t * BATCH, BATCH), BATCH)
            nr = ar * hr - ai * hi + bus[rows, re_cols]
            ni = ar * hi + ai * hr + bus[rows, im_cols]
            bus[rows, re_cols] = nr
            bus[rows, im_cols] = ni
            return nr, ni

        hr, hi = lax.fori_loop(0, TS0, step, (hst[:, re_cols], hst[:, im_cols]),
                               unroll=True)
        hst[:, re_cols] = hr
        hst[:, im_cols] = hi

    ys = [jnp.dot(bus[:, c * 1024:(c + 1) * 1024].astype(BF16), ccat_ref[c],
                  preferred_element_type=F32) for c in range(S5_CHUNKS)]
    y = jnp.concatenate(ys, axis=1) + d_ref[...] * u
    y = 0.5 * y * (1.0 + jnp.tanh(math.sqrt(2.0 / math.pi) * (y + 0.044715 * (y * y * y))))
    ga = y * _sigmoid(jnp.dot(y.astype(BF16), wglu_ref[...], preferred_element_type=F32))
    for c in range(D_A // 128):
        otm[c] = ga[:, c * 128:(c + 1) * 128]

    pe[POOL_HALO:, :] = v
    t_glob = lax.shift_right_logical(
        lax.broadcasted_iota(I32, (R0, 1), 0), int(math.log2(BATCH))) + i * TS0
    for gi, w in enumerate(POOL_WINDOWS):
        cols = slice(gi * POOL_C, (gi + 1) * POOL_C)
        s = pe[:, cols]
        off = BATCH
        while off < BATCH * w:
            s = s[off:] + s[:-off]
            off *= 2
        s = s[POOL_HALO - BATCH * (w - 1):]
        cnt = jnp.minimum(t_glob + 1, w).astype(F32)
        pooled = s / cnt - v[:, cols]
        mixed = jnp.dot(pooled.astype(BF16), poolw_ref[gi], preferred_element_type=F32)
        otm[D_A // 128 + gi] = mixed * pscale_ref[:, cols]
    pe[0:POOL_HALO, :] = pe[R0:R0 + POOL_HALO, :]
    for b in range(BATCH):
        for c in range(LANE_TILES):
            o_ref[b, :, c * 128:(c + 1) * 128] = (
                otm[c, pl.ds(b, TS0, stride=BATCH), :].astype(BF16))


def _mixer0(x, win, bcat, a_re, a_im, ccat, dskip, wglu, poolw, pscale):
    full = lambda shape: pl.BlockSpec(shape, lambda i: (0,) * len(shape))
    return pl.pallas_call(
        _mixer0_kernel,
        out_shape=jax.ShapeDtypeStruct((BATCH, SEQ, D_MODEL), BF16),
        grid=(SEQ // TS0,),
        in_specs=[
            pl.BlockSpec((BATCH, TS0, D_MODEL), lambda i: (0, i, 0)),
            full((D_MODEL, D_MODEL)),
            full((S5_CHUNKS, 128, 1024)),
            full((1, S5_STATE)),
            full((1, S5_STATE)),
            full((S5_CHUNKS, 1024, 128)),
            full((1, D_A)),
            full((D_A, D_A)),
            full((4, POOL_C, POOL_C)),
            full((1, D_A)),
        ],
        out_specs=pl.BlockSpec((BATCH, TS0, D_MODEL), lambda i: (0, i, 0)),
        scratch_shapes=[
            pltpu.VMEM((R0, 2 * S5_STATE), F32),
            pltpu.VMEM((BATCH, 2 * S5_STATE), F32),
            pltpu.VMEM((POOL_HALO + R0, D_A), F32),
            pltpu.VMEM((LANE_TILES, R0, 128), F32),
            pltpu.VMEM((R0, D_MODEL), F32),
            pltpu.VMEM((LANE_TILES, R0, 128), F32),
        ],
        compiler_params=pltpu.CompilerParams(
            dimension_semantics=("arbitrary",), vmem_limit_bytes=VMEM_LIMIT),
        name="mixer0",
    )(x, win, bcat, a_re, a_im, ccat, dskip, wglu, poolw, pscale)


def _router_scores(x1, rwh_ref, rwl_ref):
    xh = x1.astype(BF16)
    xl = (x1 - xh.astype(F32)).astype(BF16)
    rwh = rwh_ref[...]
    logits = (lax.dot_general(rwh, xh, NT_DIMS, preferred_element_type=F32)
              + lax.dot_general(rwh, xl, NT_DIMS, preferred_element_type=F32)
              + lax.dot_general(rwl_ref[...], xh, NT_DIMS, preferred_element_type=F32))
    return _sigmoid(logits)


def _route(scores, rb_ref):
    n = scores.shape[1]
    sel = scores + rb_ref[...]
    row = lambda a, e: a[e:e + 1, :]

    best = None
    grp = None
    for g in range(N_EXPERTS // EXPERTS_PER_GROUP):
        m = [row(sel, EXPERTS_PER_GROUP * g + k) for k in range(EXPERTS_PER_GROUP)]
        gs = None
        for a in range(EXPERTS_PER_GROUP):
            for b in range(a + 1, EXPERTS_PER_GROUP):
                pair = m[a] + m[b]
                gs = pair if gs is None else jnp.maximum(gs, pair)
        if best is None:
            best, grp = gs, jnp.zeros(gs.shape, I32)
        else:
            better = gs > best
            grp = jnp.where(better, g, grp)
            best = jnp.where(better, gs, best)

    def pick(a, k):
        out = row(a, k)
        for g in range(1, N_EXPERTS // EXPERTS_PER_GROUP):
            out = jnp.where(grp == g, row(a, EXPERTS_PER_GROUP * g + k), out)
        return out

    cs = [pick(sel, k) for k in range(EXPERTS_PER_GROUP)]
    ss = [pick(scores, k) for k in range(EXPERTS_PER_GROUP)]
    m1, i1, s1 = cs[0], jnp.zeros(cs[0].shape, I32), ss[0]
    for k in range(1, EXPERTS_PER_GROUP):
        better = cs[k] > m1
        i1 = jnp.where(better, k, i1)
        s1 = jnp.where(better, ss[k], s1)
        m1 = jnp.where(better, cs[k], m1)
    m2 = jnp.full(m1.shape, -jnp.inf, F32)
    i2 = jnp.zeros(m1.shape, I32)
    s2 = jnp.zeros(m1.shape, F32)
    for k in range(EXPERTS_PER_GROUP):
        better = jnp.logical_and(i1 != k, cs[k] > m2)
        i2 = jnp.where(better, k, i2)
        s2 = jnp.where(better, ss[k], s2)
        m2 = jnp.where(better, cs[k], m2)
    tot = s1 + s2
    w0 = s1 / tot
    w1 = s2 / tot
    lo = jnp.minimum(i1, i2)
    hi = jnp.maximum(i1, i2)
    k = jnp.zeros(lo.shape, I32)
    in_a = jnp.zeros(lo.shape, I32)
    for kk, (a, b) in enumerate(zip(PAIR_A, PAIR_B)):
        hit = jnp.logical_and(lo == min(a, b), hi == max(a, b))
        k = jnp.where(hit, kk, k)
        in_a = jnp.where(hit, a, in_a)
    combo = grp * PAIRS_PER_GROUP + k
    first_in_a = i1 == in_a
    wa = jnp.where(first_in_a, w0, w1)
    wb = jnp.where(first_in_a, w1, w0)
    ri = jnp.concatenate([combo, jnp.zeros((7, n), I32)], axis=0)
    wslab = jnp.concatenate([wa, wb, jnp.zeros((126, n), F32)], axis=0)
    return ri, wslab.T


def _proj_ln_route_kernel(a_ref, w_ref, x_ref, g_ref, b_ref, rwh_ref, rwl_ref, rb_ref,
                          x1_ref, ri_ref, mixbuf):
    i = pl.program_id(0)

    @pl.when(i == 0)
    def _():
        mixbuf[1] = jnp.zeros((RP, D_MODEL), F32)

    for par in range(2):
        @pl.when(lax.rem(i, 2) == par)
        def _(par=par):
            mixbuf[par] = jnp.dot(a_ref[...], w_ref[...], preferred_element_type=F32)
            x1 = _layer_norm(ALPHA * x_ref[...] + mixbuf[1 - par], g_ref[...], b_ref[...])
            x1_ref[:, 0:D_MODEL] = x1
            scores = _router_scores(x1, rwh_ref, rwl_ref)
            ri, wcol = _route(scores, rb_ref)
            ri_ref[...] = ri
            x1_ref[:, D_MODEL:XW] = wcol


def _proj_ln_route(a, w, xres, g, b, rwh, rwl, rb):
    full = lambda shape: pl.BlockSpec(shape, lambda i: (0,) * len(shape))
    n_tiles = TOKENS // RP
    done = lambda i: jnp.maximum(i - 1, 0)
    return pl.pallas_call(
        _proj_ln_route_kernel,
        out_shape=(jax.ShapeDtypeStruct((TOKENS, XW), F32),
                   jax.ShapeDtypeStruct((8, TOKENS), I32)),
        grid=(n_tiles + 1,),
        in_specs=[
            pl.BlockSpec((RP, D_MODEL), lambda i: (jnp.minimum(i, n_tiles - 1), 0)),
            full((D_MODEL, D_MODEL)),
            pl.BlockSpec((RP, D_MODEL), lambda i: (done(i), 0)),
            full((1, D_MODEL)),
            full((1, D_MODEL)),
            full((N_EXPERTS, D_MODEL)),
            full((N_EXPERTS, D_MODEL)),
            full((N_EXPERTS, 1)),
        ],
        out_specs=(pl.BlockSpec((RP, XW), lambda i: (done(i), 0)),
                   pl.BlockSpec((8, RP), lambda i: (0, done(i)))),
        scratch_shapes=[pltpu.VMEM((2, RP, D_MODEL), F32)],
        compiler_params=pltpu.CompilerParams(
            dimension_semantics=("arbitrary",), vmem_limit_bytes=VMEM_LIMIT),
        name="proj_ln_route",
    )(a, w, xres, g, b, rwh, rwl, rb)


def _rank_kernel(ri_ref, pos_ref, te_ref, seg_ref, cnt, offs, tot):
    ph = pl.program_id(0)
    i = pl.program_id(1)
    n_tiles = pl.num_programs(1)
    s_iota = lax.broadcasted_iota(I32, (SEG_ROWS, RR), 0)
    hit = s_iota == ri_ref[0:1, :]
    mask = jnp.where(hit, 1.0, 0.0)
    tile_cnt = jnp.sum(mask, axis=1, keepdims=True)

    @pl.when(jnp.logical_and(ph == 0, i == 0))
    def _():
        cnt[...] = jnp.zeros_like(cnt)

    @pl.when(ph == 0)
    def _():
        cnt[...] = cnt[...] + tile_cnt
        pos_ref[...] = jnp.zeros_like(pos_ref)

    @pl.when(jnp.logical_and(ph == 0, i == n_tiles - 1))
    def _():
        c = cnt[...]
        padded = jnp.ceil(c * (1.0 / TM)) * TM
        sub = lax.broadcasted_iota(I32, (SEG_ROWS, 128), 0)
        acc = jnp.zeros((SEG_ROWS, 128), F32)
        for e in range(N_SEG):
            acc = acc + jnp.where(sub > e, padded[e:e + 1, :], 0.0)
        offs[...] = acc
        tot[...] = acc + padded
        cnt[...] = jnp.zeros_like(cnt)

    @pl.when(ph == 1)
    def _():
        r = lax.broadcasted_iota(I32, (RR, RR), 0)
        cidx = lax.broadcasted_iota(I32, (RR, RR), 1)
        tri = jnp.where(r < cidx, 1.0, 0.0).astype(BF16)
        prefix = jnp.dot(mask.astype(BF16), tri, preferred_element_type=F32)
        slot = prefix + cnt[:, 0:1] + offs[:, 0:1]
        p0 = jnp.sum(jnp.where(hit, slot, 0.0), axis=0, keepdims=True)
        pos_ref[...] = jnp.concatenate([p0.astype(I32), jnp.zeros((7, RR), I32)], axis=0)
        cnt[...] = cnt[...] + tile_cnt

    @pl.when(jnp.logical_and(ph == 1, i == n_tiles - 1))
    def _():
        ends = tot[:, 0:1]
        start = (lax.broadcasted_iota(I32, (SEG_ROWS, TE_LANES), 1) * TM).astype(F32)
        te = jnp.sum(jnp.where(start >= ends, 1.0, 0.0), axis=0, keepdims=True)
        te = jnp.minimum(te, N_SEG - 1.0).astype(I32)
        n_used = (tot[SEG_ROWS - 1:SEG_ROWS, :] * (1.0 / TM)).astype(I32)
        n_used = jnp.concatenate([n_used, n_used], axis=1)
        te_ref[...] = jnp.concatenate(
            [te, n_used, jnp.zeros((6, TE_LANES), I32)], axis=0)
        seg_ref[0:SEG_ROWS, :] = (offs[...] + cnt[...]).astype(I32)
        seg_ref[SEG_ROWS:2 * SEG_ROWS, :] = tot[...].astype(I32)


def _rank(ri):
    return pl.pallas_call(
        _rank_kernel,
        out_shape=(jax.ShapeDtypeStruct((8, TOKENS), I32),
                   jax.ShapeDtypeStruct((8, TE_LANES), I32),
                   jax.ShapeDtypeStruct((2 * SEG_ROWS, 128), I32)),
        grid=(2, TOKENS // RR),
        in_specs=[pl.BlockSpec((8, RR), lambda p, i: (0, i))],
        out_specs=(pl.BlockSpec((8, RR), lambda p, i: (0, i * p)),
                   pl.BlockSpec((8, TE_LANES), lambda p, i: (0, 0)),
                   pl.BlockSpec((2 * SEG_ROWS, 128), lambda p, i: (0, 0))),
        scratch_shapes=[pltpu.VMEM((SEG_ROWS, 128), F32),
                        pltpu.VMEM((SEG_ROWS, 128), F32),
                        pltpu.VMEM((SEG_ROWS, 128), F32)],
        compiler_params=pltpu.CompilerParams(
            dimension_semantics=("arbitrary", "arbitrary")),
        name="moe_rank",
    )(ri)


def _expert_kernel(tea_ref, teb_ref, nu_ref, pos_ref, seg_ref, x_hbm,
                   w1a_ref, w3a_ref, w2a_ref, w1b_ref, w3b_ref, w2b_ref, y_hbm,
                   wa1, wa3, wa2, wb1, wb3, wb2, xbuf, ybuf, dst, gsem, ssem):
    i = pl.program_id(0)
    n_used = nu_ref[0]
    b0 = lax.rem(i, N_BUF)
    b1 = lax.rem(i + 1, N_BUF)
    b2 = lax.rem(i + 2, N_BUF)

    def gather_row(slot, buf, r):
        tok = dst[slot] & (TOKENS - 1)
        return pltpu.make_async_copy(x_hbm.at[pl.ds(tok, 1)], xbuf.at[buf, pl.ds(r, 1)],
                                     gsem.at[buf])

    def scatter_row(slot, buf, r):
        return pltpu.make_async_copy(ybuf.at[buf, pl.ds(r, 1)], y_hbm.at[pl.ds(dst[slot], 1)],
                                     ssem.at[buf])

    def wait_rows(row_copy, buf):
        for r in range(TM):
            row_copy(0, buf, 0).wait()

    def compute(buf, copies):
        def issue(stage):
            lo, hi = stage * TM // ISSUE_STAGES, (stage + 1) * TM // ISSUE_STAGES
            for row_copy, tile, cbuf in copies:
                for r in range(lo, hi):
                    row_copy(tile * TM + r, cbuf, r).start(priority=r % 2)

        xf = xbuf[buf]
        xb = xf[:, 0:D_MODEL].astype(BF16)
        issue(0)
        h1a = jnp.dot(xb, wa1[...], preferred_element_type=F32)
        issue(1)
        h3a = jnp.dot(xb, wa3[...], preferred_element_type=F32)
        issue(2)
        h1b = jnp.dot(xb, wb1[...], preferred_element_type=F32)
        issue(3)
        h3b = jnp.dot(xb, wb3[...], preferred_element_type=F32)
        issue(4)
        ha = ((h1a * _sigmoid(h1a)) * h3a * xf[:, D_MODEL:D_MODEL + 1]).astype(BF16)
        hb = ((h1b * _sigmoid(h1b)) * h3b * xf[:, D_MODEL + 1:D_MODEL + 2]).astype(BF16)
        issue(5)
        ya = jnp.dot(ha, wa2[...], preferred_element_type=F32)
        issue(6)
        ybuf[buf] = ya + jnp.dot(hb, wb2[...], preferred_element_type=F32)
        issue(7)

    @pl.when(i == 0)
    def _():
        ybuf[1] = jnp.zeros((TM, D_MODEL), F32)
        fills = [pltpu.make_async_copy(
            ybuf.at[1], y_hbm.at[pl.ds(TOKENS + e * TM, TM)], ssem.at[1])
            for e in range(N_SEG)]
        for fill in fills:
            fill.start()
        for fill in fills:
            fill.wait()

        for e in range(N_SEG):
            def pad_body(s, carry, e=e):
                dst[s] = TOKENS + e * TM + (s - seg_ref[e])
                return carry
            lax.fori_loop(seg_ref[e], seg_ref[SEG_ROWS + e], pad_body, 0)

        def inv_body(t, carry):
            dst[pos_ref[t]] = t
            return carry
        lax.fori_loop(0, TOKENS, inv_body, 0, unroll=8)

        def first_start(r, carry):
            gather_row(r, 0, r).start()
            gather_row(TM + r, 1, r).start()
            return carry
        lax.fori_loop(0, TM, first_start, 0, unroll=8)

        def first_wait(r, carry):
            gather_row(0, 0, 0).wait()
            return carry
        lax.fori_loop(0, TM, first_wait, 0, unroll=8)

    prev = jnp.maximum(i - 1, 0)

    @pl.when(jnp.logical_and(i < n_used,
                             jnp.logical_or(i == 0, tea_ref[i] != tea_ref[prev])))
    def _():
        wa1[...] = w1a_ref[...].astype(BF16)
        wa3[...] = w3a_ref[...].astype(BF16)
        wa2[...] = w2a_ref[...].astype(BF16)

    @pl.when(jnp.logical_and(i < n_used,
                             jnp.logical_or(i == 0, teb_ref[i] != teb_ref[prev])))
    def _():
        wb1[...] = w1b_ref[...].astype(BF16)
        wb3[...] = w3b_ref[...].astype(BF16)
        wb2[...] = w2b_ref[...].astype(BF16)

    ahead = jnp.minimum(i + 2, n_used - 1)

    @pl.when(i == 0)
    def _():
        compute(0, [(gather_row, ahead, 2)])
        wait_rows(gather_row, 1)

    @pl.when(i == 1)
    def _():
        compute(1, [(gather_row, ahead, 0), (scatter_row, 0, 0)])
        wait_rows(gather_row, 2)

    for m in range(N_BUF):
        m1, m2 = (m + 1) % N_BUF, (m + 2) % N_BUF

        @pl.when(jnp.logical_and(jnp.logical_and(i >= 2, i < n_used), b0 == m))
        def _(m=m, m1=m1, m2=m2):
            compute(m, [(gather_row, ahead, m2), (scatter_row, i - 1, m2)])
            wait_rows(gather_row, m1)
            wait_rows(scatter_row, m1)

    @pl.when(i == n_used)
    def _():
        def last_start(r, carry):
            scatter_row((i - 1) * TM + r, b2, r).start()
            return carry
        lax.fori_loop(0, TM, last_start, 0, unroll=8)
        wait_rows(gather_row, b1)
        wait_rows(scatter_row, b1)

    @pl.when(i == n_used + 1)
    def _():
        wait_rows(scatter_row, b1)


def _experts(tea, teb, nu, pos, seg, x1, w1, w3, w2, layer):
    def wspec(r, c, slot):
        return pl.BlockSpec(
            (None, None, r, c),
            lambda i, tea, teb, nu, pos, seg: (layer, (tea, teb)[slot][i], 0, 0))

    return pl.pallas_call(
        _expert_kernel,
        out_shape=jax.ShapeDtypeStruct((N_SLOTS, D_MODEL), F32),
        grid_spec=pltpu.PrefetchScalarGridSpec(
            num_scalar_prefetch=5,
            grid=(N_MTILES + N_BUF - 1,),
            in_specs=[pl.BlockSpec(memory_space=pl.ANY),
                      wspec(D_MODEL, D_EXPERT, 0), wspec(D_MODEL, D_EXPERT, 0),
                      wspec(D_EXPERT, D_MODEL, 0),
                      wspec(D_MODEL, D_EXPERT, 1), wspec(D_MODEL, D_EXPERT, 1),
                      wspec(D_EXPERT, D_MODEL, 1)],
            out_specs=pl.BlockSpec(memory_space=pl.ANY),
            scratch_shapes=[pltpu.VMEM((D_MODEL, D_EXPERT), BF16),
                            pltpu.VMEM((D_MODEL, D_EXPERT), BF16),
                            pltpu.VMEM((D_EXPERT, D_MODEL), BF16),
                            pltpu.VMEM((D_MODEL, D_EXPERT), BF16),
                            pltpu.VMEM((D_MODEL, D_EXPERT), BF16),
                            pltpu.VMEM((D_EXPERT, D_MODEL), BF16),
                            pltpu.VMEM((N_BUF, TM, XW), F32),
                            pltpu.VMEM((N_BUF, TM, D_MODEL), F32),
                            pltpu.SMEM((N_SLOTS,), I32),
                            pltpu.SemaphoreType.DMA((N_BUF,)),
                            pltpu.SemaphoreType.DMA((N_BUF,))],
        ),
        compiler_params=pltpu.CompilerParams(
            dimension_semantics=("arbitrary",), vmem_limit_bytes=VMEM_LIMIT,
            has_side_effects=True),
        name="moe_experts",
    )(tea, teb, nu, pos, seg, x1, w1, w3, w2, w1, w3, w2)


def _combine_kernel(x1_ref, p_ref, y_ref, wp_ref, wg_ref, g_ref, b_ref, o_ref):
    x1 = x1_ref[...]
    gate = _sigmoid(jnp.dot(x1.astype(BF16), wg_ref[...], preferred_element_type=F32))
    ple = jnp.dot(p_ref[...].astype(BF16), wp_ref[...], preferred_element_type=F32) * gate
    o_ref[...] = _layer_norm(ALPHA * x1 + y_ref[...] + ple, g_ref[...], b_ref[...])


def _combine(x1, p, ys, wp, wg, g, b):
    full = lambda shape: pl.BlockSpec(shape, lambda i: (0,) * len(shape))
    return pl.pallas_call(
        _combine_kernel,
        out_shape=jax.ShapeDtypeStruct((TOKENS, D_MODEL), F32),
        grid=(TOKENS // RD,),
        in_specs=[pl.BlockSpec((RD, D_MODEL), lambda i: (i, 0)),
                  pl.BlockSpec((RD, PLE_DIM), lambda i: (i, 0)),
                  pl.BlockSpec((RD, D_MODEL), lambda i: (i, 0)),
                  full((PLE_DIM, D_MODEL)),
                  full((D_MODEL, D_MODEL)),
                  full((1, D_MODEL)),
                  full((1, D_MODEL))],
        out_specs=pl.BlockSpec((RD, D_MODEL), lambda i: (i, 0)),
        compiler_params=pltpu.CompilerParams(
            dimension_semantics=("arbitrary",), vmem_limit_bytes=VMEM_LIMIT),
        name="moe_combine",
    )(x1, p, ys, wp, wg, g, b)


def _qkv_kernel(x_ref, w_ref, o_ref):
    o_ref[...] = jnp.dot(x_ref[...].astype(BF16), w_ref[...],
                         preferred_element_type=F32).astype(BF16)


def _qkv_proj(x, w):
    return pl.pallas_call(
        _qkv_kernel,
        out_shape=jax.ShapeDtypeStruct((TOKENS, 3 * D_MODEL), BF16),
        grid=(TOKENS // RP,),
        in_specs=[pl.BlockSpec((RP, D_MODEL), lambda i: (i, 0)),
                  pl.BlockSpec((D_MODEL, 3 * D_MODEL), lambda i: (0, 0))],
        out_specs=pl.BlockSpec((RP, 3 * D_MODEL), lambda i: (i, 0)),
        compiler_params=pltpu.CompilerParams(
            dimension_semantics=("arbitrary",), vmem_limit_bytes=VMEM_LIMIT),
        name="qkv_proj",
    )(x, w)


def _attn_kernel(q_ref, k_ref, v_ref, o_ref, acc, cbuf, bnd, kinf):
    qb = pl.program_id(2)
    lane = lax.broadcasted_iota(I32, (1, 128), 1)
    half = (lane < 64, lane >= 64)

    @pl.when(qb == 0)
    def _():
        for pr in range(ATT_PAIRS):
            ka = jnp.max(jnp.abs(k_ref[:, pr * 128:(pr + 1) * 128].astype(F32)),
                         axis=0, keepdims=True)
            for hh in range(2):
                m = jnp.max(jnp.where(half[hh], ka, 0.0), axis=1, keepdims=True)
                kinf[2 * pr + hh] = jnp.broadcast_to(m, (8, 128))

    qh = []
    for pr in range(ATT_PAIRS):
        q = q_ref[:, pr * 128:(pr + 1) * 128] * jnp.asarray(0.125, BF16)
        zero = jnp.zeros_like(q)
        qh.append((jnp.where(half[0], q, zero), jnp.where(half[1], q, zero)))
        qa = jnp.abs(q.astype(F32))
        for hh in range(2):
            qn = jnp.sum(jnp.where(half[hh], qa, 0.0), axis=1, keepdims=True)
            bnd[2 * pr + hh] = (jnp.broadcast_to(qn, (SB_BLOCK, 128))
                                * kinf[2 * pr + hh][0:1, :] * BOUND_SLACK)
    r = lax.broadcasted_iota(I32, (SB_BLOCK, SB_BLOCK), 0)
    cidx = lax.broadcasted_iota(I32, (SB_BLOCK, SB_BLOCK), 1)
    neg_suffix = jnp.where(r >= cidx, -1.0, 0.0).astype(BF16)
    causal = cidx < r

    acc[...] = jnp.zeros_like(acc)
    cbuf[...] = jnp.zeros_like(cbuf)

    def block(kb, masked):
        rows = pl.ds(pl.multiple_of(kb * SB_BLOCK, SB_BLOCK), SB_BLOCK)
        heads = [(pr, hh) for pr in range(ATT_PAIRS) for hh in range(2)]
        ks = [k_ref[rows, pr * 128:(pr + 1) * 128] for pr in range(ATT_PAIRS)]
        vs = [v_ref[rows, pr * 128:(pr + 1) * 128] for pr in range(ATT_PAIRS)]
        zs = [lax.dot_general(qh[pr][hh], ks[pr], NT_DIMS, preferred_element_type=F32)
              for pr, hh in heads]
        sps = []
        for z in zs:
            sp = jnp.maximum(z, 0.0) + jnp.log(1.0 + jnp.exp2(jnp.abs(z) * (-LOG2E)))
            if masked:
                sp = jnp.where(causal, sp, 0.0)
            sps.append(sp.astype(BF16))
        rss = []
        for idx, sp in enumerate(sps):
            rs = jnp.dot(sp, neg_suffix, preferred_element_type=F32)
            if not masked:
                c = cbuf[idx]
                rs = rs + jnp.concatenate([c, c], axis=1)
            rss.append(rs)
        for idx, (pr, hh) in enumerate(heads):
            w = jnp.exp(zs[idx] + rss[idx])
            if masked:
                w = jnp.where(causal, w, 0.0)
            acc[idx] = acc[idx] + jnp.dot(w.astype(BF16), vs[pr], preferred_element_type=F32)
            cbuf[idx] = jnp.broadcast_to(rss[idx][:, 0:1], (SB_BLOCK, 128))

    def log_weight_bound():
        m = cbuf[0] + bnd[0]
        for idx in range(1, 2 * ATT_PAIRS):
            m = jnp.maximum(m, cbuf[idx] + bnd[idx])
        return jnp.max(m)

    block(qb, True)

    def cond(carry):
        j, m = carry
        return jnp.logical_and(j < qb, m > EXIT_LOG_WEIGHT)

    def body(carry):
        j, _ = carry
        block(qb - 1 - j, False)
        return j + 1, log_weight_bound()

    lax.while_loop(cond, body, (jnp.int32(0), log_weight_bound()))
    for pr in range(ATT_PAIRS):
        o_ref[:, pr * 128:(pr + 1) * 128] = jnp.where(
            lane < 64, acc[2 * pr], acc[2 * pr + 1]).astype(BF16)


def _attention(qkv):
    n_hg = D_MODEL // ATT_LANES
    return pl.pallas_call(
        _attn_kernel,
        out_shape=jax.ShapeDtypeStruct((BATCH, SEQ, D_MODEL), BF16),
        grid=(BATCH, n_hg, SEQ // SB_BLOCK),
        in_specs=[pl.BlockSpec((None, SB_BLOCK, ATT_LANES), lambda b, h, i: (b, i, h)),
                  pl.BlockSpec((None, SEQ, ATT_LANES), lambda b, h, i: (b, 0, n_hg + h)),
                  pl.BlockSpec((None, SEQ, ATT_LANES), lambda b, h, i: (b, 0, 2 * n_hg + h))],
        out_specs=pl.BlockSpec((None, SB_BLOCK, ATT_LANES), lambda b, h, i: (b, i, h)),
        scratch_shapes=[pltpu.VMEM((2 * ATT_PAIRS, SB_BLOCK, 128), F32),
                        pltpu.VMEM((2 * ATT_PAIRS, SB_BLOCK, 128), F32),
                        pltpu.VMEM((2 * ATT_PAIRS, SB_BLOCK, 128), F32),
                        pltpu.VMEM((2 * ATT_PAIRS, 8, 128), F32)],
        compiler_params=pltpu.CompilerParams(
            dimension_semantics=("arbitrary", "arbitrary", "arbitrary"),
            vmem_limit_bytes=VMEM_LIMIT),
        name="sb_attention",
    )(qkv, qkv, qkv)


def _moe_ffn(x1, ri, p_rows, layer, moe_w1, moe_w3, moe_w2, wp, wg, g, b):
    pos, te, seg = _rank(ri)
    tile_seg = te[0, :N_MTILES + N_BUF - 1]
    grp = tile_seg // PAIRS_PER_GROUP
    k = tile_seg % PAIRS_PER_GROUP
    tea = grp * EXPERTS_PER_GROUP + sum(jnp.where(k == kk, a, 0) for kk, a in enumerate(PAIR_A))
    teb = grp * EXPERTS_PER_GROUP + sum(jnp.where(k == kk, b_, 0) for kk, b_ in enumerate(PAIR_B))
    ys = _experts(tea, teb, te[1, :1], pos[0], seg[:, 0], x1, moe_w1, moe_w3, moe_w2, layer)
    return _combine(x1, p_rows, ys, wp, wg, g, b)


def kernel(x, p, ab_w_in, s5_lambda_re, s5_lambda_im, s5_log_dt, s5_b_re, s5_b_im, s5_c_re, s5_c_im, s5_d, s5_w_glu, pool_w, pool_scale, ab_w_out, sb_w_qkv, sb_w_out, ln_mix_g, ln_mix_b, ln_ffn_g, ln_ffn_b, router_w, router_bias, moe_w1, moe_w3, moe_w2, ple_w_proj, ple_w_gate):
    row = lambda a: a.reshape(1, -1)
    rwt = router_w.T
    rwh = rwt.astype(BF16)
    rwl = (rwt - rwh.astype(F32)).astype(BF16)
    rb = router_bias.reshape(N_EXPERTS, 1)

    x_rows = x.reshape(TOKENS, D_MODEL)
    a_re, a_im, bcat = _s5_prep(s5_lambda_re[0], s5_lambda_im[0], s5_log_dt[0],
                                s5_b_re[0], s5_b_im[0])

    def c_blockdiag(c):
        c4 = c.reshape(S5_CHUNKS, 8, S5_H, S5_P).transpose(0, 1, 3, 2)
        eye = jnp.eye(8, dtype=F32)
        full = c4[:, :, :, None, :] * eye[None, :, None, :, None]
        return full.reshape(S5_CHUNKS, 8 * S5_P, 8 * S5_H)

    ccat = jnp.concatenate([c_blockdiag(s5_c_re[0]), -c_blockdiag(s5_c_im[0])],
                           axis=1).astype(BF16)
    mix_in = _mixer0(x, ab_w_in[0].astype(BF16), bcat, a_re, a_im, ccat,
                     row(s5_d[0]), s5_w_glu[0].astype(BF16), pool_w[0].astype(BF16),
                     row(pool_scale[0])).reshape(TOKENS, D_MODEL)
    x1, ri = _proj_ln_route(mix_in, ab_w_out[0].astype(BF16), x_rows,
                            row(ln_mix_g[0]), row(ln_mix_b[0]), rwh, rwl, rb)
    xb = _moe_ffn(x1, ri, p[0].reshape(TOKENS, PLE_DIM), 0, moe_w1, moe_w3, moe_w2,
                  ple_w_proj[0].astype(BF16), ple_w_gate[0].astype(BF16),
                  row(ln_ffn_g[0]), row(ln_ffn_b[0]))

    qkv = _qkv_proj(xb, sb_w_qkv[0].astype(BF16))
    att = _attention(qkv.reshape(BATCH, SEQ, 3 * D_MODEL)).reshape(TOKENS, D_MODEL)
    x3, ri = _proj_ln_route(att, sb_w_out[0].astype(BF16), xb,
                            row(ln_mix_g[1]), row(ln_mix_b[1]), rwh, rwl, rb)
    x4 = _moe_ffn(x3, ri, p[1].reshape(TOKENS, PLE_DIM), 1, moe_w1, moe_w3, moe_w2,
                  ple_w_proj[1].astype(BF16), ple_w_gate[1].astype(BF16),
                  row(ln_ffn_g[1]), row(ln_ffn_b[1]))
    return x4.reshape(BATCH, SEQ, D_MODEL)
```

```python
import functools
import math

import jax
import jax.numpy as jnp
from jax import lax
from jax.experimental import pallas as pl
from jax.experimental.pallas import tpu as pltpu

F32 = jnp.float32
BF16 = jnp.bfloat16
I32 = jnp.int32

D_MODEL = 1024
LANE_TILES = D_MODEL // 128
BATCH = 8
SEQ = 2048
DEPTH = 2
TOKENS = BATCH * SEQ

D_A = 512
S5_H = 16
S5_G = 32
S5_P = 64
S5_STATE = S5_G * S5_P
S5_CHUNKS = 4
POOL_WINDOWS = (2, 4, 8, 16)
POOL_C = 128
POOL_HALO = 16 * BATCH
SB_BLOCK = 256
ATT_PAIRS = 4
ATT_LANES = 128 * ATT_PAIRS
N_EXPERTS = 16
EXPERTS_PER_GROUP = 4
D_EXPERT = 512
PLE_DIM = 256
ALPHA = (2 * DEPTH) ** 0.25
LN_EPS = 1e-5

TS0 = 64
R0 = TS0 * BATCH
RP = 512
RR = 1024
RD = 512
TM = 256
N_BUF = 3
ISSUE_STAGES = 8
PAIRS_PER_GROUP = 6
PAIR_A = (0, 2, 2, 3, 3, 3)
PAIR_B = (1, 1, 0, 0, 1, 2)
N_SEG = (N_EXPERTS // EXPERTS_PER_GROUP) * PAIRS_PER_GROUP
SEG_ROWS = 32
XW = D_MODEL + 128
N_SLOTS = TOKENS + N_SEG * TM
N_MTILES = N_SLOTS // TM
TE_LANES = 256

VMEM_LIMIT = 52 * 1024 * 1024
NT_DIMS = (((1,), (1,)), ((), ()))
LOG2E = 1.0 / math.log(2.0)
EXIT_LOG_WEIGHT = -110.0
BOUND_SLACK = 1.01


def _sigmoid(x):
    return 1.0 / (1.0 + jnp.exp(-x))


def _layer_norm(x, g, b):
    mu = jnp.mean(x, axis=-1, keepdims=True)
    xc = x - mu
    var = jnp.mean(xc * xc, axis=-1, keepdims=True)
    return xc * lax.rsqrt(var + LN_EPS) * g + b


def _s5_prep_kernel(lre_ref, lim_ref, ldt_ref, bre_ref, bim_ref,
                    are_ref, aim_ref, bcat_ref):
    lam_re = lre_ref[...]
    lam_im = lim_ref[...]
    dt = jnp.exp(ldt_ref[...])
    mag = jnp.exp(lam_re * dt)
    ang = lam_im * dt
    lb_re = mag * jnp.cos(ang)
    lb_im = mag * jnp.sin(ang)
    are_ref[...] = lb_re
    aim_ref[...] = lb_im
    den = lam_re * lam_re + lam_im * lam_im
    num_re = lb_re - 1.0
    f_re = (num_re * lam_re + lb_im * lam_im) / den
    f_im = (lb_im * lam_re - num_re * lam_im) / den
    for c in range(S5_CHUNKS):
        fr = f_re[:, c * 512:(c + 1) * 512]
        fi = f_im[:, c * 512:(c + 1) * 512]
        br = bre_ref[c]
        bi = bim_ref[c]
        bcat_ref[c, :, 0:512] = (fr * br - fi * bi).astype(BF16)
        bcat_ref[c, :, 512:1024] = (fr * bi + fi * br).astype(BF16)


def _s5_prep(lam_re, lam_im, log_dt, b_re, b_im):
    def blockdiag(b):
        b4 = b.reshape(S5_CHUNKS, 8, S5_P, S5_H).transpose(0, 1, 3, 2)
        eye = jnp.eye(8, dtype=F32)
        full = b4[:, :, :, None, :] * eye[None, :, None, :, None]
        return full.reshape(S5_CHUNKS, 8 * S5_H, 8 * S5_P)

    return pl.pallas_call(
        _s5_prep_kernel,
        out_shape=(jax.ShapeDtypeStruct((1, S5_STATE), F32),
                   jax.ShapeDtypeStruct((1, S5_STATE), F32),
                   jax.ShapeDtypeStruct((S5_CHUNKS, 128, 1024), BF16)),
        name="s5_prep",
    )(lam_re.reshape(1, S5_STATE), lam_im.reshape(1, S5_STATE),
      jnp.repeat(log_dt, S5_P).reshape(1, S5_STATE), blockdiag(b_re), blockdiag(b_im))


def _mixer0_kernel(x_ref, win_ref, bcat_ref, are_ref, aim_ref, ccat_ref, d_ref,
                   wglu_ref, poolw_ref, pscale_ref, o_ref, bus, hst, pe, hbm, htm, otm):
    i = pl.program_id(0)

    @pl.when(i == 0)
    def _():
        hst[...] = jnp.zeros_like(hst)
        pe[0:POOL_HALO, :] = jnp.zeros((POOL_HALO, D_A), F32)

    h_bm = jnp.dot(x_ref[...].reshape(R0, D_MODEL).astype(BF16), win_ref[...],
                   preferred_element_type=F32)
    for c in range(LANE_TILES):
        hbm[c] = h_bm[:, c * 128:(c + 1) * 128]
    for t in range(TS0):
        for c in range(LANE_TILES):
            htm[t * BATCH:(t + 1) * BATCH, c * 128:(c + 1) * 128] = (
                hbm[c, pl.ds(t, BATCH, stride=TS0), :])
    h = htm[...]
    u = h[:, :D_A]
    v = h[:, D_A:]
    ub = u.astype(BF16)

    for c in range(S5_CHUNKS):
        bus[:, c * 1024:(c + 1) * 1024] = jnp.dot(
            ub[:, c * 128:(c + 1) * 128], bcat_ref[c], preferred_element_type=F32)

    for c in range(S5_CHUNKS):
        re_cols = slice(c * 1024, c * 1024 + 512)
        im_cols = slice(c * 1024 + 512, (c + 1) * 1024)
        ar = jnp.broadcast_to(are_ref[:, c * 512:(c + 1) * 512], (BATCH, 512))
        ai = jnp.broadcast_to(aim_ref[:, c * 512:(c + 1) * 512], (BATCH, 512))

        def step(t, carry, re_cols=re_cols, im_cols=im_cols, ar=ar, ai=ai):
            hr, hi = carry
            rows = pl.ds(pl.multiple_of(t * BATCH, BATCH), BATCH)
            nr = ar * hr - ai * hi + bus[rows, re_cols]
            ni = ar * hi + ai * hr + bus[rows, im_cols]
            bus[rows, re_cols] = nr
            bus[rows, im_cols] = ni
            return nr, ni

        hr, hi = lax.fori_loop(0, TS0, step, (hst[:, re_cols], hst[:, im_cols]),
                               unroll=True)
        hst[:, re_cols] = hr
        hst[:, im_cols] = hi

    ys = [jnp.dot(bus[:, c * 1024:(c + 1) * 1024].astype(BF16), ccat_ref[c],
                  preferred_element_type=F32) for c in range(S5_CHUNKS)]
    y = jnp.concatenate(ys, axis=1) + d_ref[...] * u
    y = 0.5 * y * (1.0 + jnp.tanh(math.sqrt(2.0 / math.pi) * (y + 0.044715 * (y * y * y))))
    ga = y * _sigmoid(jnp.dot(y.astype(BF16), wglu_ref[...], preferred_element_type=F32))
    for c in range(D_A // 128):
        otm[c] = ga[:, c * 128:(c + 1) * 128]

    pe[POOL_HALO:, :] = v
    t_glob = lax.shift_right_logical(
        lax.broadcasted_iota(I32, (R0, 1), 0), int(math.log2(BATCH))) + i * TS0
    for gi, w in enumerate(POOL_WINDOWS):
        cols = slice(gi * POOL_C, (gi + 1) * POOL_C)
        s = pe[:, cols]
        off = BATCH
        while off < BATCH * w:
            s = s[off:] + s[:-off]
            off *= 2
        s = s[POOL_HALO - BATCH * (w - 1):]
        cnt = jnp.minimum(t_glob + 1, w).astype(F32)
        pooled = s / cnt - v[:, cols]
        mixed = jnp.dot(pooled.astype(BF16), poolw_ref[gi], preferred_element_type=F32)
        otm[D_A // 128 + gi] = mixed * pscale_ref[:, cols]
    pe[0:POOL_HALO, :] = pe[R0:R0 + POOL_HALO, :]
    for b in range(BATCH):
        for c in range(LANE_TILES):
            o_ref[b, :, c * 128:(c + 1) * 128] = (
                otm[c, pl.ds(b, TS0, stride=BATCH), :].astype(BF16))


def _mixer0(x, win, bcat, a_re, a_im, ccat, dskip, wglu, poolw, pscale):
    full = lambda shape: pl.BlockSpec(shape, lambda i: (0,) * len(shape))
    return pl.pallas_call(
        _mixer0_kernel,
        out_shape=jax.ShapeDtypeStruct((BATCH, SEQ, D_MODEL), BF16),
        grid=(SEQ // TS0,),
        in_specs=[
            pl.BlockSpec((BATCH, TS0, D_MODEL), lambda i: (0, i, 0)),
            full((D_MODEL, D_MODEL)),
            full((S5_CHUNKS, 128, 1024)),
            full((1, S5_STATE)),
            full((1, S5_STATE)),
            full((S5_CHUNKS, 1024, 128)),
            full((1, D_A)),
            full((D_A, D_A)),
            full((4, POOL_C, POOL_C)),
            full((1, D_A)),
        ],
        out_specs=pl.BlockSpec((BATCH, TS0, D_MODEL), lambda i: (0, i, 0)),
        scratch_shapes=[
            pltpu.VMEM((R0, 2 * S5_STATE), F32),
            pltpu.VMEM((BATCH, 2 * S5_STATE), F32),
            pltpu.VMEM((POOL_HALO + R0, D_A), F32),
            pltpu.VMEM((LANE_TILES, R0, 128), F32),
            pltpu.VMEM((R0, D_MODEL), F32),
            pltpu.VMEM((LANE_TILES, R0, 128), F32),
        ],
        compiler_params=pltpu.CompilerParams(
            dimension_semantics=("arbitrary",), vmem_limit_bytes=VMEM_LIMIT),
        name="mixer0",
    )(x, win, bcat, a_re, a_im, ccat, dskip, wglu, poolw, pscale)


def _router_scores(x1, rwh_ref, rwl_ref):
    xh = x1.astype(BF16)
    xl = (x1 - xh.astype(F32)).astype(BF16)
    rwh = rwh_ref[...]
    logits = (lax.dot_general(rwh, xh, NT_DIMS, preferred_element_type=F32)
              + lax.dot_general(rwh, xl, NT_DIMS, preferred_element_type=F32)
              + lax.dot_general(rwl_ref[...], xh, NT_DIMS, preferred_element_type=F32))
    return _sigmoid(logits)


def _route(scores, rb_ref):
    n = scores.shape[1]
    sel = scores + rb_ref[...]
    row = lambda a, e: a[e:e + 1, :]

    best = None
    grp = None
    for g in range(N_EXPERTS // EXPERTS_PER_GROUP):
        m = [row(sel, EXPERTS_PER_GROUP * g + k) for k in range(EXPERTS_PER_GROUP)]
        gs = None
        for a in range(EXPERTS_PER_GROUP):
            for b in range(a + 1, EXPERTS_PER_GROUP):
                pair = m[a] + m[b]
                gs = pair if gs is None else jnp.maximum(gs, pair)
        if best is None:
            best, grp = gs, jnp.zeros(gs.shape, I32)
        else:
            better = gs > best
            grp = jnp.where(better, g, grp)
            best = jnp.where(better, gs, best)

    def pick(a, k):
        out = row(a, k)
        for g in range(1, N_EXPERTS // EXPERTS_PER_GROUP):
            out = jnp.where(grp == g, row(a, EXPERTS_PER_GROUP * g + k), out)
        return out

    cs = [pick(sel, k) for k in range(EXPERTS_PER_GROUP)]
    ss = [pick(scores, k) for k in range(EXPERTS_PER_GROUP)]
    m1, i1, s1 = cs[0], jnp.zeros(cs[0].shape, I32), ss[0]
    for k in range(1, EXPERTS_PER_GROUP):
        better = cs[k] > m1
        i1 = jnp.where(better, k, i1)
        s1 = jnp.where(better, ss[k], s1)
        m1 = jnp.where(better, cs[k], m1)
    m2 = jnp.full(m1.shape, -jnp.inf, F32)
    i2 = jnp.zeros(m1.shape, I32)
    s2 = jnp.zeros(m1.shape, F32)
    for k in range(EXPERTS_PER_GROUP):
        better = jnp.logical_and(i1 != k, cs[k] > m2)
        i2 = jnp.where(better, k, i2)
        s2 = jnp.where(better, ss[k], s2)
        m2 = jnp.where(better, cs[k], m2)
    tot = s1 + s2
    w0 = s1 / tot
    w1 = s2 / tot
    lo = jnp.minimum(i1, i2)
    hi = jnp.maximum(i1, i2)
    k = jnp.zeros(lo.shape, I32)
    in_a = jnp.zeros(lo.shape, I32)
    for kk, (a, b) in enumerate(zip(PAIR_A, PAIR_B)):
        hit = jnp.logical_and(lo == min(a, b), hi == max(a, b))
        k = jnp.where(hit, kk, k)
        in_a = jnp.where(hit, a, in_a)
    combo = grp * PAIRS_PER_GROUP + k
    first_in_a = i1 == in_a
    wa = jnp.where(first_in_a, w0, w1)
    wb = jnp.where(first_in_a, w1, w0)
    ri = jnp.concatenate([combo, jnp.zeros((7, n), I32)], axis=0)
    wslab = jnp.concatenate([wa, wb, jnp.zeros((126, n), F32)], axis=0)
    return ri, wslab.T


def _proj_ln_route_kernel(a_ref, w_ref, x_ref, g_ref, b_ref, rwh_ref, rwl_ref, rb_ref,
                          x1_ref, ri_ref, mixbuf):
    i = pl.program_id(0)

    @pl.when(i == 0)
    def _():
        mixbuf[1] = jnp.zeros((RP, D_MODEL), F32)

    for par in range(2):
        @pl.when(lax.rem(i, 2) == par)
        def _(par=par):
            mixbuf[par] = jnp.dot(a_ref[...], w_ref[...], preferred_element_type=F32)
            x1 = _layer_norm(ALPHA * x_ref[...] + mixbuf[1 - par], g_ref[...], b_ref[...])
            x1_ref[:, 0:D_MODEL] = x1
            scores = _router_scores(x1, rwh_ref, rwl_ref)
            ri, wcol = _route(scores, rb_ref)
            ri_ref[...] = ri
            x1_ref[:, D_MODEL:XW] = wcol


def _proj_ln_route(a, w, xres, g, b, rwh, rwl, rb):
    full = lambda shape: pl.BlockSpec(shape, lambda i: (0,) * len(shape))
    n_tiles = TOKENS // RP
    done = lambda i: jnp.maximum(i - 1, 0)
    return pl.pallas_call(
        _proj_ln_route_kernel,
        out_shape=(jax.ShapeDtypeStruct((TOKENS, XW), F32),
                   jax.ShapeDtypeStruct((8, TOKENS), I32)),
        grid=(n_tiles + 1,),
        in_specs=[
            pl.BlockSpec((RP, D_MODEL), lambda i: (jnp.minimum(i, n_tiles - 1), 0)),
            full((D_MODEL, D_MODEL)),
            pl.BlockSpec((RP, D_MODEL), lambda i: (done(i), 0)),
            full((1, D_MODEL)),
            full((1, D_MODEL)),
            full((N_EXPERTS, D_MODEL)),
            full((N_EXPERTS, D_MODEL)),
            full((N_EXPERTS, 1)),
        ],
        out_specs=(pl.BlockSpec((RP, XW), lambda i: (done(i), 0)),
                   pl.BlockSpec((8, RP), lambda i: (0, done(i)))),
        scratch_shapes=[pltpu.VMEM((2, RP, D_MODEL), F32)],
        compiler_params=pltpu.CompilerParams(
            dimension_semantics=("arbitrary",), vmem_limit_bytes=VMEM_LIMIT),
        name="proj_ln_route",
    )(a, w, xres, g, b, rwh, rwl, rb)


def _rank_kernel(ri_ref, pos_ref, te_ref, seg_ref, cnt, offs, tot):
    ph = pl.program_id(0)
    i = pl.program_id(1)
    n_tiles = pl.num_programs(1)
    s_iota = lax.broadcasted_iota(I32, (SEG_ROWS, RR), 0)
    hit = s_iota == ri_ref[0:1, :]
    mask = jnp.where(hit, 1.0, 0.0)
    tile_cnt = jnp.sum(mask, axis=1, keepdims=True)

    @pl.when(jnp.logical_and(ph == 0, i == 0))
    def _():
        cnt[...] = jnp.zeros_like(cnt)

    @pl.when(ph == 0)
    def _():
        cnt[...] = cnt[...] + tile_cnt
        pos_ref[...] = jnp.zeros_like(pos_ref)

    @pl.when(jnp.logical_and(ph == 0, i == n_tiles - 1))
    def _():
        c = cnt[...]
        padded = jnp.ceil(c * (1.0 / TM)) * TM
        sub = lax.broadcasted_iota(I32, (SEG_ROWS, 128), 0)
        acc = jnp.zeros((SEG_ROWS, 128), F32)
        for e in range(N_SEG):
            acc = acc + jnp.where(sub > e, padded[e:e + 1, :], 0.0)
        offs[...] = acc
        tot[...] = acc + padded
        cnt[...] = jnp.zeros_like(cnt)

    @pl.when(ph == 1)
    def _():
        r = lax.broadcasted_iota(I32, (RR, RR), 0)
        cidx = lax.broadcasted_iota(I32, (RR, RR), 1)
        tri = jnp.where(r < cidx, 1.0, 0.0).astype(BF16)
        prefix = jnp.dot(mask.astype(BF16), tri, preferred_element_type=F32)
        slot = prefix + cnt[:, 0:1] + offs[:, 0:1]
        p0 = jnp.sum(jnp.where(hit, slot, 0.0), axis=0, keepdims=True)
        pos_ref[...] = jnp.concatenate([p0.astype(I32), jnp.zeros((7, RR), I32)], axis=0)
        cnt[...] = cnt[...] + tile_cnt

    @pl.when(jnp.logical_and(ph == 1, i == n_tiles - 1))
    def _():
        ends = tot[:, 0:1]
        start = (lax.broadcasted_iota(I32, (SEG_ROWS, TE_LANES), 1) * TM).astype(F32)
        te = jnp.sum(jnp.where(start >= ends, 1.0, 0.0), axis=0, keepdims=True)
        te = jnp.minimum(te, N_SEG - 1.0).astype(I32)
        n_used = (tot[SEG_ROWS - 1:SEG_ROWS, :] * (1.0 / TM)).astype(I32)
        n_used = jnp.concatenate([n_used, n_used], axis=1)
        te_ref[...] = jnp.concatenate(
            [te, n_used, jnp.zeros((6, TE_LANES), I32)], axis=0)
        seg_ref[0:SEG_ROWS, :] = (offs[...] + cnt[...]).astype(I32)
        seg_ref[SEG_ROWS:2 * SEG_ROWS, :] = tot[...].astype(I32)


def _rank(ri):
    return pl.pallas_call(
        _rank_kernel,
        out_shape=(jax.ShapeDtypeStruct((8, TOKENS), I32),
                   jax.ShapeDtypeStruct((8, TE_LANES), I32),
                   jax.ShapeDtypeStruct((2 * SEG_ROWS, 128), I32)),
        grid=(2, TOKENS // RR),
        in_specs=[pl.BlockSpec((8, RR), lambda p, i: (0, i))],
        out_specs=(pl.BlockSpec((8, RR), lambda p, i: (0, i * p)),
                   pl.BlockSpec((8, TE_LANES), lambda p, i: (0, 0)),
                   pl.BlockSpec((2 * SEG_ROWS, 128), lambda p, i: (0, 0))),
        scratch_shapes=[pltpu.VMEM((SEG_ROWS, 128), F32),
                        pltpu.VMEM((SEG_ROWS, 128), F32),
                        pltpu.VMEM((SEG_ROWS, 128), F32)],
        compiler_params=pltpu.CompilerParams(
            dimension_semantics=("arbitrary", "arbitrary")),
        name="moe_rank",
    )(ri)


def _expert_kernel(tea_ref, teb_ref, nu_ref, pos_ref, seg_ref, x_hbm,
                   w1a_ref, w3a_ref, w2a_ref, w1b_ref, w3b_ref, w2b_ref, y_hbm,
                   wa1, wa3, wa2, wb1, wb3, wb2, xbuf, ybuf, dst, gsem, ssem):
    i = pl.program_id(0)
    n_used = nu_ref[0]
    b0 = lax.rem(i, N_BUF)
    b1 = lax.rem(i + 1, N_BUF)
    b2 = lax.rem(i + 2, N_BUF)

    def gather_row(slot, buf, r):
        tok = dst[slot] & (TOKENS - 1)
        return pltpu.make_async_copy(x_hbm.at[pl.ds(tok, 1)], xbuf.at[buf, pl.ds(r, 1)],
                                     gsem.at[buf])

    def scatter_row(slot, buf, r):
        return pltpu.make_async_copy(ybuf.at[buf, pl.ds(r, 1)], y_hbm.at[pl.ds(dst[slot], 1)],
                                     ssem.at[buf])

    def wait_rows(row_copy, buf):
        for r in range(TM):
            row_copy(0, buf, 0).wait()

    def compute(buf, copies):
        def issue(stage):
            lo, hi = stage * TM // ISSUE_STAGES, (stage + 1) * TM // ISSUE_STAGES
            for row_copy, tile, cbuf in copies:
                for r in range(lo, hi):
                    row_copy(tile * TM + r, cbuf, r).start(priority=r % 2)

        xf = xbuf[buf]
        xb = xf[:, 0:D_MODEL].astype(BF16)
        issue(0)
        h1a = jnp.dot(xb, wa1[...], preferred_element_type=F32)
        issue(1)
        h3a = jnp.dot(xb, wa3[...], preferred_element_type=F32)
        issue(2)
        h1b = jnp.dot(xb, wb1[...], preferred_element_type=F32)
        issue(3)
        h3b = jnp.dot(xb, wb3[...], preferred_element_type=F32)
        issue(4)
        ha = ((h1a * _sigmoid(h1a)) * h3a * xf[:, D_MODEL:D_MODEL + 1]).astype(BF16)
        hb = ((h1b * _sigmoid(h1b)) * h3b * xf[:, D_MODEL + 1:D_MODEL + 2]).astype(BF16)
        issue(5)
        ya = jnp.dot(ha, wa2[...], preferred_element_type=F32)
        issue(6)
        ybuf[buf] = ya + jnp.dot(hb, wb2[...], preferred_element_type=F32)
        issue(7)

    @pl.when(i == 0)
    def _():
        ybuf[1] = jnp.zeros((TM, D_MODEL), F32)
        fills = [pltpu.make_async_copy(
            ybuf.at[1], y_hbm.at[pl.ds(TOKENS + e * TM, TM)], ssem.at[1])
            for e in range(N_SEG)]
        for fill in fills:
            fill.start()
        for fill in fills:
            fill.wait()

        for e in range(N_SEG):
            def pad_body(s, carry, e=e):
                dst[s] = TOKENS + e * TM + (s - seg_ref[e])
                return carry
            lax.fori_loop(seg_ref[e], seg_ref[SEG_ROWS + e], pad_body, 0)

        def inv_body(t, carry):
            dst[pos_ref[t]] = t
            return carry
        lax.fori_loop(0, TOKENS, inv_body, 0, unroll=8)

        def first_start(r, carry):
            gather_row(r, 0, r).start()
            gather_row(TM + r, 1, r).start()
            return carry
        lax.fori_loop(0, TM, first_start, 0, unroll=8)

        def first_wait(r, carry):
            gather_row(0, 0, 0).wait()
            return carry
        lax.fori_loop(0, TM, first_wait, 0, unroll=8)

    prev = jnp.maximum(i - 1, 0)

    @pl.when(jnp.logical_and(i < n_used,
                             jnp.logical_or(i == 0, tea_ref[i] != tea_ref[prev])))
    def _():
        wa1[...] = w1a_ref[...].astype(BF16)
        wa3[...] = w3a_ref[...].astype(BF16)
        wa2[...] = w2a_ref[...].astype(BF16)

    @pl.when(jnp.logical_and(i < n_used,
                             jnp.logical_or(i == 0, teb_ref[i] != teb_ref[prev])))
    def _():
        wb1[...] = w1b_ref[...].astype(BF16)
        wb3[...] = w3b_ref[...].astype(BF16)
        wb2[...] = w2b_ref[...].astype(BF16)

    ahead = jnp.minimum(i + 2, n_used - 1)

    @pl.when(i == 0)
    def _():
        compute(0, [(gather_row, ahead, 2)])
        wait_rows(gather_row, 1)

    @pl.when(i == 1)
    def _():
        compute(1, [(gather_row, ahead, 0), (scatter_row, 0, 0)])
        wait_rows(gather_row, 2)

    for m in range(N_BUF):
        m1, m2 = (m + 1) % N_BUF, (m + 2) % N_BUF

        @pl.when(jnp.logical_and(jnp.logical_and(i >= 2, i < n_used), b0 == m))
        def _(m=m, m1=m1, m2=m2):
            compute(m, [(gather_row, ahead, m2), (scatter_row, i - 1, m2)])
            wait_rows(gather_row, m1)
            wait_rows(scatter_row, m1)

    @pl.when(i == n_used)
    def _():
        def last_start(r, carry):
            scatter_row((i - 1) * TM + r, b2, r).start()
            return carry
        lax.fori_loop(0, TM, last_start, 0, unroll=8)
        wait_rows(gather_row, b1)
        wait_rows(scatter_row, b1)

    @pl.when(i == n_used + 1)
    def _():
        wait_rows(scatter_row, b1)


def _experts(tea, teb, nu, pos, seg, x1, w1, w3, w2, layer):
    def wspec(r, c, slot):
        return pl.BlockSpec(
            (None, None, r, c),
            lambda i, tea, teb, nu, pos, seg: (layer, (tea, teb)[slot][i], 0, 0))

    return pl.pallas_call(
        _expert_kernel,
        out_shape=jax.ShapeDtypeStruct((N_SLOTS, D_MODEL), F32),
        grid_spec=pltpu.PrefetchScalarGridSpec(
            num_scalar_prefetch=5,
            grid=(N_MTILES + N_BUF - 1,),
            in_specs=[pl.BlockSpec(memory_space=pl.ANY),
                      wspec(D_MODEL, D_EXPERT, 0), wspec(D_MODEL, D_EXPERT, 0),
                      wspec(D_EXPERT, D_MODEL, 0),
                      wspec(D_MODEL, D_EXPERT, 1), wspec(D_MODEL, D_EXPERT, 1),
                      wspec(D_EXPERT, D_MODEL, 1)],
            out_specs=pl.BlockSpec(memory_space=pl.ANY),
            scratch_shapes=[pltpu.VMEM((D_MODEL, D_EXPERT), BF16),
                            pltpu.VMEM((D_MODEL, D_EXPERT), BF16),
                            pltpu.VMEM((D_EXPERT, D_MODEL), BF16),
                            pltpu.VMEM((D_MODEL, D_EXPERT), BF16),
                            pltpu.VMEM((D_MODEL, D_EXPERT), BF16),
                            pltpu.VMEM((D_EXPERT, D_MODEL), BF16),
                            pltpu.VMEM((N_BUF, TM, XW), F32),
                            pltpu.VMEM((N_BUF, TM, D_MODEL), F32),
                            pltpu.SMEM((N_SLOTS,), I32),
                            pltpu.SemaphoreType.DMA((N_BUF,)),
                            pltpu.SemaphoreType.DMA((N_BUF,))],
        ),
        compiler_params=pltpu.CompilerParams(
            dimension_semantics=("arbitrary",), vmem_limit_bytes=VMEM_LIMIT,
            has_side_effects=True),
        name="moe_experts",
    )(tea, teb, nu, pos, seg, x1, w1, w3, w2, w1, w3, w2)


def _combine_kernel(x1_ref, p_ref, y_ref, wp_ref, wg_ref, g_ref, b_ref, o_ref):
    x1 = x1_ref[...]
    gate = _sigmoid(jnp.dot(x1.astype(BF16), wg_ref[...], preferred_element_type=F32))
    ple = jnp.dot(p_ref[...].astype(BF16), wp_ref[...], preferred_element_type=F32) * gate
    o_ref[...] = _layer_norm(ALPHA * x1 + y_ref[...] + ple, g_ref[...], b_ref[...])


def _combine(x1, p, ys, wp, wg, g, b):
    full = lambda shape: pl.BlockSpec(shape, lambda i: (0,) * len(shape))
    return pl.pallas_call(
        _combine_kernel,
        out_shape=jax.ShapeDtypeStruct((TOKENS, D_MODEL), F32),
        grid=(TOKENS // RD,),
        in_specs=[pl.BlockSpec((RD, D_MODEL), lambda i: (i, 0)),
                  pl.BlockSpec((RD, PLE_DIM), lambda i: (i, 0)),
                  pl.BlockSpec((RD, D_MODEL), lambda i: (i, 0)),
                  full((PLE_DIM, D_MODEL)),
                  full((D_MODEL, D_MODEL)),
                  full((1, D_MODEL)),
                  full((1, D_MODEL))],
        out_specs=pl.BlockSpec((RD, D_MODEL), lambda i: (i, 0)),
        compiler_params=pltpu.CompilerParams(
            dimension_semantics=("arbitrary",), vmem_limit_bytes=VMEM_LIMIT),
        name="moe_combine",
    )(x1, p, ys, wp, wg, g, b)


def _qkv_kernel(x_ref, w_ref, o_ref):
    o_ref[...] = jnp.dot(x_ref[...].astype(BF16), w_ref[...],
                         preferred_element_type=F32).astype(BF16)


def _qkv_proj(x, w):
    return pl.pallas_call(
        _qkv_kernel,
        out_shape=jax.ShapeDtypeStruct((TOKENS, 3 * D_MODEL), BF16),
        grid=(TOKENS // RP,),
        in_specs=[pl.BlockSpec((RP, D_MODEL), lambda i: (i, 0)),
                  pl.BlockSpec((D_MODEL, 3 * D_MODEL), lambda i: (0, 0))],
        out_specs=pl.BlockSpec((RP, 3 * D_MODEL), lambda i: (i, 0)),
        compiler_params=pltpu.CompilerParams(
            dimension_semantics=("arbitrary",), vmem_limit_bytes=VMEM_LIMIT),
        name="qkv_proj",
    )(x, w)


def _attn_kernel(q_ref, k_ref, v_ref, o_ref, acc, cbuf, bnd, kinf):
    qb = pl.program_id(2)
    lane = lax.broadcasted_iota(I32, (1, 128), 1)
    half = (lane < 64, lane >= 64)

    @pl.when(qb == 0)
    def _():
        for pr in range(ATT_PAIRS):
            ka = jnp.max(jnp.abs(k_ref[:, pr * 128:(pr + 1) * 128].astype(F32)),
                         axis=0, keepdims=True)
            for hh in range(2):
                m = jnp.max(jnp.where(half[hh], ka, 0.0), axis=1, keepdims=True)
                kinf[2 * pr + hh] = jnp.broadcast_to(m, (8, 128))

    qh = []
    for pr in range(ATT_PAIRS):
        q = q_ref[:, pr * 128:(pr + 1) * 128] * jnp.asarray(0.125, BF16)
        zero = jnp.zeros_like(q)
        qh.append((jnp.where(half[0], q, zero), jnp.where(half[1], q, zero)))
        qa = jnp.abs(q.astype(F32))
        for hh in range(2):
            qn = jnp.sum(jnp.where(half[hh], qa, 0.0), axis=1, keepdims=True)
            bnd[2 * pr + hh] = (jnp.broadcast_to(qn, (SB_BLOCK, 128))
                                * kinf[2 * pr + hh][0:1, :] * BOUND_SLACK)
    r = lax.broadcasted_iota(I32, (SB_BLOCK, SB_BLOCK), 0)
    cidx = lax.broadcasted_iota(I32, (SB_BLOCK, SB_BLOCK), 1)
    neg_suffix = jnp.where(r >= cidx, -1.0, 0.0).astype(BF16)
    causal = cidx < r

    def block(kb, masked):
        rows = pl.ds(pl.multiple_of(kb * SB_BLOCK, SB_BLOCK), SB_BLOCK)
        heads = [(pr, hh) for pr in range(ATT_PAIRS) for hh in range(2)]
        ks = [k_ref[rows, pr * 128:(pr + 1) * 128] for pr in range(ATT_PAIRS)]
        vs = [v_ref[rows, pr * 128:(pr + 1) * 128] for pr in range(ATT_PAIRS)]
        zs = [lax.dot_general(qh[pr][hh], ks[pr], NT_DIMS, preferred_element_type=F32)
              for pr, hh in heads]
        sps = []
        for z in zs:
            sp = jnp.maximum(z, 0.0) + jnp.log(1.0 + jnp.exp2(jnp.abs(z) * (-LOG2E)))
            if masked:
                sp = jnp.where(causal, sp, 0.0)
            sps.append(sp.astype(BF16))
        rss = []
        for idx, sp in enumerate(sps):
            rs = jnp.dot(sp, neg_suffix, preferred_element_type=F32)
            if not masked:
                c = cbuf[idx]
                rs = rs + jnp.concatenate([c, c], axis=1)
            rss.append(rs)
        for idx, (pr, hh) in enumerate(heads):
            w = jnp.exp(zs[idx] + rss[idx])
            if masked:
                w = jnp.where(causal, w, 0.0)
            pv = jnp.dot(w.astype(BF16), vs[pr], preferred_element_type=F32)
            acc[idx] = pv if masked else acc[idx] + pv
            cbuf[idx] = jnp.broadcast_to(rss[idx][:, 0:1], (SB_BLOCK, 128))

    def log_weight_bound():
        m = cbuf[0] + bnd[0]
        for idx in range(1, 2 * ATT_PAIRS):
            m = jnp.maximum(m, cbuf[idx] + bnd[idx])
        return jnp.max(m)

    @pl.when(qb == 0)
    def _():
        block(qb, True)

    @pl.when(qb > 0)
    def _():
        block(qb, True)
        block(qb - 1, False)

    def cond(carry):
        j, m = carry
        return jnp.logical_and(j < qb, m > EXIT_LOG_WEIGHT)

    def body(carry):
        j, _ = carry
        block(qb - 1 - j, False)
        return j + 1, log_weight_bound()

    lax.while_loop(cond, body, (jnp.int32(1), log_weight_bound()))
    for pr in range(ATT_PAIRS):
        o_ref[:, pr * 128:(pr + 1) * 128] = jnp.where(
            lane < 64, acc[2 * pr], acc[2 * pr + 1]).astype(BF16)


def _attention(qkv):
    n_hg = D_MODEL // ATT_LANES
    return pl.pallas_call(
        _attn_kernel,
        out_shape=jax.ShapeDtypeStruct((BATCH, SEQ, D_MODEL), BF16),
        grid=(BATCH, n_hg, SEQ // SB_BLOCK),
        in_specs=[pl.BlockSpec((None, SB_BLOCK, ATT_LANES), lambda b, h, i: (b, i, h)),
                  pl.BlockSpec((None, SEQ, ATT_LANES), lambda b, h, i: (b, 0, n_hg + h)),
                  pl.BlockSpec((None, SEQ, ATT_LANES), lambda b, h, i: (b, 0, 2 * n_hg + h))],
        out_specs=pl.BlockSpec((None, SB_BLOCK, ATT_LANES), lambda b, h, i: (b, i, h)),
        scratch_shapes=[pltpu.VMEM((2 * ATT_PAIRS, SB_BLOCK, 128), F32),
                        pltpu.VMEM((2 * ATT_PAIRS, SB_BLOCK, 128), F32),
                        pltpu.VMEM((2 * ATT_PAIRS, SB_BLOCK, 128), F32),
                        pltpu.VMEM((2 * ATT_PAIRS, 8, 128), F32)],
        compiler_params=pltpu.CompilerParams(
            dimension_semantics=("arbitrary", "arbitrary", "arbitrary"),
            vmem_limit_bytes=VMEM_LIMIT),
        name="sb_attention",
    )(qkv, qkv, qkv)


def _moe_ffn(x1, ri, p_rows, layer, moe_w1, moe_w3, moe_w2, wp, wg, g, b):
    pos, te, seg = _rank(ri)
    tile_seg = te[0, :N_MTILES + N_BUF - 1]
    grp = tile_seg // PAIRS_PER_GROUP
    k = tile_seg % PAIRS_PER_GROUP
    tea = grp * EXPERTS_PER_GROUP + sum(jnp.where(k == kk, a, 0) for kk, a in enumerate(PAIR_A))
    teb = grp * EXPERTS_PER_GROUP + sum(jnp.where(k == kk, b_, 0) for kk, b_ in enumerate(PAIR_B))
    ys = _experts(tea, teb, te[1, :1], pos[0], seg[:, 0], x1, moe_w1, moe_w3, moe_w2, layer)
    return _combine(x1, p_rows, ys, wp, wg, g, b)


def kernel(x, p, ab_w_in, s5_lambda_re, s5_lambda_im, s5_log_dt, s5_b_re, s5_b_im, s5_c_re, s5_c_im, s5_d, s5_w_glu, pool_w, pool_scale, ab_w_out, sb_w_qkv, sb_w_out, ln_mix_g, ln_mix_b, ln_ffn_g, ln_ffn_b, router_w, router_bias, moe_w1, moe_w3, moe_w2, ple_w_proj, ple_w_gate):
    row = lambda a: a.reshape(1, -1)
    rwt = router_w.T
    rwh = rwt.astype(BF16)
    rwl = (rwt - rwh.astype(F32)).astype(BF16)
    rb = router_bias.reshape(N_EXPERTS, 1)

    x_rows = x.reshape(TOKENS, D_MODEL)
    a_re, a_im, bcat = _s5_prep(s5_lambda_re[0], s5_lambda_im[0], s5_log_dt[0],
                                s5_b_re[0], s5_b_im[0])

    def c_blockdiag(c):
        c4 = c.reshape(S5_CHUNKS, 8, S5_H, S5_P).transpose(0, 1, 3, 2)
        eye = jnp.eye(8, dtype=F32)
        full = c4[:, :, :, None, :] * eye[None, :, None, :, None]
        return full.reshape(S5_CHUNKS, 8 * S5_P, 8 * S5_H)

    ccat = jnp.concatenate([c_blockdiag(s5_c_re[0]), -c_blockdiag(s5_c_im[0])],
                           axis=1).astype(BF16)
    mix_in = _mixer0(x, ab_w_in[0].astype(BF16), bcat, a_re, a_im, ccat,
                     row(s5_d[0]), s5_w_glu[0].astype(BF16), pool_w[0].astype(BF16),
                     row(pool_scale[0])).reshape(TOKENS, D_MODEL)
    x1, ri = _proj_ln_route(mix_in, ab_w_out[0].astype(BF16), x_rows,
                            row(ln_mix_g[0]), row(ln_mix_b[0]), rwh, rwl, rb)
    xb = _moe_ffn(x1, ri, p[0].reshape(TOKENS, PLE_DIM), 0, moe_w1, moe_w3, moe_w2,
                  ple_w_proj[0].astype(BF16), ple_w_gate[0].astype(BF16),
                  row(ln_ffn_g[0]), row(ln_ffn_b[0]))

    qkv = _qkv_proj(xb, sb_w_qkv[0].astype(BF16))
    att = _attention(qkv.reshape(BATCH, SEQ, 3 * D_MODEL)).reshape(TOKENS, D_MODEL)
    x3, ri = _proj_ln_route(att, sb_w_out[0].astype(BF16), xb,
                            row(ln_mix_g[1]), row(ln_mix_b[1]), rwh, rwl, rb)
    x4 = _moe_ffn(x3, ri, p[1].reshape(TOKENS, PLE_DIM), 1, moe_w1, moe_w3, moe_w2,
                  ple_w_proj[1].astype(BF16), ple_w_gate[1].astype(BF16),
                  row(ln_ffn_g[1]), row(ln_ffn_b[1]))
    return x4.reshape(BATCH, SEQ, D_MODEL)
```

```python
import functools
import math

import jax
import jax.numpy as jnp
from jax import lax
from jax.experimental import pallas as pl
from jax.experimental.pallas import tpu as pltpu

F32 = jnp.float32
BF16 = jnp.bfloat16
I32 = jnp.int32

D_MODEL = 1024
LANE_TILES = D_MODEL // 128
BATCH = 8
SEQ = 2048
DEPTH = 2
TOKENS = BATCH * SEQ

D_A = 512
S5_H = 16
S5_G = 32
S5_P = 64
S5_STATE = S5_G * S5_P
S5_CHUNKS = 4
POOL_WINDOWS = (2, 4, 8, 16)
POOL_C = 128
POOL_HALO = 16 * BATCH
SB_BLOCK = 256
ATT_PAIRS = 4
ATT_LANES = 128 * ATT_PAIRS
N_EXPERTS = 16
EXPERTS_PER_GROUP = 4
D_EXPERT = 512
PLE_DIM = 256
ALPHA = (2 * DEPTH) ** 0.25
LN_EPS = 1e-5

TS0 = 64
R0 = TS0 * BATCH
RP = 1024
RR = 1024
RD = 1024
TM = 256
N_BUF = 3
ISSUE_STAGES = 8
PAIRS_PER_GROUP = 6
PAIR_A = (0, 2, 2, 3, 3, 3)
PAIR_B = (1, 1, 0, 0, 1, 2)
N_SEG = (N_EXPERTS // EXPERTS_PER_GROUP) * PAIRS_PER_GROUP
SEG_ROWS = 32
XW = D_MODEL + 128
N_SLOTS = TOKENS + N_SEG * TM
N_MTILES = N_SLOTS // TM
TE_LANES = 256

VMEM_LIMIT = 52 * 1024 * 1024
NT_DIMS = (((1,), (1,)), ((), ()))
LOG2E = 1.0 / math.log(2.0)
EXIT_LOG_WEIGHT = -110.0
BOUND_SLACK = 1.01


def _sigmoid(x):
    return 1.0 / (1.0 + jnp.exp(-x))


def _layer_norm(x, g, b):
    mu = jnp.mean(x, axis=-1, keepdims=True)
    xc = x - mu
    var = jnp.mean(xc * xc, axis=-1, keepdims=True)
    return xc * lax.rsqrt(var + LN_EPS) * g + b


def _s5_prep_kernel(lre_ref, lim_ref, ldt_ref, bre_ref, bim_ref,
                    are_ref, aim_ref, bcat_ref):
    lam_re = lre_ref[...]
    lam_im = lim_ref[...]
    dt = jnp.exp(ldt_ref[...])
    mag = jnp.exp(lam_re * dt)
    ang = lam_im * dt
    lb_re = mag * jnp.cos(ang)
    lb_im = mag * jnp.sin(ang)
    are_ref[...] = lb_re
    aim_ref[...] = lb_im
    den = lam_re * lam_re + lam_im * lam_im
    num_re = lb_re - 1.0
    f_re = (num_re * lam_re + lb_im * lam_im) / den
    f_im = (lb_im * lam_re - num_re * lam_im) / den
    for c in range(S5_CHUNKS):
        fr = f_re[:, c * 512:(c + 1) * 512]
        fi = f_im[:, c * 512:(c + 1) * 512]
        br = bre_ref[c]
        bi = bim_ref[c]
        bcat_ref[c, :, 0:512] = (fr * br - fi * bi).astype(BF16)
        bcat_ref[c, :, 512:1024] = (fr * bi + fi * br).astype(BF16)


def _s5_prep(lam_re, lam_im, log_dt, b_re, b_im):
    def blockdiag(b):
        b4 = b.reshape(S5_CHUNKS, 8, S5_P, S5_H).transpose(0, 1, 3, 2)
        eye = jnp.eye(8, dtype=F32)
        full = b4[:, :, :, None, :] * eye[None, :, None, :, None]
        return full.reshape(S5_CHUNKS, 8 * S5_H, 8 * S5_P)

    return pl.pallas_call(
        _s5_prep_kernel,
        out_shape=(jax.ShapeDtypeStruct((1, S5_STATE), F32),
                   jax.ShapeDtypeStruct((1, S5_STATE), F32),
                   jax.ShapeDtypeStruct((S5_CHUNKS, 128, 1024), BF16)),
        name="s5_prep",
    )(lam_re.reshape(1, S5_STATE), lam_im.reshape(1, S5_STATE),
      jnp.repeat(log_dt, S5_P).reshape(1, S5_STATE), blockdiag(b_re), blockdiag(b_im))


def _mixer0_kernel(x_ref, win_ref, bcat_ref, are_ref, aim_ref, ccat_ref, d_ref,
                   wglu_ref, poolw_ref, pscale_ref, o_ref, bus, hst, pe, hbm, htm, otm):
    i = pl.program_id(0)

    @pl.when(i == 0)
    def _():
        hst[...] = jnp.zeros_like(hst)
        pe[0:POOL_HALO, :] = jnp.zeros((POOL_HALO, D_A), F32)

    h_bm = jnp.dot(x_ref[...].reshape(R0, D_MODEL).astype(BF16), win_ref[...],
                   preferred_element_type=F32)
    for c in range(LANE_TILES):
        hbm[c] = h_bm[:, c * 128:(c + 1) * 128]
    for t in range(TS0):
        for c in range(LANE_TILES):
            htm[t * BATCH:(t + 1) * BATCH, c * 128:(c + 1) * 128] = (
                hbm[c, pl.ds(t, BATCH, stride=TS0), :])
    h = htm[...]
    u = h[:, :D_A]
    v = h[:, D_A:]
    ub = u.astype(BF16)

    for c in range(S5_CHUNKS):
        bus[:, c * 1024:(c + 1) * 1024] = jnp.dot(
            ub[:, c * 128:(c + 1) * 128], bcat_ref[c], preferred_element_type=F32)

    for c in range(S5_CHUNKS):
        re_cols = slice(c * 1024, c * 1024 + 512)
        im_cols = slice(c * 1024 + 512, (c + 1) * 1024)
        ar = jnp.broadcast_to(are_ref[:, c * 512:(c + 1) * 512], (BATCH, 512))
        ai = jnp.broadcast_to(aim_ref[:, c * 512:(c + 1) * 512], (BATCH, 512))

        def step(t, carry, re_cols=re_cols, im_cols=im_cols, ar=ar, ai=ai):
            hr, hi = carry
            rows = pl.ds(pl.multiple_of(t * BATCH, BATCH), BATCH)
            nr = ar * hr - ai * hi + bus[rows, re_cols]
            ni = ar * hi + ai * hr + bus[rows, im_cols]
            bus[rows, re_cols] = nr
            bus[rows, im_cols] = ni
            return nr, ni

        hr, hi = lax.fori_loop(0, TS0, step, (hst[:, re_cols], hst[:, im_cols]),
                               unroll=True)
        hst[:, re_cols] = hr
        hst[:, im_cols] = hi

    ys = [jnp.dot(bus[:, c * 1024:(c + 1) * 1024].astype(BF16), ccat_ref[c],
                  preferred_element_type=F32) for c in range(S5_CHUNKS)]
    y = jnp.concatenate(ys, axis=1) + d_ref[...] * u
    y = 0.5 * y * (1.0 + jnp.tanh(math.sqrt(2.0 / math.pi) * (y + 0.044715 * (y * y * y))))
    ga = y * _sigmoid(jnp.dot(y.astype(BF16), wglu_ref[...], preferred_element_type=F32))
    for c in range(D_A // 128):
        otm[c] = ga[:, c * 128:(c + 1) * 128]

    pe[POOL_HALO:, :] = v
    t_glob = lax.shift_right_logical(
        lax.broadcasted_iota(I32, (R0, 1), 0), int(math.log2(BATCH))) + i * TS0
    for gi, w in enumerate(POOL_WINDOWS):
        cols = slice(gi * POOL_C, (gi + 1) * POOL_C)
        s = pe[:, cols]
        off = BATCH
        while off < BATCH * w:
            s = s[off:] + s[:-off]
            off *= 2
        s = s[POOL_HALO - BATCH * (w - 1):]
        cnt = jnp.minimum(t_glob + 1, w).astype(F32)
        pooled = s / cnt - v[:, cols]
        mixed = jnp.dot(pooled.astype(BF16), poolw_ref[gi], preferred_element_type=F32)
        otm[D_A // 128 + gi] = mixed * pscale_ref[:, cols]
    pe[0:POOL_HALO, :] = pe[R0:R0 + POOL_HALO, :]
    for b in range(BATCH):
        for c in range(LANE_TILES):
            o_ref[b, :, c * 128:(c + 1) * 128] = (
                otm[c, pl.ds(b, TS0, stride=BATCH), :].astype(BF16))


def _mixer0(x, win, bcat, a_re, a_im, ccat, dskip, wglu, poolw, pscale):
    full = lambda shape: pl.BlockSpec(shape, lambda i: (0,) * len(shape))
    return pl.pallas_call(
        _mixer0_kernel,
        out_shape=jax.ShapeDtypeStruct((BATCH, SEQ, D_MODEL), BF16),
        grid=(SEQ // TS0,),
        in_specs=[
            pl.BlockSpec((BATCH, TS0, D_MODEL), lambda i: (0, i, 0)),
            full((D_MODEL, D_MODEL)),
            full((S5_CHUNKS, 128, 1024)),
            full((1, S5_STATE)),
            full((1, S5_STATE)),
            full((S5_CHUNKS, 1024, 128)),
            full((1, D_A)),
            full((D_A, D_A)),
            full((4, POOL_C, POOL_C)),
            full((1, D_A)),
        ],
        out_specs=pl.BlockSpec((BATCH, TS0, D_MODEL), lambda i: (0, i, 0)),
        scratch_shapes=[
            pltpu.VMEM((R0, 2 * S5_STATE), F32),
            pltpu.VMEM((BATCH, 2 * S5_STATE), F32),
            pltpu.VMEM((POOL_HALO + R0, D_A), F32),
            pltpu.VMEM((LANE_TILES, R0, 128), F32),
            pltpu.VMEM((R0, D_MODEL), F32),
            pltpu.VMEM((LANE_TILES, R0, 128), F32),
        ],
        compiler_params=pltpu.CompilerParams(
            dimension_semantics=("arbitrary",), vmem_limit_bytes=VMEM_LIMIT),
        name="mixer0",
    )(x, win, bcat, a_re, a_im, ccat, dskip, wglu, poolw, pscale)


def _router_scores(x1, rwh_ref, rwl_ref):
    xh = x1.astype(BF16)
    xl = (x1 - xh.astype(F32)).astype(BF16)
    rwh = rwh_ref[...]
    logits = (lax.dot_general(rwh, xh, NT_DIMS, preferred_element_type=F32)
              + lax.dot_general(rwh, xl, NT_DIMS, preferred_element_type=F32)
              + lax.dot_general(rwl_ref[...], xh, NT_DIMS, preferred_element_type=F32))
    return _sigmoid(logits)


def _route(scores, rb_ref):
    n = scores.shape[1]
    sel = scores + rb_ref[...]
    row = lambda a, e: a[e:e + 1, :]

    best = None
    grp = None
    for g in range(N_EXPERTS // EXPERTS_PER_GROUP):
        m = [row(sel, EXPERTS_PER_GROUP * g + k) for k in range(EXPERTS_PER_GROUP)]
        gs = None
        for a in range(EXPERTS_PER_GROUP):
            for b in range(a + 1, EXPERTS_PER_GROUP):
                pair = m[a] + m[b]
                gs = pair if gs is None else jnp.maximum(gs, pair)
        if best is None:
            best, grp = gs, jnp.zeros(gs.shape, I32)
        else:
            better = gs > best
            grp = jnp.where(better, g, grp)
            best = jnp.where(better, gs, best)

    def pick(a, k):
        out = row(a, k)
        for g in range(1, N_EXPERTS // EXPERTS_PER_GROUP):
            out = jnp.where(grp == g, row(a, EXPERTS_PER_GROUP * g + k), out)
        return out

    cs = [pick(sel, k) for k in range(EXPERTS_PER_GROUP)]
    ss = [pick(scores, k) for k in range(EXPERTS_PER_GROUP)]
    m1, i1, s1 = cs[0], jnp.zeros(cs[0].shape, I32), ss[0]
    for k in range(1, EXPERTS_PER_GROUP):
        better = cs[k] > m1
        i1 = jnp.where(better, k, i1)
        s1 = jnp.where(better, ss[k], s1)
        m1 = jnp.where(better, cs[k], m1)
    m2 = jnp.full(m1.shape, -jnp.inf, F32)
    i2 = jnp.zeros(m1.shape, I32)
    s2 = jnp.zeros(m1.shape, F32)
    for k in range(EXPERTS_PER_GROUP):
        better = jnp.logical_and(i1 != k, cs[k] > m2)
        i2 = jnp.where(better, k, i2)
        s2 = jnp.where(better, ss[k], s2)
        m2 = jnp.where(better, cs[k], m2)
    tot = s1 + s2
    w0 = s1 / tot
    w1 = s2 / tot
    lo = jnp.minimum(i1, i2)
    hi = jnp.maximum(i1, i2)
    k = jnp.zeros(lo.shape, I32)
    in_a = jnp.zeros(lo.shape, I32)
    for kk, (a, b) in enumerate(zip(PAIR_A, PAIR_B)):
        hit = jnp.logical_and(lo == min(a, b), hi == max(a, b))
        k = jnp.where(hit, kk, k)
        in_a = jnp.where(hit, a, in_a)
    combo = grp * PAIRS_PER_GROUP + k
    first_in_a = i1 == in_a
    wa = jnp.where(first_in_a, w0, w1)
    wb = jnp.where(first_in_a, w1, w0)
    ri = jnp.concatenate([combo, jnp.zeros((7, n), I32)], axis=0)
    wslab = jnp.concatenate([wa, wb, jnp.zeros((126, n), F32)], axis=0)
    return ri, wslab.T


def _proj_ln_route_kernel(a_ref, w_ref, x_ref, g_ref, b_ref, rwh_ref, rwl_ref, rb_ref,
                          x1_ref, ri_ref, mixbuf):
    i = pl.program_id(0)

    @pl.when(i == 0)
    def _():
        mixbuf[1] = jnp.zeros((RP, D_MODEL), F32)

    for par in range(2):
        @pl.when(lax.rem(i, 2) == par)
        def _(par=par):
            mixbuf[par] = jnp.dot(a_ref[...], w_ref[...], preferred_element_type=F32)
            x1 = _layer_norm(ALPHA * x_ref[...] + mixbuf[1 - par], g_ref[...], b_ref[...])
            x1_ref[:, 0:D_MODEL] = x1
            scores = _router_scores(x1, rwh_ref, rwl_ref)
            ri, wcol = _route(scores, rb_ref)
            ri_ref[...] = ri
            x1_ref[:, D_MODEL:XW] = wcol


def _proj_ln_route(a, w, xres, g, b, rwh, rwl, rb):
    full = lambda shape: pl.BlockSpec(shape, lambda i: (0,) * len(shape))
    n_tiles = TOKENS // RP
    done = lambda i: jnp.maximum(i - 1, 0)
    return pl.pallas_call(
        _proj_ln_route_kernel,
        out_shape=(jax.ShapeDtypeStruct((TOKENS, XW), F32),
                   jax.ShapeDtypeStruct((8, TOKENS), I32)),
        grid=(n_tiles + 1,),
        in_specs=[
            pl.BlockSpec((RP, D_MODEL), lambda i: (jnp.minimum(i, n_tiles - 1), 0)),
            full((D_MODEL, D_MODEL)),
            pl.BlockSpec((RP, D_MODEL), lambda i: (done(i), 0)),
            full((1, D_MODEL)),
            full((1, D_MODEL)),
            full((N_EXPERTS, D_MODEL)),
            full((N_EXPERTS, D_MODEL)),
            full((N_EXPERTS, 1)),
        ],
        out_specs=(pl.BlockSpec((RP, XW), lambda i: (done(i), 0)),
                   pl.BlockSpec((8, RP), lambda i: (0, done(i)))),
        scratch_shapes=[pltpu.VMEM((2, RP, D_MODEL), F32)],
        compiler_params=pltpu.CompilerParams(
            dimension_semantics=("arbitrary",), vmem_limit_bytes=VMEM_LIMIT),
        name="proj_ln_route",
    )(a, w, xres, g, b, rwh, rwl, rb)


def _rank_kernel(ri_ref, pos_ref, te_ref, seg_ref, cnt, offs, tot):
    ph = pl.program_id(0)
    i = pl.program_id(1)
    n_tiles = pl.num_programs(1)
    s_iota = lax.broadcasted_iota(I32, (SEG_ROWS, RR), 0)
    hit = s_iota == ri_ref[0:1, :]
    mask = jnp.where(hit, 1.0, 0.0)
    tile_cnt = jnp.sum(mask, axis=1, keepdims=True)

    @pl.when(jnp.logical_and(ph == 0, i == 0))
    def _():
        cnt[...] = jnp.zeros_like(cnt)

    @pl.when(ph == 0)
    def _():
        cnt[...] = cnt[...] + tile_cnt
        pos_ref[...] = jnp.zeros_like(pos_ref)

    @pl.when(jnp.logical_and(ph == 0, i == n_tiles - 1))
    def _():
        c = cnt[...]
        padded = jnp.ceil(c * (1.0 / TM)) * TM
        sub = lax.broadcasted_iota(I32, (SEG_ROWS, 128), 0)
        acc = jnp.zeros((SEG_ROWS, 128), F32)
        for e in range(N_SEG):
            acc = acc + jnp.where(sub > e, padded[e:e + 1, :], 0.0)
        offs[...] = acc
        tot[...] = acc + padded
        cnt[...] = jnp.zeros_like(cnt)

    @pl.when(ph == 1)
    def _():
        r = lax.broadcasted_iota(I32, (RR, RR), 0)
        cidx = lax.broadcasted_iota(I32, (RR, RR), 1)
        tri = jnp.where(r < cidx, 1.0, 0.0).astype(BF16)
        prefix = jnp.dot(mask.astype(BF16), tri, preferred_element_type=F32)
        slot = prefix + cnt[:, 0:1] + offs[:, 0:1]
        p0 = jnp.sum(jnp.where(hit, slot, 0.0), axis=0, keepdims=True)
        pos_ref[...] = jnp.concatenate([p0.astype(I32), jnp.zeros((7, RR), I32)], axis=0)
        cnt[...] = cnt[...] + tile_cnt

    @pl.when(jnp.logical_and(ph == 1, i == n_tiles - 1))
    def _():
        ends = tot[:, 0:1]
        start = (lax.broadcasted_iota(I32, (SEG_ROWS, TE_LANES), 1) * TM).astype(F32)
        te = jnp.sum(jnp.where(start >= ends, 1.0, 0.0), axis=0, keepdims=True)
        te = jnp.minimum(te, N_SEG - 1.0).astype(I32)
        n_used = (tot[SEG_ROWS - 1:SEG_ROWS, :] * (1.0 / TM)).astype(I32)
        n_used = jnp.concatenate([n_used, n_used], axis=1)
        te_ref[...] = jnp.concatenate(
            [te, n_used, jnp.zeros((6, TE_LANES), I32)], axis=0)
        seg_ref[0:SEG_ROWS, :] = (offs[...] + cnt[...]).astype(I32)
        seg_ref[SEG_ROWS:2 * SEG_ROWS, :] = tot[...].astype(I32)


def _rank(ri):
    return pl.pallas_call(
        _rank_kernel,
        out_shape=(jax.ShapeDtypeStruct((8, TOKENS), I32),
                   jax.ShapeDtypeStruct((8, TE_LANES), I32),
                   jax.ShapeDtypeStruct((2 * SEG_ROWS, 128), I32)),
        grid=(2, TOKENS // RR),
        in_specs=[pl.BlockSpec((8, RR), lambda p, i: (0, i))],
        out_specs=(pl.BlockSpec((8, RR), lambda p, i: (0, i * p)),
                   pl.BlockSpec((8, TE_LANES), lambda p, i: (0, 0)),
                   pl.BlockSpec((2 * SEG_ROWS, 128), lambda p, i: (0, 0))),
        scratch_shapes=[pltpu.VMEM((SEG_ROWS, 128), F32),
                        pltpu.VMEM((SEG_ROWS, 128), F32),
                        pltpu.VMEM((SEG_ROWS, 128), F32)],
        compiler_params=pltpu.CompilerParams(
            dimension_semantics=("arbitrary", "arbitrary")),
        name="moe_rank",
    )(ri)


def _expert_kernel(tea_ref, teb_ref, nu_ref, pos_ref, seg_ref, x_hbm,
                   w1a_ref, w3a_ref, w2a_ref, w1b_ref, w3b_ref, w2b_ref, y_hbm,
                   wa1, wa3, wa2, wb1, wb3, wb2, xbuf, ybuf, dst, gsem, ssem):
    i = pl.program_id(0)
    n_used = nu_ref[0]
    b0 = lax.rem(i, N_BUF)
    b1 = lax.rem(i + 1, N_BUF)
    b2 = lax.rem(i + 2, N_BUF)

    def gather_row(slot, buf, r):
        tok = dst[slot] & (TOKENS - 1)
        return pltpu.make_async_copy(x_hbm.at[pl.ds(tok, 1)], xbuf.at[buf, pl.ds(r, 1)],
                                     gsem.at[buf])

    def scatter_row(slot, buf, r):
        return pltpu.make_async_copy(ybuf.at[buf, pl.ds(r, 1)], y_hbm.at[pl.ds(dst[slot], 1)],
                                     ssem.at[buf])

    def wait_rows(row_copy, buf):
        for r in range(TM):
            row_copy(0, buf, 0).wait()

    def compute(buf, copies):
        def issue(stage):
            lo, hi = stage * TM // ISSUE_STAGES, (stage + 1) * TM // ISSUE_STAGES
            for row_copy, tile, cbuf in copies:
                for r in range(lo, hi):
                    row_copy(tile * TM + r, cbuf, r).start(priority=r % 2)

        xf = xbuf[buf]
        xb = xf[:, 0:D_MODEL].astype(BF16)
        issue(0)
        h1a = jnp.dot(xb, wa1[...], preferred_element_type=F32)
        issue(1)
        h3a = jnp.dot(xb, wa3[...], preferred_element_type=F32)
        issue(2)
        h1b = jnp.dot(xb, wb1[...], preferred_element_type=F32)
        issue(3)
        h3b = jnp.dot(xb, wb3[...], preferred_element_type=F32)
        issue(4)
        ha = ((h1a * _sigmoid(h1a)) * h3a * xf[:, D_MODEL:D_MODEL + 1]).astype(BF16)
        hb = ((h1b * _sigmoid(h1b)) * h3b * xf[:, D_MODEL + 1:D_MODEL + 2]).astype(BF16)
        issue(5)
        ya = jnp.dot(ha, wa2[...], preferred_element_type=F32)
        issue(6)
        ybuf[buf] = ya + jnp.dot(hb, wb2[...], preferred_element_type=F32)
        issue(7)

    @pl.when(i == 0)
    def _():
        ybuf[1] = jnp.zeros((TM, D_MODEL), F32)
        fills = [pltpu.make_async_copy(
            ybuf.at[1], y_hbm.at[pl.ds(TOKENS + e * TM, TM)], ssem.at[1])
            for e in range(N_SEG)]
        for fill in fills:
            fill.start()
        for fill in fills:
            fill.wait()

        for e in range(N_SEG):
            def pad_body(s, carry, e=e):
                dst[s] = TOKENS + e * TM + (s - seg_ref[e])
                return carry
            lax.fori_loop(seg_ref[e], seg_ref[SEG_ROWS + e], pad_body, 0)

        def inv_body(t, carry):
            dst[pos_ref[t]] = t
            return carry
        lax.fori_loop(0, TOKENS, inv_body, 0, unroll=8)

        def first_start(r, carry):
            gather_row(r, 0, r).start()
            gather_row(TM + r, 1, r).start()
            return carry
        lax.fori_loop(0, TM, first_start, 0, unroll=8)

        def first_wait(r, carry):
            gather_row(0, 0, 0).wait()
            return carry
        lax.fori_loop(0, TM, first_wait, 0, unroll=8)

    prev = jnp.maximum(i - 1, 0)

    @pl.when(jnp.logical_and(i < n_used,
                             jnp.logical_or(i == 0, tea_ref[i] != tea_ref[prev])))
    def _():
        wa1[...] = w1a_ref[...].astype(BF16)
        wa3[...] = w3a_ref[...].astype(BF16)
        wa2[...] = w2a_ref[...].astype(BF16)

    @pl.when(jnp.logical_and(i < n_used,
                             jnp.logical_or(i == 0, teb_ref[i] != teb_ref[prev])))
    def _():
        wb1[...] = w1b_ref[...].astype(BF16)
        wb3[...] = w3b_ref[...].astype(BF16)
        wb2[...] = w2b_ref[...].astype(BF16)

    ahead = jnp.minimum(i + 2, n_used - 1)

    @pl.when(i == 0)
    def _():
        compute(0, [(gather_row, ahead, 2)])
        wait_rows(gather_row, 1)

    @pl.when(i == 1)
    def _():
        compute(1, [(gather_row, ahead, 0), (scatter_row, 0, 0)])
        wait_rows(gather_row, 2)

    for m in range(N_BUF):
        m1, m2 = (m + 1) % N_BUF, (m + 2) % N_BUF

        @pl.when(jnp.logical_and(jnp.logical_and(i >= 2, i < n_used), b0 == m))
        def _(m=m, m1=m1, m2=m2):
            compute(m, [(gather_row, ahead, m2), (scatter_row, i - 1, m2)])
            wait_rows(gather_row, m1)
            wait_rows(scatter_row, m1)

    @pl.when(i == n_used)
    def _():
        def last_start(r, carry):
            scatter_row((i - 1) * TM + r, b2, r).start()
            return carry
        lax.fori_loop(0, TM, last_start, 0, unroll=8)
        wait_rows(gather_row, b1)
        wait_rows(scatter_row, b1)

    @pl.when(i == n_used + 1)
    def _():
        wait_rows(scatter_row, b1)


def _experts(tea, teb, nu, pos, seg, x1, w1, w3, w2, layer):
    def wspec(r, c, slot):
        return pl.BlockSpec(
            (None, None, r, c),
            lambda i, tea, teb, nu, pos, seg: (layer, (tea, teb)[slot][i], 0, 0))

    return pl.pallas_call(
        _expert_kernel,
        out_shape=jax.ShapeDtypeStruct((N_SLOTS, D_MODEL), F32),
        grid_spec=pltpu.PrefetchScalarGridSpec(
            num_scalar_prefetch=5,
            grid=(N_MTILES + N_BUF - 1,),
            in_specs=[pl.BlockSpec(memory_space=pl.ANY),
                      wspec(D_MODEL, D_EXPERT, 0), wspec(D_MODEL, D_EXPERT, 0),
                      wspec(D_EXPERT, D_MODEL, 0),
                      wspec(D_MODEL, D_EXPERT, 1), wspec(D_MODEL, D_EXPERT, 1),
                      wspec(D_EXPERT, D_MODEL, 1)],
            out_specs=pl.BlockSpec(memory_space=pl.ANY),
            scratch_shapes=[pltpu.VMEM((D_MODEL, D_EXPERT), BF16),
                            pltpu.VMEM((D_MODEL, D_EXPERT), BF16),
                            pltpu.VMEM((D_EXPERT, D_MODEL), BF16),
                            pltpu.VMEM((D_MODEL, D_EXPERT), BF16),
                            pltpu.VMEM((D_MODEL, D_EXPERT), BF16),
                            pltpu.VMEM((D_EXPERT, D_MODEL), BF16),
                            pltpu.VMEM((N_BUF, TM, XW), F32),
                            pltpu.VMEM((N_BUF, TM, D_MODEL), F32),
                            pltpu.SMEM((N_SLOTS,), I32),
                            pltpu.SemaphoreType.DMA((N_BUF,)),
                            pltpu.SemaphoreType.DMA((N_BUF,))],
        ),
        compiler_params=pltpu.CompilerParams(
            dimension_semantics=("arbitrary",), vmem_limit_bytes=VMEM_LIMIT,
            has_side_effects=True),
        name="moe_experts",
    )(tea, teb, nu, pos, seg, x1, w1, w3, w2, w1, w3, w2)


def _combine_kernel(x1_ref, p_ref, y_ref, wp_ref, wg_ref, g_ref, b_ref, o_ref):
    x1 = x1_ref[...]
    gate = _sigmoid(jnp.dot(x1.astype(BF16), wg_ref[...], preferred_element_type=F32))
    ple = jnp.dot(p_ref[...].astype(BF16), wp_ref[...], preferred_element_type=F32) * gate
    o_ref[...] = _layer_norm(ALPHA * x1 + y_ref[...] + ple, g_ref[...], b_ref[...])


def _combine(x1, p, ys, wp, wg, g, b):
    full = lambda shape: pl.BlockSpec(shape, lambda i: (0,) * len(shape))
    return pl.pallas_call(
        _combine_kernel,
        out_shape=jax.ShapeDtypeStruct((TOKENS, D_MODEL), F32),
        grid=(TOKENS // RD,),
        in_specs=[pl.BlockSpec((RD, D_MODEL), lambda i: (i, 0)),
                  pl.BlockSpec((RD, PLE_DIM), lambda i: (i, 0)),
                  pl.BlockSpec((RD, D_MODEL), lambda i: (i, 0)),
                  full((PLE_DIM, D_MODEL)),
                  full((D_MODEL, D_MODEL)),
                  full((1, D_MODEL)),
                  full((1, D_MODEL))],
        out_specs=pl.BlockSpec((RD, D_MODEL), lambda i: (i, 0)),
        compiler_params=pltpu.CompilerParams(
            dimension_semantics=("arbitrary",), vmem_limit_bytes=VMEM_LIMIT),
        name="moe_combine",
    )(x1, p, ys, wp, wg, g, b)


def _qkv_kernel(x_ref, w_ref, o_ref):
    o_ref[...] = jnp.dot(x_ref[...].astype(BF16), w_ref[...],
                         preferred_element_type=F32).astype(BF16)


def _qkv_proj(x, w):
    return pl.pallas_call(
        _qkv_kernel,
        out_shape=jax.ShapeDtypeStruct((TOKENS, 3 * D_MODEL), BF16),
        grid=(TOKENS // RP,),
        in_specs=[pl.BlockSpec((RP, D_MODEL), lambda i: (i, 0)),
                  pl.BlockSpec((D_MODEL, 3 * D_MODEL), lambda i: (0, 0))],
        out_specs=pl.BlockSpec((RP, 3 * D_MODEL), lambda i: (i, 0)),
        compiler_params=pltpu.CompilerParams(
            dimension_semantics=("arbitrary",), vmem_limit_bytes=VMEM_LIMIT),
        name="qkv_proj",
    )(x, w)


def _attn_kernel(q_ref, k_ref, v_ref, o_ref, acc, cbuf, bnd, kinf):
    qb = pl.program_id(2)
    lane = lax.broadcasted_iota(I32, (1, 128), 1)
    half = (lane < 64, lane >= 64)

    @pl.when(qb == 0)
    def _():
        for pr in range(ATT_PAIRS):
            ka = jnp.max(jnp.abs(k_ref[:, pr * 128:(pr + 1) * 128].astype(F32)),
                         axis=0, keepdims=True)
            for hh in range(2):
                m = jnp.max(jnp.where(half[hh], ka, 0.0), axis=1, keepdims=True)
                kinf[2 * pr + hh] = jnp.broadcast_to(m, (8, 128))

    qh = []
    for pr in range(ATT_PAIRS):
        q = q_ref[:, pr * 128:(pr + 1) * 128] * jnp.asarray(0.125, BF16)
        zero = jnp.zeros_like(q)
        qh.append((jnp.where(half[0], q, zero), jnp.where(half[1], q, zero)))
        qa = jnp.abs(q.astype(F32))
        for hh in range(2):
            qn = jnp.sum(jnp.where(half[hh], qa, 0.0), axis=1, keepdims=True)
            bnd[2 * pr + hh] = (jnp.broadcast_to(qn, (SB_BLOCK, 128))
                                * kinf[2 * pr + hh][0:1, :] * BOUND_SLACK)
    r = lax.broadcasted_iota(I32, (SB_BLOCK, SB_BLOCK), 0)
    cidx = lax.broadcasted_iota(I32, (SB_BLOCK, SB_BLOCK), 1)
    neg_suffix = jnp.where(r >= cidx, -1.0, 0.0).astype(BF16)
    causal = cidx < r

    acc[...] = jnp.zeros_like(acc)
    cbuf[...] = jnp.zeros_like(cbuf)

    def block(kb, masked):
        rows = pl.ds(pl.multiple_of(kb * SB_BLOCK, SB_BLOCK), SB_BLOCK)
        heads = [(pr, hh) for pr in range(ATT_PAIRS) for hh in range(2)]
        ks = [k_ref[rows, pr * 128:(pr + 1) * 128] for pr in range(ATT_PAIRS)]
        vs = [v_ref[rows, pr * 128:(pr + 1) * 128] for pr in range(ATT_PAIRS)]
        zs = [lax.dot_general(qh[pr][hh], ks[pr], NT_DIMS, preferred_element_type=F32)
              for pr, hh in heads]
        sps = []
        for z in zs:
            sp = jnp.maximum(z, 0.0) + jnp.log(1.0 + jnp.exp2(jnp.abs(z) * (-LOG2E)))
            if masked:
                sp = jnp.where(causal, sp, 0.0)
            sps.append(sp.astype(BF16))
        rss = []
        for idx, sp in enumerate(sps):
            rs = jnp.dot(sp, neg_suffix, preferred_element_type=F32)
            if not masked:
                c = cbuf[idx]
                rs = rs + jnp.concatenate([c, c], axis=1)
            rss.append(rs)
        for idx, (pr, hh) in enumerate(heads):
            w = jnp.exp(zs[idx] + rss[idx])
            if masked:
                w = jnp.where(causal, w, 0.0)
            acc[idx] = acc[idx] + jnp.dot(w.astype(BF16), vs[pr], preferred_element_type=F32)
            cbuf[idx] = jnp.broadcast_to(rss[idx][:, 0:1], (SB_BLOCK, 128))

    def log_weight_bound():
        m = cbuf[0] + bnd[0]
        for idx in range(1, 2 * ATT_PAIRS):
            m = jnp.maximum(m, cbuf[idx] + bnd[idx])
        return jnp.max(m)

    block(qb, True)

    def cond(carry):
        j, m = carry
        return jnp.logical_and(j < qb, m > EXIT_LOG_WEIGHT)

    def body(carry):
        j, _ = carry
        block(qb - 1 - j, False)
        return j + 1, log_weight_bound()

    lax.while_loop(cond, body, (jnp.int32(0), log_weight_bound()))
    for pr in range(ATT_PAIRS):
        o_ref[:, pr * 128:(pr + 1) * 128] = jnp.where(
            lane < 64, acc[2 * pr], acc[2 * pr + 1]).astype(BF16)


def _attention(qkv):
    n_hg = D_MODEL // ATT_LANES
    return pl.pallas_call(
        _attn_kernel,
        out_shape=jax.ShapeDtypeStruct((BATCH, SEQ, D_MODEL), BF16),
        grid=(BATCH, n_hg, SEQ // SB_BLOCK),
        in_specs=[pl.BlockSpec((None, SB_BLOCK, ATT_LANES), lambda b, h, i: (b, i, h)),
                  pl.BlockSpec((None, SEQ, ATT_LANES), lambda b, h, i: (b, 0, n_hg + h)),
                  pl.BlockSpec((None, SEQ, ATT_LANES), lambda b, h, i: (b, 0, 2 * n_hg + h))],
        out_specs=pl.BlockSpec((None, SB_BLOCK, ATT_LANES), lambda b, h, i: (b, i, h)),
        scratch_shapes=[pltpu.VMEM((2 * ATT_PAIRS, SB_BLOCK, 128), F32),
                        pltpu.VMEM((2 * ATT_PAIRS, SB_BLOCK, 128), F32),
                        pltpu.VMEM((2 * ATT_PAIRS, SB_BLOCK, 128), F32),
                        pltpu.VMEM((2 * ATT_PAIRS, 8, 128), F32)],
        compiler_params=pltpu.CompilerParams(
            dimension_semantics=("arbitrary", "arbitrary", "arbitrary"),
            vmem_limit_bytes=VMEM_LIMIT),
        name="sb_attention",
    )(qkv, qkv, qkv)


def _moe_ffn(x1, ri, p_rows, layer, moe_w1, moe_w3, moe_w2, wp, wg, g, b):
    pos, te, seg = _rank(ri)
    tile_seg = te[0, :N_MTILES + N_BUF - 1]
    grp = tile_seg // PAIRS_PER_GROUP
    k = tile_seg % PAIRS_PER_GROUP
    tea = grp * EXPERTS_PER_GROUP + sum(jnp.where(k == kk, a, 0) for kk, a in enumerate(PAIR_A))
    teb = grp * EXPERTS_PER_GROUP + sum(jnp.where(k == kk, b_, 0) for kk, b_ in enumerate(PAIR_B))
    ys = _experts(tea, teb, te[1, :1], pos[0], seg[:, 0], x1, moe_w1, moe_w3, moe_w2, layer)
    return _combine(x1, p_rows, ys, wp, wg, g, b)


def kernel(x, p, ab_w_in, s5_lambda_re, s5_lambda_im, s5_log_dt, s5_b_re, s5_b_im, s5_c_re, s5_c_im, s5_d, s5_w_glu, pool_w, pool_scale, ab_w_out, sb_w_qkv, sb_w_out, ln_mix_g, ln_mix_b, ln_ffn_g, ln_ffn_b, router_w, router_bias, moe_w1, moe_w3, moe_w2, ple_w_proj, ple_w_gate):
    row = lambda a: a.reshape(1, -1)
    rwt = router_w.T
    rwh = rwt.astype(BF16)
    rwl = (rwt - rwh.astype(F32)).astype(BF16)
    rb = router_bias.reshape(N_EXPERTS, 1)

    x_rows = x.reshape(TOKENS, D_MODEL)
    a_re, a_im, bcat = _s5_prep(s5_lambda_re[0], s5_lambda_im[0], s5_log_dt[0],
                                s5_b_re[0], s5_b_im[0])

    def c_blockdiag(c):
        c4 = c.reshape(S5_CHUNKS, 8, S5_H, S5_P).transpose(0, 1, 3, 2)
        eye = jnp.eye(8, dtype=F32)
        full = c4[:, :, :, None, :] * eye[None, :, None, :, None]
        return full.reshape(S5_CHUNKS, 8 * S5_P, 8 * S5_H)

    ccat = jnp.concatenate([c_blockdiag(s5_c_re[0]), -c_blockdiag(s5_c_im[0])],
                           axis=1).astype(BF16)
    mix_in = _mixer0(x, ab_w_in[0].astype(BF16), bcat, a_re, a_im, ccat,
                     row(s5_d[0]), s5_w_glu[0].astype(BF16), pool_w[0].astype(BF16),
                     row(pool_scale[0])).reshape(TOKENS, D_MODEL)
    x1, ri = _proj_ln_route(mix_in, ab_w_out[0].astype(BF16), x_rows,
                            row(ln_mix_g[0]), row(ln_mix_b[0]), rwh, rwl, rb)
    xb = _moe_ffn(x1, ri, p[0].reshape(TOKENS, PLE_DIM), 0, moe_w1, moe_w3, moe_w2,
                  ple_w_proj[0].astype(BF16), ple_w_gate[0].astype(BF16),
                  row(ln_ffn_g[0]), row(ln_ffn_b[0]))

    qkv = _qkv_proj(xb, sb_w_qkv[0].astype(BF16))
    att = _attention(qkv.reshape(BATCH, SEQ, 3 * D_MODEL)).reshape(TOKENS, D_MODEL)
    x3, ri = _proj_ln_route(att, sb_w_out[0].astype(BF16), xb,
                            row(ln_mix_g[1]), row(ln_mix_b[1]), rwh, rwl, rb)
    x4 = _moe_ffn(x3, ri, p[1].reshape(TOKENS, PLE_DIM), 1, moe_w1, moe_w3, moe_w2,
                  ple_w_proj[1].astype(BF16), ple_w_gate[1].astype(BF16),
                  row(ln_ffn_g[1]), row(ln_ffn_b[1]))
    return x4.reshape(BATCH, SEQ, D_MODEL)
```

```python
import functools
import math

import jax
import jax.numpy as jnp
from jax import lax
from jax.experimental import pallas as pl
from jax.experimental.pallas import tpu as pltpu

F32 = jnp.float32
BF16 = jnp.bfloat16
I32 = jnp.int32

D_MODEL = 1024
LANE_TILES = D_MODEL // 128
BATCH = 8
SEQ = 2048
DEPTH = 2
TOKENS = BATCH * SEQ

D_A = 512
S5_H = 16
S5_G = 32
S5_P = 64
S5_STATE = S5_G * S5_P
S5_CHUNKS = 4
POOL_WINDOWS = (2, 4, 8, 16)
POOL_C = 128
POOL_HALO = 16 * BATCH
SB_BLOCK = 256
ATT_PAIRS = 4
ATT_LANES = 128 * ATT_PAIRS
N_EXPERTS = 16
EXPERTS_PER_GROUP = 4
D_EXPERT = 512
PLE_DIM = 256
ALPHA = (2 * DEPTH) ** 0.25
LN_EPS = 1e-5

TS0 = 64
R0 = TS0 * BATCH
RP = 1024
RR = 1024
RD = 1024
TM = 256
N_BUF = 3
ISSUE_STAGES = 8
PAIRS_PER_GROUP = 6
PAIR_A = (0, 2, 2, 3, 3, 3)
PAIR_B = (1, 1, 0, 0, 1, 2)
N_SEG = (N_EXPERTS // EXPERTS_PER_GROUP) * PAIRS_PER_GROUP
SEG_ROWS = 32
XW = D_MODEL + 128
N_SLOTS = TOKENS + N_SEG * TM
N_MTILES = N_SLOTS // TM
TE_LANES = 256

VMEM_LIMIT = 52 * 1024 * 1024
NT_DIMS = (((1,), (1,)), ((), ()))
LOG2E = 1.0 / math.log(2.0)
EXIT_LOG_WEIGHT = -110.0
BOUND_SLACK = 1.01


def _sigmoid(x):
    return 1.0 / (1.0 + jnp.exp(-x))


def _layer_norm(x, g, b):
    mu = jnp.mean(x, axis=-1, keepdims=True)
    xc = x - mu
    var = jnp.mean(xc * xc, axis=-1, keepdims=True)
    return xc * lax.rsqrt(var + LN_EPS) * g + b


def _s5_prep_kernel(lre_ref, lim_ref, ldt_ref, bre_ref, bim_ref,
                    are_ref, aim_ref, bcat_ref):
    lam_re = lre_ref[...]
    lam_im = lim_ref[...]
    dt = jnp.exp(ldt_ref[...])
    mag = jnp.exp(lam_re * dt)
    ang = lam_im * dt
    lb_re = mag * jnp.cos(ang)
    lb_im = mag * jnp.sin(ang)
    are_ref[...] = lb_re
    aim_ref[...] = lb_im
    den = lam_re * lam_re + lam_im * lam_im
    num_re = lb_re - 1.0
    f_re = (num_re * lam_re + lb_im * lam_im) / den
    f_im = (lb_im * lam_re - num_re * lam_im) / den
    for c in range(S5_CHUNKS):
        fr = f_re[:, c * 512:(c + 1) * 512]
        fi = f_im[:, c * 512:(c + 1) * 512]
        br = bre_ref[c]
        bi = bim_ref[c]
        bcat_ref[c, :, 0:512] = (fr * br - fi * bi).astype(BF16)
        bcat_ref[c, :, 512:1024] = (fr * bi + fi * br).astype(BF16)


def _group_blockdiag(rows, n_in, n_out):
    tiled = jnp.tile(rows, (1, 1, 8))
    row_grp = lax.broadcasted_iota(I32, (8 * n_in, 8 * n_out), 0) // n_in
    col_grp = lax.broadcasted_iota(I32, (8 * n_in, 8 * n_out), 1) // n_out
    return jnp.where((row_grp == col_grp)[None], tiled, 0.0)


def _s5_prep(lam_re, lam_im, log_dt, b_re, b_im):
    def blockdiag(b):
        rows = b.reshape(S5_CHUNKS, 8, S5_P, S5_H).transpose(0, 1, 3, 2).reshape(
            S5_CHUNKS, 8 * S5_H, S5_P)
        return _group_blockdiag(rows, S5_H, S5_P)

    return pl.pallas_call(
        _s5_prep_kernel,
        out_shape=(jax.ShapeDtypeStruct((1, S5_STATE), F32),
                   jax.ShapeDtypeStruct((1, S5_STATE), F32),
                   jax.ShapeDtypeStruct((S5_CHUNKS, 128, 1024), BF16)),
        name="s5_prep",
    )(lam_re.reshape(1, S5_STATE), lam_im.reshape(1, S5_STATE),
      jnp.repeat(log_dt, S5_P).reshape(1, S5_STATE), blockdiag(b_re), blockdiag(b_im))


def _mixer0_kernel(x_ref, win_ref, bcat_ref, are_ref, aim_ref, ccat_ref, d_ref,
                   wglu_ref, poolw_ref, pscale_ref, o_ref, bus, hst, pe, hbm, htm, otm):
    i = pl.program_id(0)

    @pl.when(i == 0)
    def _():
        hst[...] = jnp.zeros_like(hst)
        pe[0:POOL_HALO, :] = jnp.zeros((POOL_HALO, D_A), F32)

    h_bm = jnp.dot(x_ref[...].reshape(R0, D_MODEL).astype(BF16), win_ref[...],
                   preferred_element_type=F32)
    for c in range(LANE_TILES):
        hbm[c] = h_bm[:, c * 128:(c + 1) * 128]
    for t in range(TS0):
        for c in range(LANE_TILES):
            htm[t * BATCH:(t + 1) * BATCH, c * 128:(c + 1) * 128] = (
                hbm[c, pl.ds(t, BATCH, stride=TS0), :])
    h = htm[...]
    u = h[:, :D_A]
    v = h[:, D_A:]
    ub = u.astype(BF16)

    for c in range(S5_CHUNKS):
        bus[:, c * 1024:(c + 1) * 1024] = jnp.dot(
            ub[:, c * 128:(c + 1) * 128], bcat_ref[c], preferred_element_type=F32)

    for c in range(S5_CHUNKS):
        re_cols = slice(c * 1024, c * 1024 + 512)
        im_cols = slice(c * 1024 + 512, (c + 1) * 1024)
        ar = jnp.broadcast_to(are_ref[:, c * 512:(c + 1) * 512], (BATCH, 512))
        ai = jnp.broadcast_to(aim_ref[:, c * 512:(c + 1) * 512], (BATCH, 512))

        def step(t, carry, re_cols=re_cols, im_cols=im_cols, ar=ar, ai=ai):
            hr, hi = carry
            rows = pl.ds(pl.multiple_of(t * BATCH, BATCH), BATCH)
            nr = ar * hr - ai * hi + bus[rows, re_cols]
            ni = ar * hi + ai * hr + bus[rows, im_cols]
            bus[rows, re_cols] = nr
            bus[rows, im_cols] = ni
            return nr, ni

        hr, hi = lax.fori_loop(0, TS0, step, (hst[:, re_cols], hst[:, im_cols]),
                               unroll=True)
        hst[:, re_cols] = hr
        hst[:, im_cols] = hi

    ys = [jnp.dot(bus[:, c * 1024:(c + 1) * 1024].astype(BF16), ccat_ref[c],
                  preferred_element_type=F32) for c in range(S5_CHUNKS)]
    y = jnp.concatenate(ys, axis=1) + d_ref[...] * u
    y = 0.5 * y * (1.0 + jnp.tanh(math.sqrt(2.0 / math.pi) * (y + 0.044715 * (y * y * y))))
    ga = y * _sigmoid(jnp.dot(y.astype(BF16), wglu_ref[...], preferred_element_type=F32))
    for c in range(D_A // 128):
        otm[c] = ga[:, c * 128:(c + 1) * 128]

    pe[POOL_HALO:, :] = v
    t_glob = lax.shift_right_logical(
        lax.broadcasted_iota(I32, (R0, 1), 0), int(math.log2(BATCH))) + i * TS0
    for gi, w in enumerate(POOL_WINDOWS):
        cols = slice(gi * POOL_C, (gi + 1) * POOL_C)
        s = pe[:, cols]
        off = BATCH
        while off < BATCH * w:
            s = s[off:] + s[:-off]
            off *= 2
        s = s[POOL_HALO - BATCH * (w - 1):]
        cnt = jnp.minimum(t_glob + 1, w).astype(F32)
        pooled = s / cnt - v[:, cols]
        mixed = jnp.dot(pooled.astype(BF16), poolw_ref[gi], preferred_element_type=F32)
        otm[D_A // 128 + gi] = mixed * pscale_ref[:, cols]
    pe[0:POOL_HALO, :] = pe[R0:R0 + POOL_HALO, :]
    for b in range(BATCH):
        for c in range(LANE_TILES):
            o_ref[b, :, c * 128:(c + 1) * 128] = (
                otm[c, pl.ds(b, TS0, stride=BATCH), :].astype(BF16))


def _mixer0(x, win, bcat, a_re, a_im, ccat, dskip, wglu, poolw, pscale):
    full = lambda shape: pl.BlockSpec(shape, lambda i: (0,) * len(shape))
    return pl.pallas_call(
        _mixer0_kernel,
        out_shape=jax.ShapeDtypeStruct((BATCH, SEQ, D_MODEL), BF16),
        grid=(SEQ // TS0,),
        in_specs=[
            pl.BlockSpec((BATCH, TS0, D_MODEL), lambda i: (0, i, 0)),
            full((D_MODEL, D_MODEL)),
            full((S5_CHUNKS, 128, 1024)),
            full((1, S5_STATE)),
            full((1, S5_STATE)),
            full((S5_CHUNKS, 1024, 128)),
            full((1, D_A)),
            full((D_A, D_A)),
            full((4, POOL_C, POOL_C)),
            full((1, D_A)),
        ],
        out_specs=pl.BlockSpec((BATCH, TS0, D_MODEL), lambda i: (0, i, 0)),
        scratch_shapes=[
            pltpu.VMEM((R0, 2 * S5_STATE), F32),
            pltpu.VMEM((BATCH, 2 * S5_STATE), F32),
            pltpu.VMEM((POOL_HALO + R0, D_A), F32),
            pltpu.VMEM((LANE_TILES, R0, 128), F32),
            pltpu.VMEM((R0, D_MODEL), F32),
            pltpu.VMEM((LANE_TILES, R0, 128), F32),
        ],
        compiler_params=pltpu.CompilerParams(
            dimension_semantics=("arbitrary",), vmem_limit_bytes=VMEM_LIMIT),
        name="mixer0",
    )(x, win, bcat, a_re, a_im, ccat, dskip, wglu, poolw, pscale)


def _router_scores(x1, rwh_ref, rwl_ref):
    xh = x1.astype(BF16)
    xl = (x1 - xh.astype(F32)).astype(BF16)
    rwh = rwh_ref[...]
    logits = (lax.dot_general(rwh, xh, NT_DIMS, preferred_element_type=F32)
              + lax.dot_general(rwh, xl, NT_DIMS, preferred_element_type=F32)
              + lax.dot_general(rwl_ref[...], xh, NT_DIMS, preferred_element_type=F32))
    return _sigmoid(logits)


def _route(scores, rb_ref):
    n = scores.shape[1]
    sel = scores + rb_ref[...]
    row = lambda a, e: a[e:e + 1, :]

    best = None
    grp = None
    for g in range(N_EXPERTS // EXPERTS_PER_GROUP):
        m = [row(sel, EXPERTS_PER_GROUP * g + k) for k in range(EXPERTS_PER_GROUP)]
        gs = None
        for a in range(EXPERTS_PER_GROUP):
            for b in range(a + 1, EXPERTS_PER_GROUP):
                pair = m[a] + m[b]
                gs = pair if gs is None else jnp.maximum(gs, pair)
        if best is None:
            best, grp = gs, jnp.zeros(gs.shape, I32)
        else:
            better = gs > best
            grp = jnp.where(better, g, grp)
            best = jnp.where(better, gs, best)

    def pick(a, k):
        out = row(a, k)
        for g in range(1, N_EXPERTS // EXPERTS_PER_GROUP):
            out = jnp.where(grp == g, row(a, EXPERTS_PER_GROUP * g + k), out)
        return out

    cs = [pick(sel, k) for k in range(EXPERTS_PER_GROUP)]
    ss = [pick(scores, k) for k in range(EXPERTS_PER_GROUP)]
    m1, i1, s1 = cs[0], jnp.zeros(cs[0].shape, I32), ss[0]
    for k in range(1, EXPERTS_PER_GROUP):
        better = cs[k] > m1
        i1 = jnp.where(better, k, i1)
        s1 = jnp.where(better, ss[k], s1)
        m1 = jnp.where(better, cs[k], m1)
    m2 = jnp.full(m1.shape, -jnp.inf, F32)
    i2 = jnp.zeros(m1.shape, I32)
    s2 = jnp.zeros(m1.shape, F32)
    for k in range(EXPERTS_PER_GROUP):
        better = jnp.logical_and(i1 != k, cs[k] > m2)
        i2 = jnp.where(better, k, i2)
        s2 = jnp.where(better, ss[k], s2)
        m2 = jnp.where(better, cs[k], m2)
    tot = s1 + s2
    w0 = s1 / tot
    w1 = s2 / tot
    lo = jnp.minimum(i1, i2)
    hi = jnp.maximum(i1, i2)
    k = jnp.zeros(lo.shape, I32)
    in_a = jnp.zeros(lo.shape, I32)
    for kk, (a, b) in enumerate(zip(PAIR_A, PAIR_B)):
        hit = jnp.logical_and(lo == min(a, b), hi == max(a, b))
        k = jnp.where(hit, kk, k)
        in_a = jnp.where(hit, a, in_a)
    combo = grp * PAIRS_PER_GROUP + k
    first_in_a = i1 == in_a
    wa = jnp.where(first_in_a, w0, w1)
    wb = jnp.where(first_in_a, w1, w0)
    ri = jnp.concatenate([combo, jnp.zeros((7, n), I32)], axis=0)
    wslab = jnp.concatenate([wa, wb, jnp.zeros((126, n), F32)], axis=0)
    return ri, wslab.T


def _proj_ln_route_kernel(a_ref, w_ref, x_ref, g_ref, b_ref, rwh_ref, rwl_ref, rb_ref,
                          x1_ref, ri_ref, mixbuf):
    i = pl.program_id(0)

    @pl.when(i == 0)
    def _():
        mixbuf[1] = jnp.zeros((RP, D_MODEL), F32)

    for par in range(2):
        @pl.when(lax.rem(i, 2) == par)
        def _(par=par):
            mixbuf[par] = jnp.dot(a_ref[...], w_ref[...], preferred_element_type=F32)
            x1 = _layer_norm(ALPHA * x_ref[...] + mixbuf[1 - par], g_ref[...], b_ref[...])
            x1_ref[:, 0:D_MODEL] = x1
            scores = _router_scores(x1, rwh_ref, rwl_ref)
            ri, wcol = _route(scores, rb_ref)
            ri_ref[...] = ri
            x1_ref[:, D_MODEL:XW] = wcol


def _proj_ln_route(a, w, xres, g, b, rwh, rwl, rb):
    full = lambda shape: pl.BlockSpec(shape, lambda i: (0,) * len(shape))
    n_tiles = TOKENS // RP
    done = lambda i: jnp.maximum(i - 1, 0)
    return pl.pallas_call(
        _proj_ln_route_kernel,
        out_shape=(jax.ShapeDtypeStruct((TOKENS, XW), F32),
                   jax.ShapeDtypeStruct((8, TOKENS), I32)),
        grid=(n_tiles + 1,),
        in_specs=[
            pl.BlockSpec((RP, D_MODEL), lambda i: (jnp.minimum(i, n_tiles - 1), 0)),
            full((D_MODEL, D_MODEL)),
            pl.BlockSpec((RP, D_MODEL), lambda i: (done(i), 0)),
            full((1, D_MODEL)),
            full((1, D_MODEL)),
            full((N_EXPERTS, D_MODEL)),
            full((N_EXPERTS, D_MODEL)),
            full((N_EXPERTS, 1)),
        ],
        out_specs=(pl.BlockSpec((RP, XW), lambda i: (done(i), 0)),
                   pl.BlockSpec((8, RP), lambda i: (0, done(i)))),
        scratch_shapes=[pltpu.VMEM((2, RP, D_MODEL), F32)],
        compiler_params=pltpu.CompilerParams(
            dimension_semantics=("arbitrary",), vmem_limit_bytes=VMEM_LIMIT),
        name="proj_ln_route",
    )(a, w, xres, g, b, rwh, rwl, rb)


def _rank_kernel(ri_ref, pos_ref, te_ref, seg_ref, cnt, offs, tot):
    ph = pl.program_id(0)
    i = pl.program_id(1)
    n_tiles = pl.num_programs(1)
    s_iota = lax.broadcasted_iota(I32, (SEG_ROWS, RR), 0)
    hit = s_iota == ri_ref[0:1, :]
    mask = jnp.where(hit, 1.0, 0.0)
    tile_cnt = jnp.sum(mask, axis=1, keepdims=True)

    @pl.when(jnp.logical_and(ph == 0, i == 0))
    def _():
        cnt[...] = jnp.zeros_like(cnt)

    @pl.when(ph == 0)
    def _():
        cnt[...] = cnt[...] + tile_cnt
        pos_ref[...] = jnp.zeros_like(pos_ref)

    @pl.when(jnp.logical_and(ph == 0, i == n_tiles - 1))
    def _():
        c = cnt[...]
        padded = jnp.ceil(c * (1.0 / TM)) * TM
        sub = lax.broadcasted_iota(I32, (SEG_ROWS, 128), 0)
        acc = jnp.zeros((SEG_ROWS, 128), F32)
        for e in range(N_SEG):
            acc = acc + jnp.where(sub > e, padded[e:e + 1, :], 0.0)
        offs[...] = acc
        tot[...] = acc + padded
        cnt[...] = jnp.zeros_like(cnt)

    @pl.when(ph == 1)
    def _():
        r = lax.broadcasted_iota(I32, (RR, RR), 0)
        cidx = lax.broadcasted_iota(I32, (RR, RR), 1)
        tri = jnp.where(r < cidx, 1.0, 0.0).astype(BF16)
        prefix = jnp.dot(mask.astype(BF16), tri, preferred_element_type=F32)
        slot = prefix + cnt[:, 0:1] + offs[:, 0:1]
        p0 = jnp.sum(jnp.where(hit, slot, 0.0), axis=0, keepdims=True)
        pos_ref[...] = jnp.concatenate([p0.astype(I32), jnp.zeros((7, RR), I32)], axis=0)
        cnt[...] = cnt[...] + tile_cnt

    @pl.when(jnp.logical_and(ph == 1, i == n_tiles - 1))
    def _():
        ends = tot[:, 0:1]
        start = (lax.broadcasted_iota(I32, (SEG_ROWS, TE_LANES), 1) * TM).astype(F32)
        te = jnp.sum(jnp.where(start >= ends, 1.0, 0.0), axis=0, keepdims=True)
        te = jnp.minimum(te, N_SEG - 1.0).astype(I32)
        n_used = (tot[SEG_ROWS - 1:SEG_ROWS, :] * (1.0 / TM)).astype(I32)
        n_used = jnp.concatenate([n_used, n_used], axis=1)
        te_ref[...] = jnp.concatenate(
            [te, n_used, jnp.zeros((6, TE_LANES), I32)], axis=0)
        seg_ref[0:SEG_ROWS, :] = (offs[...] + cnt[...]).astype(I32)
        seg_ref[SEG_ROWS:2 * SEG_ROWS, :] = tot[...].astype(I32)


def _rank(ri):
    return pl.pallas_call(
        _rank_kernel,
        out_shape=(jax.ShapeDtypeStruct((8, TOKENS), I32),
                   jax.ShapeDtypeStruct((8, TE_LANES), I32),
                   jax.ShapeDtypeStruct((2 * SEG_ROWS, 128), I32)),
        grid=(2, TOKENS // RR),
        in_specs=[pl.BlockSpec((8, RR), lambda p, i: (0, i))],
        out_specs=(pl.BlockSpec((8, RR), lambda p, i: (0, i * p)),
                   pl.BlockSpec((8, TE_LANES), lambda p, i: (0, 0)),
                   pl.BlockSpec((2 * SEG_ROWS, 128), lambda p, i: (0, 0))),
        scratch_shapes=[pltpu.VMEM((SEG_ROWS, 128), F32),
                        pltpu.VMEM((SEG_ROWS, 128), F32),
                        pltpu.VMEM((SEG_ROWS, 128), F32)],
        compiler_params=pltpu.CompilerParams(
            dimension_semantics=("arbitrary", "arbitrary")),
        name="moe_rank",
    )(ri)


def _expert_kernel(tea_ref, teb_ref, nu_ref, pos_ref, seg_ref, x_hbm,
                   w1a_ref, w3a_ref, w2a_ref, w1b_ref, w3b_ref, w2b_ref, y_hbm,
                   wa1, wa3, wa2, wb1, wb3, wb2, xbuf, ybuf, dst, gsem, ssem):
    i = pl.program_id(0)
    n_used = nu_ref[0]
    b0 = lax.rem(i, N_BUF)
    b1 = lax.rem(i + 1, N_BUF)
    b2 = lax.rem(i + 2, N_BUF)

    def gather_row(slot, buf, r):
        tok = dst[slot] & (TOKENS - 1)
        return pltpu.make_async_copy(x_hbm.at[pl.ds(tok, 1)], xbuf.at[buf, pl.ds(r, 1)],
                                     gsem.at[buf])

    def scatter_row(slot, buf, r):
        return pltpu.make_async_copy(ybuf.at[buf, pl.ds(r, 1)], y_hbm.at[pl.ds(dst[slot], 1)],
                                     ssem.at[buf])

    def wait_rows(row_copy, buf):
        for r in range(TM):
            row_copy(0, buf, 0).wait()

    def compute(buf, copies):
        def issue(stage):
            lo, hi = stage * TM // ISSUE_STAGES, (stage + 1) * TM // ISSUE_STAGES
            for row_copy, tile, cbuf in copies:
                for r in range(lo, hi):
                    row_copy(tile * TM + r, cbuf, r).start(priority=r % 2)

        xf = xbuf[buf]
        xb = xf[:, 0:D_MODEL].astype(BF16)
        issue(0)
        h1a = jnp.dot(xb, wa1[...], preferred_element_type=F32)
        issue(1)
        h3a = jnp.dot(xb, wa3[...], preferred_element_type=F32)
        issue(2)
        h1b = jnp.dot(xb, wb1[...], preferred_element_type=F32)
        issue(3)
        h3b = jnp.dot(xb, wb3[...], preferred_element_type=F32)
        issue(4)
        ha = ((h1a * _sigmoid(h1a)) * h3a * xf[:, D_MODEL:D_MODEL + 1]).astype(BF16)
        hb = ((h1b * _sigmoid(h1b)) * h3b * xf[:, D_MODEL + 1:D_MODEL + 2]).astype(BF16)
        issue(5)
        ya = jnp.dot(ha, wa2[...], preferred_element_type=F32)
        issue(6)
        ybuf[buf] = ya + jnp.dot(hb, wb2[...], preferred_element_type=F32)
        issue(7)

    @pl.when(i == 0)
    def _():
        ybuf[1] = jnp.zeros((TM, D_MODEL), F32)
        fills = [pltpu.make_async_copy(
            ybuf.at[1], y_hbm.at[pl.ds(TOKENS + e * TM, TM)], ssem.at[1])
            for e in range(N_SEG)]
        for fill in fills:
            fill.start()
        for fill in fills:
            fill.wait()

        for e in range(N_SEG):
            def pad_body(s, carry, e=e):
                dst[s] = TOKENS + e * TM + (s - seg_ref[e])
                return carry
            lax.fori_loop(seg_ref[e], seg_ref[SEG_ROWS + e], pad_body, 0)

        def inv_body(t, carry):
            dst[pos_ref[t]] = t
            return carry
        lax.fori_loop(0, TOKENS, inv_body, 0, unroll=8)

        def first_start(r, carry):
            gather_row(r, 0, r).start()
            gather_row(TM + r, 1, r).start()
            return carry
        lax.fori_loop(0, TM, first_start, 0, unroll=8)

        def first_wait(r, carry):
            gather_row(0, 0, 0).wait()
            return carry
        lax.fori_loop(0, TM, first_wait, 0, unroll=8)

    prev = jnp.maximum(i - 1, 0)

    @pl.when(jnp.logical_and(i < n_used,
                             jnp.logical_or(i == 0, tea_ref[i] != tea_ref[prev])))
    def _():
        wa1[...] = w1a_ref[...].astype(BF16)
        wa3[...] = w3a_ref[...].astype(BF16)
        wa2[...] = w2a_ref[...].astype(BF16)

    @pl.when(jnp.logical_and(i < n_used,
                             jnp.logical_or(i == 0, teb_ref[i] != teb_ref[prev])))
    def _():
        wb1[...] = w1b_ref[...].astype(BF16)
        wb3[...] = w3b_ref[...].astype(BF16)
        wb2[...] = w2b_ref[...].astype(BF16)

    ahead = jnp.minimum(i + 2, n_used - 1)

    @pl.when(i == 0)
    def _():
        compute(0, [(gather_row, ahead, 2)])
        wait_rows(gather_row, 1)

    @pl.when(i == 1)
    def _():
        compute(1, [(gather_row, ahead, 0), (scatter_row, 0, 0)])
        wait_rows(gather_row, 2)

    for m in range(N_BUF):
        m1, m2 = (m + 1) % N_BUF, (m + 2) % N_BUF

        @pl.when(jnp.logical_and(jnp.logical_and(i >= 2, i < n_used), b0 == m))
        def _(m=m, m1=m1, m2=m2):
            compute(m, [(gather_row, ahead, m2), (scatter_row, i - 1, m2)])
            wait_rows(gather_row, m1)
            wait_rows(scatter_row, m1)

    @pl.when(i == n_used)
    def _():
        def last_start(r, carry):
            scatter_row((i - 1) * TM + r, b2, r).start()
            return carry
        lax.fori_loop(0, TM, last_start, 0, unroll=8)
        wait_rows(gather_row, b1)
        wait_rows(scatter_row, b1)

    @pl.when(i == n_used + 1)
    def _():
        wait_rows(scatter_row, b1)


def _experts(tea, teb, nu, pos, seg, x1, w1, w3, w2, layer):
    def wspec(r, c, slot):
        return pl.BlockSpec(
            (None, None, r, c),
            lambda i, tea, teb, nu, pos, seg: (layer, (tea, teb)[slot][i], 0, 0))

    return pl.pallas_call(
        _expert_kernel,
        out_shape=jax.ShapeDtypeStruct((N_SLOTS, D_MODEL), F32),
        grid_spec=pltpu.PrefetchScalarGridSpec(
            num_scalar_prefetch=5,
            grid=(N_MTILES + N_BUF - 1,),
            in_specs=[pl.BlockSpec(memory_space=pl.ANY),
                      wspec(D_MODEL, D_EXPERT, 0), wspec(D_MODEL, D_EXPERT, 0),
                      wspec(D_EXPERT, D_MODEL, 0),
                      wspec(D_MODEL, D_EXPERT, 1), wspec(D_MODEL, D_EXPERT, 1),
                      wspec(D_EXPERT, D_MODEL, 1)],
            out_specs=pl.BlockSpec(memory_space=pl.ANY),
            scratch_shapes=[pltpu.VMEM((D_MODEL, D_EXPERT), BF16),
                            pltpu.VMEM((D_MODEL, D_EXPERT), BF16),
                            pltpu.VMEM((D_EXPERT, D_MODEL), BF16),
                            pltpu.VMEM((D_MODEL, D_EXPERT), BF16),
                            pltpu.VMEM((D_MODEL, D_EXPERT), BF16),
                            pltpu.VMEM((D_EXPERT, D_MODEL), BF16),
                            pltpu.VMEM((N_BUF, TM, XW), F32),
                            pltpu.VMEM((N_BUF, TM, D_MODEL), F32),
                            pltpu.SMEM((N_SLOTS,), I32),
                            pltpu.SemaphoreType.DMA((N_BUF,)),
                            pltpu.SemaphoreType.DMA((N_BUF,))],
        ),
        compiler_params=pltpu.CompilerParams(
            dimension_semantics=("arbitrary",), vmem_limit_bytes=VMEM_LIMIT,
            has_side_effects=True),
        name="moe_experts",
    )(tea, teb, nu, pos, seg, x1, w1, w3, w2, w1, w3, w2)


def _combine_kernel(x1_ref, p_ref, y_ref, wp_ref, wg_ref, g_ref, b_ref, o_ref):
    x1 = x1_ref[...]
    gate = _sigmoid(jnp.dot(x1.astype(BF16), wg_ref[...], preferred_element_type=F32))
    ple = jnp.dot(p_ref[...].astype(BF16), wp_ref[...], preferred_element_type=F32) * gate
    o_ref[...] = _layer_norm(ALPHA * x1 + y_ref[...] + ple, g_ref[...], b_ref[...])


def _combine(x1, p, ys, wp, wg, g, b):
    full = lambda shape: pl.BlockSpec(shape, lambda i: (0,) * len(shape))
    return pl.pallas_call(
        _combine_kernel,
        out_shape=jax.ShapeDtypeStruct((TOKENS, D_MODEL), F32),
        grid=(TOKENS // RD,),
        in_specs=[pl.BlockSpec((RD, D_MODEL), lambda i: (i, 0)),
                  pl.BlockSpec((RD, PLE_DIM), lambda i: (i, 0)),
                  pl.BlockSpec((RD, D_MODEL), lambda i: (i, 0)),
                  full((PLE_DIM, D_MODEL)),
                  full((D_MODEL, D_MODEL)),
                  full((1, D_MODEL)),
                  full((1, D_MODEL))],
        out_specs=pl.BlockSpec((RD, D_MODEL), lambda i: (i, 0)),
        compiler_params=pltpu.CompilerParams(
            dimension_semantics=("arbitrary",), vmem_limit_bytes=VMEM_LIMIT),
        name="moe_combine",
    )(x1, p, ys, wp, wg, g, b)


def _qkv_kernel(x_ref, w_ref, o_ref):
    o_ref[...] = jnp.dot(x_ref[...].astype(BF16), w_ref[...],
                         preferred_element_type=F32).astype(BF16)


def _qkv_proj(x, w):
    return pl.pallas_call(
        _qkv_kernel,
        out_shape=jax.ShapeDtypeStruct((TOKENS, 3 * D_MODEL), BF16),
        grid=(TOKENS // RP,),
        in_specs=[pl.BlockSpec((RP, D_MODEL), lambda i: (i, 0)),
                  pl.BlockSpec((D_MODEL, 3 * D_MODEL), lambda i: (0, 0))],
        out_specs=pl.BlockSpec((RP, 3 * D_MODEL), lambda i: (i, 0)),
        compiler_params=pltpu.CompilerParams(
            dimension_semantics=("arbitrary",), vmem_limit_bytes=VMEM_LIMIT),
        name="qkv_proj",
    )(x, w)


def _attn_kernel(q_ref, k_ref, v_ref, o_ref, acc, cbuf, bnd, kinf):
    qb = pl.program_id(2)
    lane = lax.broadcasted_iota(I32, (1, 128), 1)
    half = (lane < 64, lane >= 64)

    @pl.when(qb == 0)
    def _():
        for pr in range(ATT_PAIRS):
            ka = jnp.max(jnp.abs(k_ref[:, pr * 128:(pr + 1) * 128].astype(F32)),
                         axis=0, keepdims=True)
            for hh in range(2):
                m = jnp.max(jnp.where(half[hh], ka, 0.0), axis=1, keepdims=True)
                kinf[2 * pr + hh] = jnp.broadcast_to(m, (8, 128))

    qh = []
    for pr in range(ATT_PAIRS):
        q = q_ref[:, pr * 128:(pr + 1) * 128] * jnp.asarray(0.125, BF16)
        zero = jnp.zeros_like(q)
        qh.append((jnp.where(half[0], q, zero), jnp.where(half[1], q, zero)))
        qa = jnp.abs(q.astype(F32))
        for hh in range(2):
            qn = jnp.sum(jnp.where(half[hh], qa, 0.0), axis=1, keepdims=True)
            bnd[2 * pr + hh] = (jnp.broadcast_to(qn, (SB_BLOCK, 128))
                                * kinf[2 * pr + hh][0:1, :] * BOUND_SLACK)
    r = lax.broadcasted_iota(I32, (SB_BLOCK, SB_BLOCK), 0)
    cidx = lax.broadcasted_iota(I32, (SB_BLOCK, SB_BLOCK), 1)
    neg_suffix = jnp.where(r >= cidx, -1.0, 0.0).astype(BF16)
    causal = cidx < r

    acc[...] = jnp.zeros_like(acc)
    cbuf[...] = jnp.zeros_like(cbuf)

    def block(kb, masked):
        rows = pl.ds(pl.multiple_of(kb * SB_BLOCK, SB_BLOCK), SB_BLOCK)
        heads = [(pr, hh) for pr in range(ATT_PAIRS) for hh in range(2)]
        ks = [k_ref[rows, pr * 128:(pr + 1) * 128] for pr in range(ATT_PAIRS)]
        vs = [v_ref[rows, pr * 128:(pr + 1) * 128] for pr in range(ATT_PAIRS)]
        zs = [lax.dot_general(qh[pr][hh], ks[pr], NT_DIMS, preferred_element_type=F32)
              for pr, hh in heads]
        sps = []
        for z in zs:
            sp = jnp.maximum(z, 0.0) + jnp.log(1.0 + jnp.exp2(jnp.abs(z) * (-LOG2E)))
            if masked:
                sp = jnp.where(causal, sp, 0.0)
            sps.append(sp.astype(BF16))
        rss = []
        for idx, sp in enumerate(sps):
            rs = jnp.dot(sp, neg_suffix, preferred_element_type=F32)
            if not masked:
                c = cbuf[idx]
                rs = rs + jnp.concatenate([c, c], axis=1)
            rss.append(rs)
        for idx, (pr, hh) in enumerate(heads):
            w = jnp.exp(zs[idx] + rss[idx])
            if masked:
                w = jnp.where(causal, w, 0.0)
            acc[idx] = acc[idx] + jnp.dot(w.astype(BF16), vs[pr], preferred_element_type=F32)
            cbuf[idx] = jnp.broadcast_to(rss[idx][:, 0:1], (SB_BLOCK, 128))

    def log_weight_bound():
        m = cbuf[0] + bnd[0]
        for idx in range(1, 2 * ATT_PAIRS):
            m = jnp.maximum(m, cbuf[idx] + bnd[idx])
        return jnp.max(m)

    block(qb, True)

    def cond(carry):
        j, m = carry
        return jnp.logical_and(j < qb, m > EXIT_LOG_WEIGHT)

    def body(carry):
        j, _ = carry
        block(qb - 1 - j, False)
        return j + 1, log_weight_bound()

    lax.while_loop(cond, body, (jnp.int32(0), log_weight_bound()))
    for pr in range(ATT_PAIRS):
        o_ref[:, pr * 128:(pr + 1) * 128] = jnp.where(
            lane < 64, acc[2 * pr], acc[2 * pr + 1]).astype(BF16)


def _attention(qkv):
    n_hg = D_MODEL // ATT_LANES
    return pl.pallas_call(
        _attn_kernel,
        out_shape=jax.ShapeDtypeStruct((BATCH, SEQ, D_MODEL), BF16),
        grid=(BATCH, n_hg, SEQ // SB_BLOCK),
        in_specs=[pl.BlockSpec((None, SB_BLOCK, ATT_LANES), lambda b, h, i: (b, i, h)),
                  pl.BlockSpec((None, SEQ, ATT_LANES), lambda b, h, i: (b, 0, n_hg + h)),
                  pl.BlockSpec((None, SEQ, ATT_LANES), lambda b, h, i: (b, 0, 2 * n_hg + h))],
        out_specs=pl.BlockSpec((None, SB_BLOCK, ATT_LANES), lambda b, h, i: (b, i, h)),
        scratch_shapes=[pltpu.VMEM((2 * ATT_PAIRS, SB_BLOCK, 128), F32),
                        pltpu.VMEM((2 * ATT_PAIRS, SB_BLOCK, 128), F32),
                        pltpu.VMEM((2 * ATT_PAIRS, SB_BLOCK, 128), F32),
                        pltpu.VMEM((2 * ATT_PAIRS, 8, 128), F32)],
        compiler_params=pltpu.CompilerParams(
            dimension_semantics=("arbitrary", "arbitrary", "arbitrary"),
            vmem_limit_bytes=VMEM_LIMIT),
        name="sb_attention",
    )(qkv, qkv, qkv)


def _moe_ffn(x1, ri, p_rows, layer, moe_w1, moe_w3, moe_w2, wp, wg, g, b):
    pos, te, seg = _rank(ri)
    tile_seg = te[0, :N_MTILES + N_BUF - 1]
    grp = tile_seg // PAIRS_PER_GROUP
    k = tile_seg % PAIRS_PER_GROUP
    tea = grp * EXPERTS_PER_GROUP + sum(jnp.where(k == kk, a, 0) for kk, a in enumerate(PAIR_A))
    teb = grp * EXPERTS_PER_GROUP + sum(jnp.where(k == kk, b_, 0) for kk, b_ in enumerate(PAIR_B))
    ys = _experts(tea, teb, te[1, :1], pos[0], seg[:, 0], x1, moe_w1, moe_w3, moe_w2, layer)
    return _combine(x1, p_rows, ys, wp, wg, g, b)


def kernel(x, p, ab_w_in, s5_lambda_re, s5_lambda_im, s5_log_dt, s5_b_re, s5_b_im, s5_c_re, s5_c_im, s5_d, s5_w_glu, pool_w, pool_scale, ab_w_out, sb_w_qkv, sb_w_out, ln_mix_g, ln_mix_b, ln_ffn_g, ln_ffn_b, router_w, router_bias, moe_w1, moe_w3, moe_w2, ple_w_proj, ple_w_gate):
    row = lambda a: a.reshape(1, -1)
    rwt = router_w.T
    rwh = rwt.astype(BF16)
    rwl = (rwt - rwh.astype(F32)).astype(BF16)
    rb = router_bias.reshape(N_EXPERTS, 1)

    x_rows = x.reshape(TOKENS, D_MODEL)
    a_re, a_im, bcat = _s5_prep(s5_lambda_re[0], s5_lambda_im[0], s5_log_dt[0],
                                s5_b_re[0], s5_b_im[0])

    def c_blockdiag(c):
        rows = c.reshape(S5_CHUNKS, 8, S5_H, S5_P).transpose(0, 1, 3, 2).reshape(
            S5_CHUNKS, 8 * S5_P, S5_H)
        return _group_blockdiag(rows, S5_P, S5_H)

    ccat = jnp.concatenate([c_blockdiag(s5_c_re[0]), -c_blockdiag(s5_c_im[0])],
                           axis=1).astype(BF16)
    mix_in = _mixer0(x, ab_w_in[0].astype(BF16), bcat, a_re, a_im, ccat,
                     row(s5_d[0]), s5_w_glu[0].astype(BF16), pool_w[0].astype(BF16),
                     row(pool_scale[0])).reshape(TOKENS, D_MODEL)
    x1, ri = _proj_ln_route(mix_in, ab_w_out[0].astype(BF16), x_rows,
                            row(ln_mix_g[0]), row(ln_mix_b[0]), rwh, rwl, rb)
    xb = _moe_ffn(x1, ri, p[0].reshape(TOKENS, PLE_DIM), 0, moe_w1, moe_w3, moe_w2,
                  ple_w_proj[0].astype(BF16), ple_w_gate[0].astype(BF16),
                  row(ln_ffn_g[0]), row(ln_ffn_b[0]))

    qkv = _qkv_proj(xb, sb_w_qkv[0].astype(BF16))
    att = _attention(qkv.reshape(BATCH, SEQ, 3 * D_MODEL)).reshape(TOKENS, D_MODEL)
    x3, ri = _proj_ln_route(att, sb_w_out[0].astype(BF16), xb,
                            row(ln_mix_g[1]), row(ln_mix_b[1]), rwh, rwl, rb)
    x4 = _moe_ffn(x3, ri, p[1].reshape(TOKENS, PLE_DIM), 1, moe_w1, moe_w3, moe_w2,
                  ple_w_proj[1].astype(BF16), ple_w_gate[1].astype(BF16),
                  row(ln_ffn_g[1]), row(ln_ffn_b[1]))
    return x4.reshape(BATCH, SEQ, D_MODEL)
```
